```python
import math
import jax, jax.numpy as jnp
from jax import lax
import numpy as np

D_MODEL = 2048
BATCH = 1
SEQ = 16384
DEPTH = 2

GRID_W = 64
CTX_LEN = 256
N_EVEN = (DEPTH + 1) // 2
N_ODD = DEPTH // 2
EPS = 1e-6
NEG_INF = -1e30
N_MOD = 9

HEAD_DIM = 128
HY_WIDTH = D_MODEL // 2
HY_GROUPS = HY_WIDTH // HEAD_DIM
ATT_WIDTH = D_MODEL - HY_WIDTH
N_Q_HEADS = ATT_WIDTH // HEAD_DIM
N_KV_HEADS = N_Q_HEADS // 4
Q_PER_KV = N_Q_HEADS // N_KV_HEADS
WINDOW = 128
ATT_BLOCK = 128
ROPE_BASE = 10000.0
SHORT_CONV = 3
HY_BANDS = 16
HY_EMB = 1 + 2 * HY_BANDS
HY_FILTER_HIDDEN = 64
HY_DECAY_TARGET = 1e-2
HY_FAST_PCT = 0.3
HY_SLOW_PCT = 1.5
HY_IN = 3 * HY_WIDTH
Q_END = HY_IN + ATT_WIDTH
KV_W = N_KV_HEADS * HEAD_DIM
K_END = Q_END + KV_W
IN_EVEN = K_END + KV_W

CHUNK = 128
SG_WIDTH = D_MODEL
SG_GROUPS = 8
SG_GROUP_DIM = SG_WIDTH // SG_GROUPS

D_FF = 5632

kernel_name = 'hybrid_hyena_swa_gmlp_macaron'


def rmsnorm(x, g):
    xf = x.astype(jnp.float32)
    y = xf * lax.rsqrt(jnp.mean(xf * xf, axis=-1, keepdims=True) + EPS) * g.astype(jnp.float32)
    return y.astype(x.dtype)


def modulate(h, shift, scale):
    return h * (1 + scale) + shift


def swiglu(h, wg, wu, wd):
    return (jax.nn.silu(h @ wg) * (h @ wu)) @ wd


def half_ffn(x, g, shift, scale, gate, wg, wu, wd):
    return x + 0.5 * gate * swiglu(modulate(rmsnorm(x, g), shift, scale), wg, wu, wd)


def short_conv(z, w, b):
    L = z.shape[1]
    p = SHORT_CONV // 2
    zp = jnp.pad(z, ((0, 0), (p, p), (0, 0)))
    return sum(zp[:, j:j + L] * w[j] for j in range(SHORT_CONV)) + b


def hyena_filter(L, w1, b1, f1, w2, b2, f2, w3):
    f32 = jnp.float32
    t = jnp.linspace(0.0, 1.0, L, dtype=f32)[:, None]
    w = (2.0 * math.pi / L) * jnp.arange(L, dtype=f32)[:, None]
    bands = jnp.linspace(1e-4, HY_BANDS - 1, HY_BANDS, dtype=f32)[None, :]
    z = jnp.concatenate([t, jnp.cos(bands * w), -jnp.sin(bands * w)], axis=-1)
    h = jnp.sin(f1.astype(f32) * (z @ w1.astype(f32) + b1.astype(f32)))
    h = jnp.sin(f2.astype(f32) * (h @ w2.astype(f32) + b2.astype(f32)))
    h = (h @ w3.astype(f32)).reshape(L, 2, HY_WIDTH)
    lt = math.log(HY_DECAY_TARGET)
    deltas = jnp.abs(jnp.linspace(lt / HY_SLOW_PCT, lt / HY_FAST_PCT, HY_WIDTH, dtype=f32))
    h = h * jnp.exp(-t * deltas)[:, None, :]
    return h[:, 0], h[:, 1]


def bidir_long_conv(v, h_fwd, h_bwd, skip):
    B, L, C = v.shape
    k = jnp.concatenate([h_fwd, jnp.zeros((1, C), jnp.float32), h_bwd[:0:-1]], axis=0)
    k = k * lax.rsqrt(jnp.sum(k * k, axis=0, keepdims=True) + EPS)
    vf = v.astype(jnp.float32)
    y = jnp.fft.irfft(jnp.fft.rfft(vf, n=2 * L, axis=1) * jnp.fft.rfft(k, axis=0)[None], n=2 * L, axis=1)[:, :L]
    return (y + vf * skip.astype(jnp.float32)).astype(v.dtype)


def axial_rope(L):
    t = jnp.arange(L, dtype=jnp.int32)
    row = (t // GRID_W).astype(jnp.float32)
    col = (t % GRID_W).astype(jnp.float32)
    n = HEAD_DIM // 4
    inv = ROPE_BASE ** (-jnp.arange(n, dtype=jnp.float32) / n)
    ang = jnp.concatenate([row[:, None] * inv, col[:, None] * inv], axis=-1)
    return jnp.cos(ang), jnp.sin(ang)


def apply_rope(x, cos, sin):
    xf = x.astype(jnp.float32)
    half = HEAD_DIM // 2
    x1, x2 = xf[..., :half], xf[..., half:]
    c, s = cos[None, :, None, :], sin[None, :, None, :]
    return jnp.concatenate([x1 * c - x2 * s, x2 * c + x1 * s], axis=-1).astype(x.dtype)


def band(a):
    B, L = a.shape[:2]
    nb = L // ATT_BLOCK
    ap = jnp.pad(a, ((0, 0), (ATT_BLOCK, ATT_BLOCK), (0, 0), (0, 0))).reshape(B, nb + 2, ATT_BLOCK, *a.shape[2:])
    return jnp.concatenate([ap[:, :-2], ap[:, 1:-1], ap[:, 2:]], axis=2)


def window_attention(q, k, v, kc, vc, sink):
    B, L = q.shape[:2]
    nb = L // ATT_BLOCK
    f32 = jnp.float32
    scale = HEAD_DIM ** -0.5
    qb = q.reshape(B, nb, ATT_BLOCK, N_KV_HEADS, Q_PER_KV, HEAD_DIM)
    kb, vb = band(k), band(v)
    s_loc = jnp.einsum('bnqkgd,bnjkd->bnkgqj', qb, kb, preferred_element_type=f32) * scale
    s_ctx = jnp.einsum('bnqkgd,bckd->bnkgqc', qb, kc, preferred_element_type=f32) * scale
    off = jnp.arange(ATT_BLOCK)[:, None]
    cidx = jnp.arange(3 * ATT_BLOCK)[None, :]
    kpos = (jnp.arange(nb) * ATT_BLOCK - ATT_BLOCK)[:, None, None] + cidx[None]
    mask = (jnp.abs(cidx - ATT_BLOCK - off) <= WINDOW)[None] & (kpos >= 0) & (kpos < L)
    s_loc = jnp.where(mask[None, :, None, None], s_loc, NEG_INF)
    sink_l = jnp.broadcast_to(sink.astype(f32).reshape(N_KV_HEADS, Q_PER_KV, 1, 1), s_ctx.shape[:-1] + (1,))
    p = jax.nn.softmax(jnp.concatenate([sink_l, s_ctx, s_loc], axis=-1), axis=-1)
    n_ctx = kc.shape[1]
    p_ctx = p[..., 1:1 + n_ctx].astype(v.dtype)
    p_loc = p[..., 1 + n_ctx:].astype(v.dtype)
    o = (jnp.einsum('bnkgqc,bckd->bnqkgd', p_ctx, vc, preferred_element_type=f32)
         + jnp.einsum('bnkgqj,bnjkd->bnqkgd', p_loc, vb, preferred_element_type=f32))
    return o.reshape(B, L, ATT_WIDTH).astype(v.dtype)


def context_attention(qc, kc, vc, sink):
    B, C = qc.shape[:2]
    f32 = jnp.float32
    s = jnp.einsum('bqkgd,bckd->bkgqc', qc, kc, preferred_element_type=f32) * HEAD_DIM ** -0.5
    sink_l = jnp.broadcast_to(sink.astype(f32).reshape(N_KV_HEADS, Q_PER_KV, 1, 1), s.shape[:-1] + (1,))
    p = jax.nn.softmax(jnp.concatenate([sink_l, s], axis=-1), axis=-1)[..., 1:].astype(vc.dtype)
    o = jnp.einsum('bkgqc,bckd->bqkgd', p, vc, preferred_element_type=f32)
    return o.reshape(B, C, ATT_WIDTH).astype(vc.dtype)


def even_mixer(z, kc, vc, conv_w, conv_b, w1, b1, f1, w2, b2, f2, w3, skip, sink, latent):
    B, L, _ = z.shape
    hy = short_conv(z[..., :HY_IN], conv_w, conv_b)
    x0, x1, hv = hy[..., :HY_WIDTH], hy[..., HY_WIDTH:2 * HY_WIDTH], hy[..., 2 * HY_WIDTH:]
    h_fwd, h_bwd = hyena_filter(L, w1, b1, f1, w2, b2, f2, w3)
    y_hy = x0 * bidir_long_conv(hv * x1, h_fwd, h_bwd, skip)
    q = z[..., HY_IN:Q_END].reshape(B, L, N_Q_HEADS, HEAD_DIM)
    if latent:
        cos, sin = axial_rope(L)
        q = apply_rope(q, cos, sin).reshape(B, L, N_KV_HEADS, Q_PER_KV, HEAD_DIM)
        k = apply_rope(z[..., Q_END:K_END].reshape(B, L, N_KV_HEADS, HEAD_DIM), cos, sin)
        va = z[..., K_END:].reshape(B, L, N_KV_HEADS, HEAD_DIM)
        y_att = window_attention(q, k, va, kc, vc, sink)
    else:
        y_att = context_attention(q.reshape(B, L, N_KV_HEADS, Q_PER_KV, HEAD_DIM), kc, vc, sink)
    return jnp.concatenate([y_hy, y_att], axis=-1)


def spatial_gating(z, g, ws, bs):
    B, L, _ = z.shape
    z = jax.nn.gelu(z)
    u, v = z[..., :SG_WIDTH], rmsnorm(z[..., SG_WIDTH:], g)
    vr = v.reshape(B, L // CHUNK, CHUNK, SG_GROUPS, SG_GROUP_DIM)
    mixed = jnp.einsum('gpq,bnqgd->bnpgd', ws, vr) + bs.T[:, :, None]
    return u * mixed.reshape(B, L, SG_WIDTH)


def setup_inputs(seed: int = 0) -> dict:
    key = jax.random.key(seed)
    ks = jax.random.split(key, 32)
    f32 = jnp.float32
    D = D_MODEL

    def nrm(k, shape, s):
        return s * jax.random.normal(k, shape, f32)

    return {
        'x': nrm(ks[0], (BATCH, SEQ, D), 1.0),
        'c': nrm(ks[1], (BATCH, D), 1.0),
        'ctx': nrm(ks[2], (BATCH, CTX_LEN, D), 1.0),
        'c_ctx': nrm(ks[3], (D,), 1.0),
        'ada_w': nrm(ks[4], (DEPTH, D, N_MOD * D), 0.5 * D ** -0.5),
        'ada_b': nrm(ks[5], (DEPTH, N_MOD * D), 0.01),
        'norm_g': 1.0 + nrm(ks[6], (DEPTH, 3, D), 0.02),
        'ffn_wg': nrm(ks[7], (DEPTH, 2, D, D_FF), D ** -0.5),
        'ffn_wu': nrm(ks[8], (DEPTH, 2, D, D_FF), D ** -0.5),
        'ffn_wd': nrm(ks[9], (DEPTH, 2, D_FF, D), D_FF ** -0.5),
        'ev_w_in': nrm(ks[10], (N_EVEN, D, IN_EVEN), D ** -0.5),
        'ev_conv_w': nrm(ks[11], (N_EVEN, SHORT_CONV, HY_IN), SHORT_CONV ** -0.5),
        'ev_conv_b': nrm(ks[12], (N_EVEN, HY_IN), 0.01),
        'hy_w1': nrm(ks[13], (N_EVEN, HY_EMB, HY_FILTER_HIDDEN), HY_EMB ** -0.5),
        'hy_b1': nrm(ks[14], (N_EVEN, HY_FILTER_HIDDEN), 0.1),
        'hy_f1': 1.0 + nrm(ks[15], (N_EVEN, HY_FILTER_HIDDEN), 0.02),
        'hy_w2': nrm(ks[16], (N_EVEN, HY_FILTER_HIDDEN, HY_FILTER_HIDDEN), HY_FILTER_HIDDEN ** -0.5),
        'hy_b2': nrm(ks[17], (N_EVEN, HY_FILTER_HIDDEN), 0.1),
        'hy_f2': 1.0 + nrm(ks[18], (N_EVEN, HY_FILTER_HIDDEN), 0.02),
        'hy_w3': nrm(ks[19], (N_EVEN, HY_FILTER_HIDDEN, 2 * HY_WIDTH), HY_FILTER_HIDDEN ** -0.5),
        'hy_skip': nrm(ks[20], (N_EVEN, HY_WIDTH), 1.0),
        'att_sink': nrm(ks[21], (N_EVEN, N_Q_HEADS), 1.0),
        'ev_w_out': nrm(ks[22], (N_EVEN, D, D), D ** -0.5),
        'od_w_in': nrm(ks[23], (N_ODD, D, 2 * SG_WIDTH), D ** -0.5),
        'sg_g': 1.0 + nrm(ks[24], (N_ODD, SG_WIDTH), 0.02),
        'sg_ws': nrm(ks[25], (N_ODD, SG_GROUPS, CHUNK, CHUNK), CHUNK ** -0.5),
        'sg_bs': 1.0 + nrm(ks[26], (N_ODD, SG_GROUPS, CHUNK), 0.02),
        'od_w_out': nrm(ks[27], (N_ODD, SG_WIDTH, D), SG_WIDTH ** -0.5),
        'final_g': 1.0 + nrm(ks[28], (D,), 0.02),
    }


def reference(x, c, ctx, c_ctx, ada_w, ada_b, norm_g, ffn_wg, ffn_wu, ffn_wd,
              ev_w_in, ev_conv_w, ev_conv_b, hy_w1, hy_b1, hy_f1, hy_w2, hy_b2, hy_f2, hy_w3, hy_skip,
              att_sink, ev_w_out, od_w_in, sg_g, sg_ws, sg_bs, od_w_out, final_g):
    B = x.shape[0]
    s_lat = jax.nn.silu(c)
    s_ctx = jax.nn.silu(c_ctx)
    xc = ctx
    for layer in range(DEPTH):
        even = layer % 2 == 0
        li = layer // 2
        ctx_out = any(j % 2 == 0 for j in range(layer + 1, DEPTH))
        ctx_in = even or ctx_out
        mod = (s_lat @ ada_w[layer] + ada_b[layer]).reshape(B, N_MOD, 1, D_MODEL)
        mc = (s_ctx @ ada_w[layer] + ada_b[layer]).reshape(N_MOD, D_MODEL)
        g = norm_g[layer]
        wg, wu, wd = ffn_wg[layer], ffn_wu[layer], ffn_wd[layer]

        x = half_ffn(x, g[0], mod[:, 0], mod[:, 1], mod[:, 2], wg[0], wu[0], wd[0])
        if ctx_in:
            xc = half_ffn(xc, g[0], mc[0], mc[1], mc[2], wg[0], wu[0], wd[0])

        h = modulate(rmsnorm(x, g[1]), mod[:, 3], mod[:, 4])
        if even:
            w_in = ev_w_in[li]
            prm = (ev_conv_w[li], ev_conv_b[li], hy_w1[li], hy_b1[li], hy_f1[li], hy_w2[li], hy_b2[li],
                   hy_f2[li], hy_w3[li], hy_skip[li], att_sink[li])
            hc = modulate(rmsnorm(xc, g[1]), mc[3], mc[4])
            if ctx_out:
                zc = hc @ w_in
                kvc = zc[..., Q_END:]
            else:
                kvc = hc @ w_in[:, Q_END:]
            kc = kvc[..., :KV_W].reshape(B, -1, N_KV_HEADS, HEAD_DIM)
            vc = kvc[..., KV_W:].reshape(B, -1, N_KV_HEADS, HEAD_DIM)
            y = even_mixer(h @ w_in, kc, vc, *prm, latent=True) @ ev_w_out[li]
            if ctx_out:
                xc = xc + mc[5] * (even_mixer(zc, kc, vc, *prm, latent=False) @ ev_w_out[li])
        else:
            y = spatial_gating(h @ od_w_in[li], sg_g[li], sg_ws[li], sg_bs[li]) @ od_w_out[li]
            if ctx_out:
                hc = modulate(rmsnorm(xc, g[1]), mc[3], mc[4])
                xc = xc + mc[5] * (spatial_gating(hc @ od_w_in[li], sg_g[li], sg_ws[li], sg_bs[li]) @ od_w_out[li])
        x = x + mod[:, 5] * y

        x = half_ffn(x, g[2], mod[:, 6], mod[:, 7], mod[:, 8], wg[1], wu[1], wd[1])
        if ctx_out:
            xc = half_ffn(xc, g[2], mc[6], mc[7], mc[8], wg[1], wu[1], wd[1])
    return rmsnorm(x, final_g)
```

```python
import functools
import math

import numpy as np
import jax
import jax.numpy as jnp
from jax import lax
from jax.experimental import pallas as pl
from jax.experimental.pallas import tpu as pltpu

F32 = jnp.float32
BF16 = jnp.bfloat16

LANES = 128
SUBLANES = 8
VMEM_LIMIT_BYTES = 56 * 1024 * 1024

EPS = 1e-6
NEG_INF = -1e30
N_MOD = 9
HEAD_DIM = 128
N_KV_HEADS = 2
Q_PER_KV = 4
ATT_BLOCK = 128
GRID_W = 64
ROPE_BASE = 10000.0
HY_BANDS = 16
HY_DECAY_TARGET = 1e-2
HY_FAST_PCT = 0.3
HY_SLOW_PCT = 1.5
SG_GROUPS = 8
CHUNK = 128
HY_GROUP = SUBLANES


def _params(*sem):
    return pltpu.CompilerParams(dimension_semantics=sem, vmem_limit_bytes=VMEM_LIMIT_BYTES)


def _norm_mod(x, g, shift, scale):
    y = x * lax.rsqrt(jnp.mean(x * x, axis=-1, keepdims=True) + EPS) * g
    return y * (1.0 + scale) + shift


def _ada_kernel(s_ref, w_ref, b_ref, o_ref):
    s = s_ref[...]
    s = (s * jax.nn.sigmoid(s)).astype(BF16)
    o_ref[0] = jnp.dot(s, w_ref[0].astype(BF16), preferred_element_type=F32) + b_ref[0]


def ada_mods(cond8, ada_w, ada_b, tn=1024):
    depth, d, n = ada_w.shape
    return pl.pallas_call(
        _ada_kernel,
        grid=(depth, n // tn),
        in_specs=[
            pl.BlockSpec((SUBLANES, d), lambda l, j: (0, 0)),
            pl.BlockSpec((1, d, tn), lambda l, j: (l, 0, j)),
            pl.BlockSpec((1, 1, tn), lambda l, j: (l, 0, j)),
        ],
        out_specs=pl.BlockSpec((1, SUBLANES, tn), lambda l, j: (l, 0, j)),
        out_shape=jax.ShapeDtypeStruct((depth, SUBLANES, n), F32),
        compiler_params=_params("parallel", "parallel"),
        name="ada_mods",
    )(cond8, ada_w, ada_b.reshape(depth, 1, n))


def _ffn_kernel(x_ref, g_ref, sh_ref, sc_ref, gt_ref, wg_ref, wu_ref, wd_ref, *rest, final):
    if final:
        fg_ref, o_ref, h_scr, acc_scr = rest
    else:
        o_ref, h_scr, acc_scr = rest
    f = pl.program_id(1)

    @pl.when(f == 0)
    def _():
        h_scr[...] = _norm_mod(x_ref[...], g_ref[...], sh_ref[...], sc_ref[...]).astype(BF16)
        acc_scr[...] = jnp.zeros_like(acc_scr)

    h = h_scr[...]
    gate_act = jnp.dot(h, wg_ref[...], preferred_element_type=F32)
    up = jnp.dot(h, wu_ref[...], preferred_element_type=F32)
    a = (gate_act * jax.nn.sigmoid(gate_act) * up).astype(BF16)
    acc_scr[...] += jnp.dot(a, wd_ref[...], preferred_element_type=F32)

    @pl.when(f == pl.num_programs(1) - 1)
    def _():
        out = x_ref[...] + (0.5 * gt_ref[...]) * acc_scr[...]
        if final:
            out = out * lax.rsqrt(jnp.mean(out * out, axis=-1, keepdims=True) + EPS) * fg_ref[...]
        o_ref[...] = out


def half_ffn(x, g, shift, scale, gate, wg, wu, wd, final_g=None, tm=512, tf=512):
    rows, d = x.shape
    dff = wg.shape[1]
    tm = min(tm, rows)
    vec = pl.BlockSpec((1, d), lambda i, f: (0, 0))
    in_specs = [
        pl.BlockSpec((tm, d), lambda i, f: (i, 0)),
        vec, vec, vec, vec,
        pl.BlockSpec((d, tf), lambda i, f: (0, f)),
        pl.BlockSpec((d, tf), lambda i, f: (0, f)),
        pl.BlockSpec((tf, d), lambda i, f: (f, 0)),
    ]
    args = [x, g, shift, scale, gate, wg, wu, wd]
    if final_g is not None:
        in_specs.append(vec)
        args.append(final_g)
    return pl.pallas_call(
        functools.partial(_ffn_kernel, final=final_g is not None),
        grid=(rows // tm, dff // tf),
        in_specs=in_specs,
        out_specs=pl.BlockSpec((tm, d), lambda i, f: (i, 0)),
        out_shape=jax.ShapeDtypeStruct((rows, d), F32),
        scratch_shapes=[pltpu.VMEM((tm, d), BF16), pltpu.VMEM((tm, d), F32)],
        compiler_params=_params("parallel", "arbitrary"),
        name="half_ffn",
    )(*args)


def _inproj_kernel(x_ref, g_ref, sh_ref, sc_ref, w_ref, o_ref, h_scr, *, gelu, transposed):
    @pl.when(pl.program_id(1) == 0)
    def _():
        h_scr[...] = _norm_mod(x_ref[...], g_ref[...], sh_ref[...], sc_ref[...]).astype(BF16)

    h = h_scr[...]
    if transposed:
        zt = lax.dot_general(w_ref[...], h, (((1,), (1,)), ((), ())), preferred_element_type=F32)
        ct, tm = zt.shape
        for r in range(tm // LANES):
            o_ref[:, r * SUBLANES:(r + 1) * SUBLANES, :] = zt[:, r * LANES:(r + 1) * LANES].reshape(
                ct // SUBLANES, SUBLANES, LANES)
    else:
        z = jnp.dot(h, w_ref[...], preferred_element_type=F32)
        if gelu:
            z = jax.nn.gelu(z, approximate=True)
        o_ref[...] = z.astype(o_ref.dtype)


def in_proj(x, g, shift, scale, w, gelu=False, tm=1024, tn=512):
    rows, d = x.shape
    n = w.shape[1]
    tm = min(tm, rows)
    vec = pl.BlockSpec((1, d), lambda i, j: (0, 0))
    return pl.pallas_call(
        functools.partial(_inproj_kernel, gelu=gelu, transposed=False),
        grid=(rows // tm, n // tn),
        in_specs=[pl.BlockSpec((tm, d), lambda i, j: (i, 0)), vec, vec, vec,
                  pl.BlockSpec((d, tn), lambda i, j: (0, j))],
        out_specs=pl.BlockSpec((tm, tn), lambda i, j: (i, j)),
        out_shape=jax.ShapeDtypeStruct((rows, n), F32),
        scratch_shapes=[pltpu.VMEM((tm, d), BF16)],
        compiler_params=_params("parallel", "arbitrary"),
        name="in_proj",
    )(x, g, shift, scale, w)


def in_proj_transposed(x, g, shift, scale, wt, tm=1024, ct=512):
    rows, d = x.shape
    n = wt.shape[0]
    tm = min(tm, rows)
    vec = pl.BlockSpec((1, d), lambda i, j: (0, 0))
    return pl.pallas_call(
        functools.partial(_inproj_kernel, gelu=False, transposed=True),
        grid=(rows // tm, n // ct),
        in_specs=[pl.BlockSpec((tm, d), lambda i, j: (i, 0)), vec, vec, vec,
                  pl.BlockSpec((ct, d), lambda i, j: (j, 0))],
        out_specs=pl.BlockSpec((ct // SUBLANES, SUBLANES * tm // LANES, LANES), lambda i, j: (j, i, 0)),
        out_shape=jax.ShapeDtypeStruct((n // SUBLANES, SUBLANES * rows // LANES, LANES), F32),
        scratch_shapes=[pltpu.VMEM((tm, d), BF16)],
        compiler_params=_params("parallel", "arbitrary"),
        name="in_proj_t",
    )(x, g, shift, scale, wt)


def _filter_kernel(z_ref, t_ref, w1_ref, b1_ref, f1_ref, w2_ref, b2_ref, f2_ref, w3_ref, dl_ref, o_ref):
    hi = lax.Precision.HIGHEST
    ct = w3_ref.shape[0]
    for r in range(z_ref.shape[1] // LANES):
        sl = slice(r * LANES, (r + 1) * LANES)
        a1 = jnp.dot(w1_ref[...], z_ref[:, sl], precision=hi, preferred_element_type=F32)
        h1 = jnp.sin(f1_ref[...] * (a1 + b1_ref[...]))
        a2 = jnp.dot(w2_ref[...], h1, precision=hi, preferred_element_type=F32)
        h2 = jnp.sin(f2_ref[...] * (a2 + b2_ref[...]))
        h3 = jnp.dot(w3_ref[...], h2, precision=hi, preferred_element_type=F32)
        dec = jnp.exp(-(t_ref[0:1, sl] * dl_ref[...]))
        o_ref[:, r * SUBLANES:(r + 1) * SUBLANES, :] = (h3 * dec).reshape(ct // SUBLANES, SUBLANES, LANES)


def hyena_filter_t(L, w1, b1, f1, w2, b2, f2, w3, tl=1024, ct=512):
    hid = w1.shape[1]
    c2 = w3.shape[1]
    width = c2 // 2
    t = jnp.linspace(0.0, 1.0, L, dtype=F32)[:, None]
    w = (2.0 * math.pi / L) * jnp.arange(L, dtype=F32)[:, None]
    bands = jnp.linspace(1e-4, HY_BANDS - 1, HY_BANDS, dtype=F32)[None, :]
    z = jnp.concatenate([t, jnp.cos(bands * w), -jnp.sin(bands * w)], axis=-1)
    emb = z.shape[1]
    embp = -(-emb // SUBLANES) * SUBLANES
    zt = jnp.pad(z.T, ((0, embp - emb), (0, 0)))
    w1t = jnp.pad(w1.astype(F32).T, ((0, 0), (0, embp - emb)))
    lt = math.log(HY_DECAY_TARGET)
    deltas = jnp.abs(jnp.linspace(lt / HY_SLOW_PCT, lt / HY_FAST_PCT, width, dtype=F32))
    deltas2 = jnp.concatenate([deltas, deltas])
    col = lambda v: jnp.broadcast_to(v.astype(F32)[:, None], (v.shape[0], LANES))
    trow = jnp.broadcast_to(t.T, (SUBLANES, L))
    tl = min(tl, L)
    full = lambda shape: pl.BlockSpec(shape, lambda i, j: (0,) * len(shape))
    return pl.pallas_call(
        _filter_kernel,
        grid=(L // tl, c2 // ct),
        in_specs=[
            pl.BlockSpec((embp, tl), lambda i, j: (0, i)),
            pl.BlockSpec((SUBLANES, tl), lambda i, j: (0, i)),
            full((hid, embp)), full((hid, LANES)), full((hid, LANES)),
            full((hid, hid)), full((hid, LANES)), full((hid, LANES)),
            pl.BlockSpec((ct, hid), lambda i, j: (j, 0)),
            pl.BlockSpec((ct, LANES), lambda i, j: (j, 0)),
        ],
        out_specs=pl.BlockSpec((ct // SUBLANES, SUBLANES * tl // LANES, LANES), lambda i, j: (j, i, 0)),
        out_shape=jax.ShapeDtypeStruct((c2 // SUBLANES, SUBLANES * L // LANES, LANES), F32),
        compiler_params=_params("parallel", "parallel"),
        name="hyena_filter",
    )(zt, trow, w1t, col(b1), col(f1), w2.astype(F32).T, col(b2), col(f2), w3.astype(F32).T, col(deltas2))


def _dft_constants(h1):
    n_outer = 2 * h1
    n = n_outer * LANES
    kp = -(-(h1 + 1) // SUBLANES) * SUBLANES
    k1 = np.arange(kp)[:, None]
    n1 = np.arange(h1)[None, :]
    ang_a = 2.0 * np.pi * ((k1 * n1) % n_outer) / n_outer
    fa = np.concatenate([np.cos(ang_a), -np.sin(ang_a)], axis=0)
    n2 = np.arange(LANES)[None, :]
    ang_t = 2.0 * np.pi * ((k1 * n2) % n) / n
    twr, twi = np.cos(ang_t), -np.sin(ang_t)
    a = np.arange(LANES)
    ang_b = 2.0 * np.pi * ((a[:, None] * a[None, :]) % LANES) / LANES
    cb, sb = np.cos(ang_b), np.sin(ang_b)
    fb = np.block([[cb, -sb], [sb, cb]])
    gb = np.block([[cb, sb], [-sb, cb]])
    wk = np.where((k1 == 0) | (k1 == h1), 1.0, 2.0) * (k1 <= h1)
    ga = np.concatenate([(wk * np.cos(ang_a)).T, (-wk * np.sin(ang_a)).T], axis=1)
    as_bf = lambda m: jnp.asarray(m, dtype=F32).astype(BF16)
    return (as_bf(fa), jnp.asarray(twr, F32), jnp.asarray(twi, F32), as_bf(fb), as_bf(gb), as_bf(ga), kp, n)


def _hyena_kernel(scal_ref, z0_ref, z1_ref, zv_ref, hf_ref, hb_ref, fa_ref, twr_ref, twi_ref, fb_ref, gb_ref,
                  ga_ref, o_ref, *, h1, kp, inv_n):
    grp = pl.program_id(0)
    row = lax.broadcasted_iota(jnp.int32, (h1, LANES), 0)
    lane = lax.broadcasted_iota(jnp.int32, (h1, LANES), 1)
    first = (row == 0) & (lane == 0)
    last = (row == h1 - 1) & (lane == LANES - 1)

    def chan(ref, ci):
        return ref[0, pl.ds(ci, h1, stride=SUBLANES), :]

    def prev(x):
        r = pltpu.roll(x, 1, 1)
        r = jnp.where(lane == 0, pltpu.roll(r, 1, 0), r)
        return jnp.where(first, 0.0, r)

    def nxt(x):
        r = pltpu.roll(x, LANES - 1, 1)
        r = jnp.where(lane == LANES - 1, pltpu.roll(r, h1 - 1, 0), r)
        return jnp.where(last, 0.0, r)

    us, x0s, hfs, hbs, nrm = [], [], [], [], []
    for ci in range(HY_GROUP):
        c = grp * HY_GROUP + ci

        def sconv(ref, s):
            x = chan(ref, ci)
            return (scal_ref[3 * s, c] * prev(x) + scal_ref[3 * s + 1, c] * x
                    + scal_ref[3 * s + 2, c] * nxt(x) + scal_ref[9 + s, c])

        x0 = sconv(z0_ref, 0)
        x1 = sconv(z1_ref, 1)
        hv = sconv(zv_ref, 2)
        us.append(hv * x1)
        x0s.append(x0)
        hf = chan(hf_ref, ci)
        hb = jnp.where(first, 0.0, chan(hb_ref, ci))
        hfs.append(hf)
        hbs.append(hb)
        ssq = jnp.sum(hf * hf, keepdims=True) + jnp.sum(hb * hb, keepdims=True)
        nrm.append(lax.rsqrt(ssq + EPS))

    fa = fa_ref[...]
    twr, twi = twr_ref[...], twi_ref[...]

    def outer_fwd(mats):
        xc = jnp.concatenate([m.astype(BF16) for m in mats], axis=1)
        a = jnp.dot(fa, xc, preferred_element_type=F32)
        out = []
        for ci in range(HY_GROUP):
            ar = a[:kp, ci * LANES:(ci + 1) * LANES]
            ai = a[kp:, ci * LANES:(ci + 1) * LANES]
            out.append(jnp.concatenate([ar * twr - ai * twi, ar * twi + ai * twr], axis=1))
        return out

    stacked = jnp.concatenate(outer_fwd(us) + outer_fwd(hfs) + outer_fwd(hbs), axis=0).astype(BF16)
    spec = jnp.dot(stacked, fb_ref[...], preferred_element_type=F32)

    prod = []
    for ci in range(HY_GROUP):
        xu = spec[ci * kp:(ci + 1) * kp]
        xf = spec[(HY_GROUP + ci) * kp:(HY_GROUP + ci + 1) * kp]
        xb = spec[(2 * HY_GROUP + ci) * kp:(2 * HY_GROUP + ci + 1) * kp]
        xr, xi = xu[:, :LANES], xu[:, LANES:]
        kr = xf[:, :LANES] + xb[:, :LANES]
        ki = xf[:, LANES:] - xb[:, LANES:]
        prod.append(jnp.concatenate([xr * kr - xi * ki, xr * ki + xi * kr], axis=1))
    inner = jnp.dot(jnp.concatenate(prod, axis=0).astype(BF16), gb_ref[...], preferred_element_type=F32)

    cols = []
    for ci in range(HY_GROUP):
        b = inner[ci * kp:(ci + 1) * kp]
        br, bi = b[:, :LANES], b[:, LANES:]
        cols.append(jnp.concatenate([br * twr + bi * twi, bi * twr - br * twi], axis=0))
    y = jnp.dot(ga_ref[...], jnp.concatenate(cols, axis=1).astype(BF16), preferred_element_type=F32)

    for ci in range(HY_GROUP):
        c = grp * HY_GROUP + ci
        yc = y[:, ci * LANES:(ci + 1) * LANES] * (nrm[ci] * inv_n)
        o_ref[0, pl.ds(ci, h1, stride=SUBLANES), :] = x0s[ci] * (yc + us[ci] * scal_ref[12, c])


def hyena_mix(zt, ht, conv_w, conv_b, skip):
    ng3, r8, _ = zt.shape
    ng = ng3 // 3
    h1 = r8 // SUBLANES
    width = ng * SUBLANES
    fa, twr, twi, fb, gb, ga, kp, n = _dft_constants(h1)
    scal = jnp.concatenate([
        conv_w[:, 0:width], conv_w[:, width:2 * width], conv_w[:, 2 * width:3 * width],
        conv_b.reshape(3, width), skip.reshape(1, width)], axis=0).astype(F32)
    blk = lambda off: pl.BlockSpec((1, r8, LANES), lambda g, off=off: (g + off, 0, 0))
    full = lambda a: pl.BlockSpec(a.shape, lambda g: (0,) * a.ndim)
    return pl.pallas_call(
        functools.partial(_hyena_kernel, h1=h1, kp=kp, inv_n=1.0 / n),
        grid=(ng,),
        in_specs=[pl.BlockSpec(memory_space=pltpu.SMEM),
                  blk(0), blk(ng), blk(2 * ng), blk(0), blk(ng),
                  full(fa), full(twr), full(twi), full(fb), full(gb), full(ga)],
        out_specs=pl.BlockSpec((1, r8, LANES), lambda g: (g, 0, 0)),
        out_shape=jax.ShapeDtypeStruct((ng, r8, LANES), F32),
        compiler_params=_params("parallel"),
        name="hyena_mix",
    )(scal, zt, zt, zt, ht, ht, fa, twr, twi, fb, gb, ga)


def _attn_kernel(sink_ref, q_ref, kp_ref, k0_ref, kn_ref, vp_ref, v0_ref, vn_ref, ccp_ref, cc0_ref, ccn_ref,
                 ssp_ref, ss0_ref, ssn_ref, kc_ref, vc_ref, o_ref, *, scale):
    i = pl.program_id(0)
    nblk = pl.num_programs(0)
    hd, blk = HEAD_DIM, ATT_BLOCK
    nt = (((1,), (1,)), ((), ()))

    def rope(x, cc, ss):
        return x * cc + pltpu.roll(x, hd // 2, 1) * ss

    rows = Q_PER_KV * blk
    off = lax.broadcasted_iota(jnp.int32, (rows, 3 * blk), 0) & (blk - 1)
    cidx = lax.broadcasted_iota(jnp.int32, (rows, 3 * blk), 1)
    rel = cidx - blk - off
    kpos = (i - 1) * blk + cidx
    mask = (jnp.abs(rel) <= blk) & (kpos >= 0) & (kpos < nblk * blk)
    head_of_row = lax.broadcasted_iota(jnp.int32, (rows, 1), 0) // blk

    for g in range(N_KV_HEADS):
        gs = slice(g * hd, (g + 1) * hd)
        kb = jnp.concatenate([
            rope(kp_ref[:, gs], ccp_ref[...], ssp_ref[...]),
            rope(k0_ref[:, gs], cc0_ref[...], ss0_ref[...]),
            rope(kn_ref[:, gs], ccn_ref[...], ssn_ref[...])], axis=0).astype(BF16)
        vb = jnp.concatenate([vp_ref[:, gs], v0_ref[:, gs], vn_ref[:, gs]], axis=0).astype(BF16)
        kcg = kc_ref[:, gs].astype(BF16)
        vcg = vc_ref[:, gs].astype(BF16)
        q4 = jnp.concatenate([
            rope(q_ref[:, (g * Q_PER_KV + h) * hd:(g * Q_PER_KV + h + 1) * hd], cc0_ref[...], ss0_ref[...])
            for h in range(Q_PER_KV)], axis=0).astype(BF16)
        s_loc = lax.dot_general(q4, kb, nt, preferred_element_type=F32) * scale
        s_ctx = lax.dot_general(q4, kcg, nt, preferred_element_type=F32) * scale
        s_loc = jnp.where(mask, s_loc, NEG_INF)
        sink = jnp.zeros((rows, 1), F32)
        for h in range(Q_PER_KV):
            sink = jnp.where(head_of_row == h, sink_ref[g * Q_PER_KV + h], sink)
        m = jnp.maximum(jnp.maximum(jnp.max(s_loc, axis=-1, keepdims=True),
                                    jnp.max(s_ctx, axis=-1, keepdims=True)), sink)
        p_loc = jnp.exp(s_loc - m)
        p_ctx = jnp.exp(s_ctx - m)
        den = jnp.exp(sink - m) + jnp.sum(p_loc, axis=-1, keepdims=True) + jnp.sum(p_ctx, axis=-1, keepdims=True)
        inv = 1.0 / den
        o = (jnp.dot((p_ctx * inv).astype(BF16), vcg, preferred_element_type=F32)
             + jnp.dot((p_loc * inv).astype(BF16), vb, preferred_element_type=F32))
        for h in range(Q_PER_KV):
            o_ref[:, (g * Q_PER_KV + h) * hd:(g * Q_PER_KV + h + 1) * hd] = o[h * blk:(h + 1) * blk].astype(o_ref.dtype)


def window_attention(qkv, kvc, sink):
    L = qkv.shape[0]
    n_ctx = kvc.shape[0]
    nb = L // ATT_BLOCK
    kvw = N_KV_HEADS * HEAD_DIM
    qw = N_KV_HEADS * Q_PER_KV * HEAD_DIM
    kcol, vcol = qw // kvw, qw // kvw + 1
    t = jnp.arange(L, dtype=jnp.int32)
    rowp = (t // GRID_W).astype(F32)
    colp = (t % GRID_W).astype(F32)
    nq = HEAD_DIM // 4
    inv = ROPE_BASE ** (-jnp.arange(nq, dtype=F32) / nq)
    ang = jnp.concatenate([rowp[:, None] * inv, colp[:, None] * inv], axis=-1)
    cos, sin = jnp.cos(ang), jnp.sin(ang)
    cc = jnp.concatenate([cos, cos], axis=-1)
    ss = jnp.concatenate([-sin, sin], axis=-1)
    prv = lambda i: jnp.maximum(i - 1, 0)
    nxt = lambda i: jnp.minimum(i + 1, nb - 1)
    cur = lambda i: i
    kspec = lambda f: pl.BlockSpec((ATT_BLOCK, kvw), lambda i, f=f: (f(i), kcol))
    vspec = lambda f: pl.BlockSpec((ATT_BLOCK, kvw), lambda i, f=f: (f(i), vcol))
    tspec = lambda f: pl.BlockSpec((ATT_BLOCK, HEAD_DIM), lambda i, f=f: (f(i), 0))
    return pl.pallas_call(
        functools.partial(_attn_kernel, scale=HEAD_DIM ** -0.5),
        grid=(nb,),
        in_specs=[pl.BlockSpec(memory_space=pltpu.SMEM),
                  pl.BlockSpec((ATT_BLOCK, qw), lambda i: (i, 0)),
                  kspec(prv), kspec(cur), kspec(nxt), vspec(prv), vspec(cur), vspec(nxt),
                  tspec(prv), tspec(cur), tspec(nxt), tspec(prv), tspec(cur), tspec(nxt),
                  pl.BlockSpec((n_ctx, kvw), lambda i: (0, 0)),
                  pl.BlockSpec((n_ctx, kvw), lambda i: (0, 1))],
        out_specs=pl.BlockSpec((ATT_BLOCK, qw), lambda i: (i, 0)),
        out_shape=jax.ShapeDtypeStruct((L, qw), BF16),
        compiler_params=_params("parallel"),
        name="window_attention",
    )(sink.astype(F32), qkv, qkv, qkv, qkv, qkv, qkv, qkv, cc, cc, cc, ss, ss, ss, kvc, kvc)


def _sg_kernel(u_ref, v_ref, g_ref, ws_ref, bs_ref, o_ref):
    v = v_ref[...]
    vn = (v * lax.rsqrt(jnp.mean(v * v, axis=-1, keepdims=True) + EPS) * g_ref[...]).astype(BF16)
    rows, width = v.shape
    gd = width // SG_GROUPS
    for ch in range(rows // CHUNK):
        rs = slice(ch * CHUNK, (ch + 1) * CHUNK)
        for g in range(SG_GROUPS):
            cs = slice(g * gd, (g + 1) * gd)
            bias = jnp.concatenate([bs_ref[g]] * (gd // LANES), axis=1)
            mixed = jnp.dot(ws_ref[g], vn[rs, cs], preferred_element_type=F32) + bias
            o_ref[rs, cs] = (u_ref[rs, cs] * mixed).astype(o_ref.dtype)


def spatial_gate(z, g, ws, bs, rows=256):
    L, w2 = z.shape
    width = w2 // 2
    bsb = jnp.broadcast_to(bs.astype(F32)[:, :, None], (SG_GROUPS, CHUNK, LANES))
    return pl.pallas_call(
        _sg_kernel,
        grid=(L // rows,),
        in_specs=[pl.BlockSpec((rows, width), lambda i: (i, 0)),
                  pl.BlockSpec((rows, width), lambda i: (i, 1)),
                  pl.BlockSpec((1, width), lambda i: (0, 0)),
                  pl.BlockSpec((SG_GROUPS, CHUNK, CHUNK), lambda i: (0, 0, 0)),
                  pl.BlockSpec((SG_GROUPS, CHUNK, LANES), lambda i: (0, 0, 0))],
        out_specs=pl.BlockSpec((rows, width), lambda i: (i, 0)),
        out_shape=jax.ShapeDtypeStruct((L, width), BF16),
        compiler_params=_params("parallel"),
        name="spatial_gate",
    )(z, z, g, ws.astype(BF16), bsb)


def _outproj_kernel(y_ref, w_ref, x_ref, gt_ref, o_ref):
    acc = jnp.dot(y_ref[...], w_ref[...], preferred_element_type=F32)
    o_ref[...] = x_ref[...] + gt_ref[...] * acc


def out_proj(y, w, x, gate, tm=512, tn=512):
    rows, k = y.shape
    n = w.shape[1]
    return pl.pallas_call(
        _outproj_kernel,
        grid=(rows // tm, n // tn),
        in_specs=[pl.BlockSpec((tm, k), lambda i, j: (i, 0)),
                  pl.BlockSpec((k, tn), lambda i, j: (0, j)),
                  pl.BlockSpec((tm, tn), lambda i, j: (i, j)),
                  pl.BlockSpec((1, tn), lambda i, j: (0, j))],
        out_specs=pl.BlockSpec((tm, tn), lambda i, j: (i, j)),
        out_shape=jax.ShapeDtypeStruct((rows, n), F32),
        compiler_params=_params("parallel", "arbitrary"),
        name="out_proj",
    )(y, w, x, gate)


def _outproj_even_kernel(yt_ref, ya_ref, w_ref, x_ref, gt_ref, o_ref, lhs_scr):
    @pl.when(pl.program_id(1) == 0)
    def _():
        ng = yt_ref.shape[0]
        c = ng * SUBLANES
        for r in range(yt_ref.shape[1] // SUBLANES):
            sub = yt_ref[:, r * SUBLANES:(r + 1) * SUBLANES, :].reshape(c, LANES)
            lhs_scr[r * LANES:(r + 1) * LANES, 0:c] = sub.T.astype(BF16)
        lhs_scr[:, c:] = ya_ref[...]

    acc = jnp.dot(lhs_scr[...], w_ref[...], preferred_element_type=F32)
    o_ref[...] = x_ref[...] + gt_ref[...] * acc


def out_proj_even(yt, ya, w, x, gate, tm=512, tn=512):
    rows, aw = ya.shape
    ng = yt.shape[0]
    k, n = w.shape
    return pl.pallas_call(
        _outproj_even_kernel,
        grid=(rows // tm, n // tn),
        in_specs=[pl.BlockSpec((ng, SUBLANES * tm // LANES, LANES), lambda i, j: (0, i, 0)),
                  pl.BlockSpec((tm, aw), lambda i, j: (i, 0)),
                  pl.BlockSpec((k, tn), lambda i, j: (0, j)),
                  pl.BlockSpec((tm, tn), lambda i, j: (i, j)),
                  pl.BlockSpec((1, tn), lambda i, j: (0, j))],
        out_specs=pl.BlockSpec((tm, tn), lambda i, j: (i, j)),
        out_shape=jax.ShapeDtypeStruct((rows, n), F32),
        scratch_shapes=[pltpu.VMEM((tm, k), BF16)],
        compiler_params=_params("parallel", "arbitrary"),
        name="out_proj_even",
    )(yt, ya, w, x, gate)


def kernel(x, c, ctx, c_ctx, ada_w, ada_b, norm_g, ffn_wg, ffn_wu, ffn_wd, ev_w_in, ev_conv_w, ev_conv_b,
           hy_w1, hy_b1, hy_f1, hy_w2, hy_b2, hy_f2, hy_w3, hy_skip, att_sink, ev_w_out, od_w_in, sg_g, sg_ws,
           sg_bs, od_w_out, final_g):
    assert x.shape[0] == 1 and ada_w.shape[0] == 2, "written for batch 1, depth 2 (even layer then odd layer)"
    _, L, d = x.shape
    hy_width = hy_skip.shape[1]
    hy_in = 3 * hy_width
    kv_w = N_KV_HEADS * HEAD_DIM
    q_end = hy_in + (d - hy_width)

    xs = x[0]
    xc = ctx[0]
    cond8 = jnp.zeros((SUBLANES, d), F32).at[0].set(c[0]).at[1].set(c_ctx)
    mods = ada_mods(cond8, ada_w, ada_b)
    row = lambda v: v.reshape(1, d)
    bf = lambda w: w.astype(BF16)

    mod = mods[0, 0].reshape(N_MOD, 1, d)
    mc = mods[0, 1].reshape(N_MOD, 1, d)
    g = norm_g[0]
    wg, wu, wd = bf(ffn_wg[0, 0]), bf(ffn_wu[0, 0]), bf(ffn_wd[0, 0])
    xs = half_ffn(xs, row(g[0]), mod[0], mod[1], mod[2], wg, wu, wd)
    xc = half_ffn(xc, row(g[0]), mc[0], mc[1], mc[2], wg, wu, wd)
    w_in = ev_w_in[0]
    zt = in_proj_transposed(xs, row(g[1]), mod[3], mod[4], bf(w_in[:, :hy_in].T))
    qkv = in_proj(xs, row(g[1]), mod[3], mod[4], bf(w_in[:, hy_in:]))
    kvc = in_proj(xc, row(g[1]), mc[3], mc[4], bf(w_in[:, q_end:]))
    ht = hyena_filter_t(L, hy_w1[0], hy_b1[0], hy_f1[0], hy_w2[0], hy_b2[0], hy_f2[0], hy_w3[0])
    yt = hyena_mix(zt, ht, ev_conv_w[0], ev_conv_b[0], hy_skip[0])
    ya = window_attention(qkv, kvc, att_sink[0])
    xs = out_proj_even(yt, ya, bf(ev_w_out[0]), xs, mod[5])
    xs = half_ffn(xs, row(g[2]), mod[6], mod[7], mod[8], bf(ffn_wg[0, 1]), bf(ffn_wu[0, 1]), bf(ffn_wd[0, 1]))

    mod = mods[1, 0].reshape(N_MOD, 1, d)
    g = norm_g[1]
    xs = half_ffn(xs, row(g[0]), mod[0], mod[1], mod[2], bf(ffn_wg[1, 0]), bf(ffn_wu[1, 0]), bf(ffn_wd[1, 0]))
    zz = in_proj(xs, row(g[1]), mod[3], mod[4], bf(od_w_in[0]), gelu=True)
    sgo = spatial_gate(zz, row(sg_g[0]), sg_ws[0], sg_bs[0])
    xs = out_proj(sgo, bf(od_w_out[0]), xs, mod[5])
    xs = half_ffn(xs, row(g[2]), mod[6], mod[7], mod[8], bf(ffn_wg[1, 1]), bf(ffn_wu[1, 1]), bf(ffn_wd[1, 1]),
                  final_g=row(final_g))
    return xs[None]
```

```python
import functools
import math

import numpy as np
import jax
import jax.numpy as jnp
from jax import lax
from jax.experimental import pallas as pl
from jax.experimental.pallas import tpu as pltpu

F32 = jnp.float32
BF16 = jnp.bfloat16

LANES = 128
SUBLANES = 8
VMEM_LIMIT_BYTES = 56 * 1024 * 1024

EPS = 1e-6
NEG_INF = -1e30
N_MOD = 9
HEAD_DIM = 128
N_KV_HEADS = 2
Q_PER_KV = 4
ATT_BLOCK = 128
GRID_W = 64
ROPE_BASE = 10000.0
HY_BANDS = 16
HY_DECAY_TARGET = 1e-2
HY_FAST_PCT = 0.3
HY_SLOW_PCT = 1.5
SG_GROUPS = 8
CHUNK = 128
HY_GROUP = SUBLANES


def _params(*sem):
    return pltpu.CompilerParams(dimension_semantics=sem, vmem_limit_bytes=VMEM_LIMIT_BYTES)


NORM_ROWS = 16


def _norm_mod_store(x_ref, g_ref, sh_ref, sc_ref, h_scr):
    g = g_ref[...]
    one_plus_scale = 1.0 + sc_ref[...]
    shift = sh_ref[...]

    def body(i, carry):
        rows = pl.ds(pl.multiple_of(i * NORM_ROWS, NORM_ROWS), NORM_ROWS)
        x = x_ref[rows, :]
        y = x * lax.rsqrt(jnp.mean(x * x, axis=-1, keepdims=True) + EPS) * g
        h_scr[rows, :] = (y * one_plus_scale + shift).astype(BF16)
        return carry

    lax.fori_loop(0, x_ref.shape[0] // NORM_ROWS, body, 0)


def _ada_kernel(s_ref, w_ref, b_ref, o_ref):
    s = s_ref[...]
    s = (s * jax.nn.sigmoid(s)).astype(BF16)
    o_ref[0] = jnp.dot(s, w_ref[0].astype(BF16), preferred_element_type=F32) + b_ref[0]


def ada_mods(cond8, ada_w, ada_b, tn=1024):
    depth, d, n = ada_w.shape
    return pl.pallas_call(
        _ada_kernel,
        grid=(depth, n // tn),
        in_specs=[
            pl.BlockSpec((SUBLANES, d), lambda l, j: (0, 0)),
            pl.BlockSpec((1, d, tn), lambda l, j: (l, 0, j)),
            pl.BlockSpec((1, 1, tn), lambda l, j: (l, 0, j)),
        ],
        out_specs=pl.BlockSpec((1, SUBLANES, tn), lambda l, j: (l, 0, j)),
        out_shape=jax.ShapeDtypeStruct((depth, SUBLANES, n), F32),
        compiler_params=_params("parallel", "parallel"),
        name="ada_mods",
    )(cond8, ada_w, ada_b.reshape(depth, 1, n))


def _ffn_kernel(x_ref, g_ref, sh_ref, sc_ref, gt_ref, wg_ref, wu_ref, wd_ref, *rest, final):
    if final:
        fg_ref, o_ref, h_scr = rest
    else:
        o_ref, h_scr = rest
    f = pl.program_id(1)

    @pl.when(f == 0)
    def _():
        _norm_mod_store(x_ref, g_ref, sh_ref, sc_ref, h_scr)
        o_ref[...] = jnp.zeros_like(o_ref)

    h = h_scr[...]
    gate_act = jnp.dot(h, wg_ref[...], preferred_element_type=F32)
    up = jnp.dot(h, wu_ref[...], preferred_element_type=F32)
    a = (gate_act * jax.nn.sigmoid(gate_act) * up).astype(BF16)
    o_ref[...] += jnp.dot(a, wd_ref[...], preferred_element_type=F32)

    @pl.when(f == pl.num_programs(1) - 1)
    def _():
        out = x_ref[...] + (0.5 * gt_ref[...]) * o_ref[...]
        if final:
            out = out * lax.rsqrt(jnp.mean(out * out, axis=-1, keepdims=True) + EPS) * fg_ref[...]
        o_ref[...] = out


def half_ffn(x, g, shift, scale, gate, wg, wu, wd, final_g=None, tm=512, tf=512):
    rows, d = x.shape
    dff = wg.shape[1]
    tm = min(tm, rows)
    vec = pl.BlockSpec((1, d), lambda i, f: (0, 0))
    in_specs = [
        pl.BlockSpec((tm, d), lambda i, f: (i, 0)),
        vec, vec, vec, vec,
        pl.BlockSpec((d, tf), lambda i, f: (0, f)),
        pl.BlockSpec((d, tf), lambda i, f: (0, f)),
        pl.BlockSpec((tf, d), lambda i, f: (f, 0)),
    ]
    args = [x, g, shift, scale, gate, wg, wu, wd]
    if final_g is not None:
        in_specs.append(vec)
        args.append(final_g)
    return pl.pallas_call(
        functools.partial(_ffn_kernel, final=final_g is not None),
        grid=(rows // tm, dff // tf),
        in_specs=in_specs,
        out_specs=pl.BlockSpec((tm, d), lambda i, f: (i, 0)),
        out_shape=jax.ShapeDtypeStruct((rows, d), F32),
        scratch_shapes=[pltpu.VMEM((tm, d), BF16)],
        compiler_params=_params("parallel", "arbitrary"),
        name="half_ffn",
    )(*args)


def _inproj_kernel(x_ref, g_ref, sh_ref, sc_ref, w_ref, o_ref, h_scr):
    @pl.when(pl.program_id(1) == 0)
    def _():
        _norm_mod_store(x_ref, g_ref, sh_ref, sc_ref, h_scr)

    o_ref[...] = jnp.dot(h_scr[...], w_ref[...], preferred_element_type=F32)


def in_proj(x, g, shift, scale, w, tm=1024, tn=512):
    rows, d = x.shape
    n = w.shape[1]
    tm = min(tm, rows)
    vec = pl.BlockSpec((1, d), lambda i, j: (0, 0))
    return pl.pallas_call(
        _inproj_kernel,
        grid=(rows // tm, n // tn),
        in_specs=[pl.BlockSpec((tm, d), lambda i, j: (i, 0)), vec, vec, vec,
                  pl.BlockSpec((d, tn), lambda i, j: (0, j))],
        out_specs=pl.BlockSpec((tm, tn), lambda i, j: (i, j)),
        out_shape=jax.ShapeDtypeStruct((rows, n), F32),
        scratch_shapes=[pltpu.VMEM((tm, d), BF16)],
        compiler_params=_params("parallel", "arbitrary"),
        name="in_proj",
    )(x, g, shift, scale, w)


HY_CHANNEL_TILE = 512


def _even_inproj_kernel(x_ref, g_ref, sh_ref, sc_ref, wt_ref, w_ref, zt_ref, qkv_ref, h_scr):
    _norm_mod_store(x_ref, g_ref, sh_ref, sc_ref, h_scr)
    tm = x_ref.shape[0]
    ct = HY_CHANNEL_TILE
    contract_last = (((1,), (1,)), ((), ()))
    for j in range(wt_ref.shape[0] // ct):
        zt = lax.dot_general(wt_ref[j * ct:(j + 1) * ct, :], h_scr[...], contract_last,
                             preferred_element_type=F32)
        for r in range(tm // LANES):
            zt_ref[j * ct // SUBLANES:(j + 1) * ct // SUBLANES, r * SUBLANES:(r + 1) * SUBLANES, :] = (
                zt[:, r * LANES:(r + 1) * LANES].reshape(ct // SUBLANES, SUBLANES, LANES))
    qkv_ref[...] = jnp.dot(h_scr[...], w_ref[...], preferred_element_type=F32)


def even_in_proj(x, g, shift, scale, wt, w, tm=512):
    rows, d = x.shape
    nc = wt.shape[0]
    n = w.shape[1]
    vec = pl.BlockSpec((1, d), lambda i: (0, 0))
    once = pl.Buffered(1)
    return pl.pallas_call(
        _even_inproj_kernel,
        grid=(rows // tm,),
        in_specs=[pl.BlockSpec((tm, d), lambda i: (i, 0)), vec, vec, vec,
                  pl.BlockSpec((nc, d), lambda i: (0, 0), pipeline_mode=once),
                  pl.BlockSpec((d, n), lambda i: (0, 0), pipeline_mode=once)],
        out_specs=[pl.BlockSpec((nc // SUBLANES, SUBLANES * tm // LANES, LANES), lambda i: (0, i, 0)),
                   pl.BlockSpec((tm, n), lambda i: (i, 0))],
        out_shape=[jax.ShapeDtypeStruct((nc // SUBLANES, SUBLANES * rows // LANES, LANES), F32),
                   jax.ShapeDtypeStruct((rows, n), F32)],
        scratch_shapes=[pltpu.VMEM((tm, d), BF16)],
        compiler_params=_params("parallel"),
        name="even_in_proj",
    )(x, g, shift, scale, wt, w)


def _filter_kernel(z_ref, t_ref, w1_ref, b1_ref, f1_ref, w2_ref, b2_ref, f2_ref, w3_ref, dl_ref, o_ref):
    hi = lax.Precision.HIGHEST
    ct = w3_ref.shape[0]
    for r in range(z_ref.shape[1] // LANES):
        sl = slice(r * LANES, (r + 1) * LANES)
        a1 = jnp.dot(w1_ref[...], z_ref[:, sl], precision=hi, preferred_element_type=F32)
        h1 = jnp.sin(f1_ref[...] * (a1 + b1_ref[...]))
        a2 = jnp.dot(w2_ref[...], h1, precision=hi, preferred_element_type=F32)
        h2 = jnp.sin(f2_ref[...] * (a2 + b2_ref[...]))
        h3 = jnp.dot(w3_ref[...], h2.astype(BF16), preferred_element_type=F32)
        dec = jnp.exp(-(t_ref[0:1, sl] * dl_ref[...]))
        o_ref[:, r * SUBLANES:(r + 1) * SUBLANES, :] = (h3 * dec).reshape(ct // SUBLANES, SUBLANES, LANES)


def hyena_filter_t(L, w1, b1, f1, w2, b2, f2, w3, tl=1024):
    hid = w1.shape[1]
    c2 = w3.shape[1]
    width = c2 // 2
    t = jnp.linspace(0.0, 1.0, L, dtype=F32)[:, None]
    w = (2.0 * math.pi / L) * jnp.arange(L, dtype=F32)[:, None]
    bands = jnp.linspace(1e-4, HY_BANDS - 1, HY_BANDS, dtype=F32)[None, :]
    z = jnp.concatenate([t, jnp.cos(bands * w), -jnp.sin(bands * w)], axis=-1)
    emb = z.shape[1]
    embp = -(-emb // SUBLANES) * SUBLANES
    zt = jnp.pad(z.T, ((0, embp - emb), (0, 0)))
    w1t = jnp.pad(w1.astype(F32).T, ((0, 0), (0, embp - emb)))
    lt = math.log(HY_DECAY_TARGET)
    deltas = jnp.abs(jnp.linspace(lt / HY_SLOW_PCT, lt / HY_FAST_PCT, width, dtype=F32))
    deltas2 = jnp.concatenate([deltas, deltas])
    col = lambda v: jnp.broadcast_to(v.astype(F32)[:, None], (v.shape[0], LANES))
    trow = jnp.broadcast_to(t.T, (SUBLANES, L))
    tl = min(tl, L)
    full = lambda shape: pl.BlockSpec(shape, lambda i: (0,) * len(shape))
    return pl.pallas_call(
        _filter_kernel,
        grid=(L // tl,),
        in_specs=[
            pl.BlockSpec((embp, tl), lambda i: (0, i)),
            pl.BlockSpec((SUBLANES, tl), lambda i: (0, i)),
            full((hid, embp)), full((hid, LANES)), full((hid, LANES)),
            full((hid, hid)), full((hid, LANES)), full((hid, LANES)),
            full((c2, hid)), full((c2, LANES)),
        ],
        out_specs=pl.BlockSpec((c2 // SUBLANES, SUBLANES * tl // LANES, LANES), lambda i: (0, i, 0)),
        out_shape=jax.ShapeDtypeStruct((c2 // SUBLANES, SUBLANES * L // LANES, LANES), F32),
        compiler_params=_params("parallel"),
        name="hyena_filter",
    )(zt, trow, w1t, col(b1), col(f1), w2.astype(F32).T, col(b2), col(f2), w3.T.astype(BF16), col(deltas2))


def _dft_constants(h1):
    n_outer = 2 * h1
    n = n_outer * LANES
    kp = -(-(h1 + 1) // SUBLANES) * SUBLANES
    k1 = np.arange(kp)[:, None]
    n1 = np.arange(h1)[None, :]
    ang_a = 2.0 * np.pi * ((k1 * n1) % n_outer) / n_outer
    fa = np.concatenate([np.cos(ang_a), -np.sin(ang_a)], axis=0)
    n2 = np.arange(LANES)[None, :]
    ang_t = 2.0 * np.pi * ((k1 * n2) % n) / n
    twr, twi = np.cos(ang_t), -np.sin(ang_t)
    a = np.arange(LANES)
    ang_b = 2.0 * np.pi * ((a[:, None] * a[None, :]) % LANES) / LANES
    cb, sb = np.cos(ang_b), np.sin(ang_b)
    fb = np.block([[cb, -sb], [sb, cb]])
    gb = np.block([[cb, sb], [-sb, cb]])
    wk = np.where((k1 == 0) | (k1 == h1), 1.0, 2.0) * (k1 <= h1)
    ga = np.concatenate([(wk * np.cos(ang_a)).T, (-wk * np.sin(ang_a)).T], axis=1)
    as_bf = lambda m: jnp.asarray(m, dtype=F32).astype(BF16)
    return (as_bf(fa), jnp.asarray(twr, F32), jnp.asarray(twi, F32), as_bf(fb), as_bf(gb), as_bf(ga), kp, n)


def _hyena_kernel(scal_ref, z0_ref, z1_ref, zv_ref, hf_ref, hb_ref, fa_ref, twr_ref, twi_ref, fb_ref, gb_ref,
                  ga_ref, o_ref, *, h1, kp, inv_n):
    grp = pl.program_id(0)
    row = lax.broadcasted_iota(jnp.int32, (h1, LANES), 0)
    lane = lax.broadcasted_iota(jnp.int32, (h1, LANES), 1)
    first = (row == 0) & (lane == 0)
    last = (row == h1 - 1) & (lane == LANES - 1)

    def chan(ref, ci):
        return ref[0, pl.ds(ci, h1, stride=SUBLANES), :]

    def prev(x):
        r = pltpu.roll(x, 1, 1)
        r = jnp.where(lane == 0, pltpu.roll(r, 1, 0), r)
        return jnp.where(first, 0.0, r)

    def nxt(x):
        r = pltpu.roll(x, LANES - 1, 1)
        r = jnp.where(lane == LANES - 1, pltpu.roll(r, h1 - 1, 0), r)
        return jnp.where(last, 0.0, r)

    us, x0s, hfs, hbs, nrm = [], [], [], [], []
    for ci in range(HY_GROUP):
        c = grp * HY_GROUP + ci

        def sconv(ref, s):
            x = chan(ref, ci)
            return (scal_ref[3 * s, c] * prev(x) + scal_ref[3 * s + 1, c] * x
                    + scal_ref[3 * s + 2, c] * nxt(x) + scal_ref[9 + s, c])

        x0 = sconv(z0_ref, 0)
        x1 = sconv(z1_ref, 1)
        hv = sconv(zv_ref, 2)
        us.append(hv * x1)
        x0s.append(x0)
        hf = chan(hf_ref, ci)
        hb = jnp.where(first, 0.0, chan(hb_ref, ci))
        hfs.append(hf)
        hbs.append(hb)
        ssq = jnp.sum(hf * hf, keepdims=True) + jnp.sum(hb * hb, keepdims=True)
        nrm.append(lax.rsqrt(ssq + EPS))

    fa = fa_ref[...]
    twr, twi = twr_ref[...], twi_ref[...]

    def outer_fwd(mats):
        xc = jnp.concatenate([m.astype(BF16) for m in mats], axis=1)
        a = jnp.dot(fa, xc, preferred_element_type=F32)
        out = []
        for ci in range(HY_GROUP):
            ar = a[:kp, ci * LANES:(ci + 1) * LANES]
            ai = a[kp:, ci * LANES:(ci + 1) * LANES]
            out.append(jnp.concatenate([ar * twr - ai * twi, ar * twi + ai * twr], axis=1))
        return out

    stacked = jnp.concatenate(outer_fwd(us) + outer_fwd(hfs) + outer_fwd(hbs), axis=0).astype(BF16)
    spec = jnp.dot(stacked, fb_ref[...], preferred_element_type=F32)

    prod = []
    for ci in range(HY_GROUP):
        xu = spec[ci * kp:(ci + 1) * kp]
        xf = spec[(HY_GROUP + ci) * kp:(HY_GROUP + ci + 1) * kp]
        xb = spec[(2 * HY_GROUP + ci) * kp:(2 * HY_GROUP + ci + 1) * kp]
        xr, xi = xu[:, :LANES], xu[:, LANES:]
        kr = xf[:, :LANES] + xb[:, :LANES]
        ki = xf[:, LANES:] - xb[:, LANES:]
        prod.append(jnp.concatenate([xr * kr - xi * ki, xr * ki + xi * kr], axis=1))
    inner = jnp.dot(jnp.concatenate(prod, axis=0).astype(BF16), gb_ref[...], preferred_element_type=F32)

    cols = []
    for ci in range(HY_GROUP):
        b = inner[ci * kp:(ci + 1) * kp]
        br, bi = b[:, :LANES], b[:, LANES:]
        cols.append(jnp.concatenate([br * twr + bi * twi, bi * twr - br * twi], axis=0))
    y = jnp.dot(ga_ref[...], jnp.concatenate(cols, axis=1).astype(BF16), preferred_element_type=F32)

    for ci in range(HY_GROUP):
        c = grp * HY_GROUP + ci
        yc = y[:, ci * LANES:(ci + 1) * LANES] * (nrm[ci] * inv_n)
        o_ref[0, pl.ds(ci, h1, stride=SUBLANES), :] = x0s[ci] * (yc + us[ci] * scal_ref[12, c])


def hyena_mix(zt, ht, conv_w, conv_b, skip):
    ng3, r8, _ = zt.shape
    ng = ng3 // 3
    h1 = r8 // SUBLANES
    width = ng * SUBLANES
    fa, twr, twi, fb, gb, ga, kp, n = _dft_constants(h1)
    scal = jnp.concatenate([
        conv_w[:, 0:width], conv_w[:, width:2 * width], conv_w[:, 2 * width:3 * width],
        conv_b.reshape(3, width), skip.reshape(1, width)], axis=0).astype(F32)
    blk = lambda off: pl.BlockSpec((1, r8, LANES), lambda g, off=off: (g + off, 0, 0))
    full = lambda a: pl.BlockSpec(a.shape, lambda g: (0,) * a.ndim)
    return pl.pallas_call(
        functools.partial(_hyena_kernel, h1=h1, kp=kp, inv_n=1.0 / n),
        grid=(ng,),
        in_specs=[pl.BlockSpec(memory_space=pltpu.SMEM),
                  blk(0), blk(ng), blk(2 * ng), blk(0), blk(ng),
                  full(fa), full(twr), full(twi), full(fb), full(gb), full(ga)],
        out_specs=pl.BlockSpec((1, r8, LANES), lambda g: (g, 0, 0)),
        out_shape=jax.ShapeDtypeStruct((ng, r8, LANES), F32),
        compiler_params=_params("parallel"),
        name="hyena_mix",
    )(scal, zt, zt, zt, ht, ht, fa, twr, twi, fb, gb, ga)


def _attn_kernel(sink_ref, q_ref, kp_ref, k0_ref, kn_ref, vp_ref, v0_ref, vn_ref, ccp_ref, cc0_ref, ccn_ref,
                 ssp_ref, ss0_ref, ssn_ref, kc_ref, vc_ref, o_ref, *, scale):
    i = pl.program_id(0)
    nblk = pl.num_programs(0)
    hd, blk = HEAD_DIM, ATT_BLOCK
    nt = (((1,), (1,)), ((), ()))

    def rope(x, cc, ss):
        return x * cc + pltpu.roll(x, hd // 2, 1) * ss

    rows = Q_PER_KV * blk
    off = lax.broadcasted_iota(jnp.int32, (rows, 3 * blk), 0) & (blk - 1)
    cidx = lax.broadcasted_iota(jnp.int32, (rows, 3 * blk), 1)
    rel = cidx - blk - off
    kpos = (i - 1) * blk + cidx
    mask = (jnp.abs(rel) <= blk) & (kpos >= 0) & (kpos < nblk * blk)
    head_of_row = lax.broadcasted_iota(jnp.int32, (rows, 1), 0) // blk

    for g in range(N_KV_HEADS):
        gs = slice(g * hd, (g + 1) * hd)
        kb = jnp.concatenate([
            rope(kp_ref[:, gs], ccp_ref[...], ssp_ref[...]),
            rope(k0_ref[:, gs], cc0_ref[...], ss0_ref[...]),
            rope(kn_ref[:, gs], ccn_ref[...], ssn_ref[...])], axis=0).astype(BF16)
        vb = jnp.concatenate([vp_ref[:, gs], v0_ref[:, gs], vn_ref[:, gs]], axis=0).astype(BF16)
        kcg = kc_ref[:, gs].astype(BF16)
        vcg = vc_ref[:, gs].astype(BF16)
        q4 = jnp.concatenate([
            rope(q_ref[:, (g * Q_PER_KV + h) * hd:(g * Q_PER_KV + h + 1) * hd], cc0_ref[...], ss0_ref[...])
            for h in range(Q_PER_KV)], axis=0).astype(BF16)
        s_loc = lax.dot_general(q4, kb, nt, preferred_element_type=F32) * scale
        s_ctx = lax.dot_general(q4, kcg, nt, preferred_element_type=F32) * scale
        s_loc = jnp.where(mask, s_loc, NEG_INF)
        sink = jnp.zeros((rows, 1), F32)
        for h in range(Q_PER_KV):
            sink = jnp.where(head_of_row == h, sink_ref[g * Q_PER_KV + h], sink)
        m = jnp.maximum(jnp.maximum(jnp.max(s_loc, axis=-1, keepdims=True),
                                    jnp.max(s_ctx, axis=-1, keepdims=True)), sink)
        p_loc = jnp.exp(s_loc - m)
        p_ctx = jnp.exp(s_ctx - m)
        den = jnp.exp(sink - m) + jnp.sum(p_loc, axis=-1, keepdims=True) + jnp.sum(p_ctx, axis=-1, keepdims=True)
        inv = 1.0 / den
        o = (jnp.dot((p_ctx * inv).astype(BF16), vcg, preferred_element_type=F32)
             + jnp.dot((p_loc * inv).astype(BF16), vb, preferred_element_type=F32))
        for h in range(Q_PER_KV):
            o_ref[:, (g * Q_PER_KV + h) * hd:(g * Q_PER_KV + h + 1) * hd] = o[h * blk:(h + 1) * blk].astype(o_ref.dtype)


def window_attention(qkv, kvc, sink):
    L = qkv.shape[0]
    n_ctx = kvc.shape[0]
    nb = L // ATT_BLOCK
    kvw = N_KV_HEADS * HEAD_DIM
    qw = N_KV_HEADS * Q_PER_KV * HEAD_DIM
    kcol, vcol = qw // kvw, qw // kvw + 1
    t = jnp.arange(L, dtype=jnp.int32)
    rowp = (t // GRID_W).astype(F32)
    colp = (t % GRID_W).astype(F32)
    nq = HEAD_DIM // 4
    inv = ROPE_BASE ** (-jnp.arange(nq, dtype=F32) / nq)
    ang = jnp.concatenate([rowp[:, None] * inv, colp[:, None] * inv], axis=-1)
    cos, sin = jnp.cos(ang), jnp.sin(ang)
    cc = jnp.concatenate([cos, cos], axis=-1)
    ss = jnp.concatenate([-sin, sin], axis=-1)
    prv = lambda i: jnp.maximum(i - 1, 0)
    nxt = lambda i: jnp.minimum(i + 1, nb - 1)
    cur = lambda i: i
    kspec = lambda f: pl.BlockSpec((ATT_BLOCK, kvw), lambda i, f=f: (f(i), kcol))
    vspec = lambda f: pl.BlockSpec((ATT_BLOCK, kvw), lambda i, f=f: (f(i), vcol))
    tspec = lambda f: pl.BlockSpec((ATT_BLOCK, HEAD_DIM), lambda i, f=f: (f(i), 0))
    return pl.pallas_call(
        functools.partial(_attn_kernel, scale=HEAD_DIM ** -0.5),
        grid=(nb,),
        in_specs=[pl.BlockSpec(memory_space=pltpu.SMEM),
                  pl.BlockSpec((ATT_BLOCK, qw), lambda i: (i, 0)),
                  kspec(prv), kspec(cur), kspec(nxt), vspec(prv), vspec(cur), vspec(nxt),
                  tspec(prv), tspec(cur), tspec(nxt), tspec(prv), tspec(cur), tspec(nxt),
                  pl.BlockSpec((n_ctx, kvw), lambda i: (0, 0)),
                  pl.BlockSpec((n_ctx, kvw), lambda i: (0, 1))],
        out_specs=pl.BlockSpec((ATT_BLOCK, qw), lambda i: (i, 0)),
        out_shape=jax.ShapeDtypeStruct((L, qw), BF16),
        compiler_params=_params("parallel"),
        name="window_attention",
    )(sink.astype(F32), qkv, qkv, qkv, qkv, qkv, qkv, qkv, cc, cc, cc, ss, ss, ss, kvc, kvc)


def _odd_mixer_kernel(xf_ref, g_ref, sh_ref, sc_ref, win_ref, sgg_ref, ws_ref, bs_ref, wout_ref, x_ref, gt_ref,
                      o_ref, h_scr, vn_scr, sg_scr):
    @pl.when(pl.program_id(1) == 0)
    def _():
        tm, width = sg_scr.shape
        gd = width // SG_GROUPS
        _norm_mod_store(xf_ref, g_ref, sh_ref, sc_ref, h_scr)
        v = jax.nn.gelu(jnp.dot(h_scr[...], win_ref[:, width:], preferred_element_type=F32), approximate=True)
        vn_scr[...] = (v * lax.rsqrt(jnp.mean(v * v, axis=-1, keepdims=True) + EPS) * sgg_ref[...]).astype(BF16)
        for g in range(SG_GROUPS):
            cs = slice(g * gd, (g + 1) * gd)
            u = jax.nn.gelu(jnp.dot(h_scr[...], win_ref[:, cs], preferred_element_type=F32), approximate=True)
            bias = jnp.concatenate([bs_ref[g]] * (gd // LANES), axis=1)
            for ch in range(tm // CHUNK):
                rs = slice(ch * CHUNK, (ch + 1) * CHUNK)
                mixed = jnp.dot(ws_ref[g], vn_scr[rs, cs], preferred_element_type=F32) + bias
                sg_scr[rs, cs] = (u[rs] * mixed).astype(BF16)

    acc = jnp.dot(sg_scr[...], wout_ref[...], preferred_element_type=F32)
    o_ref[...] = x_ref[...] + gt_ref[...] * acc


def odd_mixer(x, g, shift, scale, gate, w_in, sg_g, ws, bs, w_out, tm=512, tn=512):
    rows, d = x.shape
    width = w_in.shape[1] // 2
    n = w_out.shape[1]
    bsb = jnp.broadcast_to(bs.astype(F32)[:, :, None], (SG_GROUPS, CHUNK, LANES))
    vec = lambda w: pl.BlockSpec((1, w), lambda i, j: (0, 0))
    once = pl.Buffered(1)
    return pl.pallas_call(
        _odd_mixer_kernel,
        grid=(rows // tm, n // tn),
        in_specs=[pl.BlockSpec((tm, d), lambda i, j: (i, 0)), vec(d), vec(d), vec(d),
                  pl.BlockSpec((d, 2 * width), lambda i, j: (0, 0), pipeline_mode=once),
                  vec(width),
                  pl.BlockSpec((SG_GROUPS, CHUNK, CHUNK), lambda i, j: (0, 0, 0)),
                  pl.BlockSpec((SG_GROUPS, CHUNK, LANES), lambda i, j: (0, 0, 0)),
                  pl.BlockSpec((width, tn), lambda i, j: (0, j)),
                  pl.BlockSpec((tm, tn), lambda i, j: (i, j)),
                  pl.BlockSpec((1, tn), lambda i, j: (0, j))],
        out_specs=pl.BlockSpec((tm, tn), lambda i, j: (i, j)),
        out_shape=jax.ShapeDtypeStruct((rows, n), F32),
        scratch_shapes=[pltpu.VMEM((tm, d), BF16), pltpu.VMEM((tm, width), BF16), pltpu.VMEM((tm, width), BF16)],
        compiler_params=_params("parallel", "arbitrary"),
        name="odd_mixer",
    )(x, g, shift, scale, w_in, sg_g, ws.astype(BF16), bsb, w_out, x, gate)


def _outproj_even_kernel(yt_ref, ya_ref, w_ref, x_ref, gt_ref, o_ref, lhs_scr):
    ng = yt_ref.shape[0]
    c = ng * SUBLANES
    for r in range(yt_ref.shape[1] // SUBLANES):
        sub = yt_ref[:, r * SUBLANES:(r + 1) * SUBLANES, :].reshape(c, LANES)
        lhs_scr[r * LANES:(r + 1) * LANES, 0:c] = sub.T.astype(BF16)
    lhs_scr[:, c:] = ya_ref[...]
    acc = jnp.dot(lhs_scr[...], w_ref[...], preferred_element_type=F32)
    o_ref[...] = x_ref[...] + gt_ref[...] * acc


def out_proj_even(yt, ya, w, x, gate, tm=512):
    rows, aw = ya.shape
    ng = yt.shape[0]
    k, n = w.shape
    return pl.pallas_call(
        _outproj_even_kernel,
        grid=(rows // tm,),
        in_specs=[pl.BlockSpec((ng, SUBLANES * tm // LANES, LANES), lambda i: (0, i, 0)),
                  pl.BlockSpec((tm, aw), lambda i: (i, 0)),
                  pl.BlockSpec((k, n), lambda i: (0, 0), pipeline_mode=pl.Buffered(1)),
                  pl.BlockSpec((tm, n), lambda i: (i, 0)),
                  pl.BlockSpec((1, n), lambda i: (0, 0))],
        out_specs=pl.BlockSpec((tm, n), lambda i: (i, 0)),
        out_shape=jax.ShapeDtypeStruct((rows, n), F32),
        scratch_shapes=[pltpu.VMEM((tm, k), BF16)],
        compiler_params=_params("parallel"),
        name="out_proj_even",
    )(yt, ya, w, x, gate)


def kernel(x, c, ctx, c_ctx, ada_w, ada_b, norm_g, ffn_wg, ffn_wu, ffn_wd, ev_w_in, ev_conv_w, ev_conv_b,
           hy_w1, hy_b1, hy_f1, hy_w2, hy_b2, hy_f2, hy_w3, hy_skip, att_sink, ev_w_out, od_w_in, sg_g, sg_ws,
           sg_bs, od_w_out, final_g):
    assert x.shape[0] == 1 and ada_w.shape[0] == 2, "written for batch 1, depth 2 (even layer then odd layer)"
    _, L, d = x.shape
    hy_width = hy_skip.shape[1]
    hy_in = 3 * hy_width
    kv_w = N_KV_HEADS * HEAD_DIM
    q_end = hy_in + (d - hy_width)

    xs = x[0]
    xc = ctx[0]
    cond8 = jnp.zeros((SUBLANES, d), F32).at[0].set(c[0]).at[1].set(c_ctx)
    mods = ada_mods(cond8, ada_w, ada_b)
    row = lambda v: v.reshape(1, d)
    bf = lambda w: w.astype(BF16)

    mod = mods[0, 0].reshape(N_MOD, 1, d)
    mc = mods[0, 1].reshape(N_MOD, 1, d)
    g = norm_g[0]
    wg, wu, wd = bf(ffn_wg[0, 0]), bf(ffn_wu[0, 0]), bf(ffn_wd[0, 0])
    xs = half_ffn(xs, row(g[0]), mod[0], mod[1], mod[2], wg, wu, wd)
    xc = half_ffn(xc, row(g[0]), mc[0], mc[1], mc[2], wg, wu, wd)
    w_in = ev_w_in[0]
    zt, qkv = even_in_proj(xs, row(g[1]), mod[3], mod[4], bf(w_in[:, :hy_in].T), bf(w_in[:, hy_in:]))
    kvc = in_proj(xc, row(g[1]), mc[3], mc[4], bf(w_in[:, q_end:]))
    ht = hyena_filter_t(L, hy_w1[0], hy_b1[0], hy_f1[0], hy_w2[0], hy_b2[0], hy_f2[0], hy_w3[0])
    yt = hyena_mix(zt, ht, ev_conv_w[0], ev_conv_b[0], hy_skip[0])
    ya = window_attention(qkv, kvc, att_sink[0])
    xs = out_proj_even(yt, ya, bf(ev_w_out[0]), xs, mod[5])
    xs = half_ffn(xs, row(g[2]), mod[6], mod[7], mod[8], bf(ffn_wg[0, 1]), bf(ffn_wu[0, 1]), bf(ffn_wd[0, 1]))

    mod = mods[1, 0].reshape(N_MOD, 1, d)
    g = norm_g[1]
    xs = half_ffn(xs, row(g[0]), mod[0], mod[1], mod[2], bf(ffn_wg[1, 0]), bf(ffn_wu[1, 0]), bf(ffn_wd[1, 0]))
    xs = odd_mixer(xs, row(g[1]), mod[3], mod[4], mod[5], bf(od_w_in[0]), row(sg_g[0]), sg_ws[0], sg_bs[0],
                   bf(od_w_out[0]))
    xs = half_ffn(xs, row(g[2]), mod[6], mod[7], mod[8], bf(ffn_wg[1, 1]), bf(ffn_wu[1, 1]), bf(ffn_wd[1, 1]),
                  final_g=row(final_g))
    return xs[None]
```

```python
import functools
import math

import numpy as np
import jax
import jax.numpy as jnp
from jax import lax
from jax.experimental import pallas as pl
from jax.experimental.pallas import tpu as pltpu

F32 = jnp.float32
BF16 = jnp.bfloat16

LANES = 128
SUBLANES = 8
VMEM_LIMIT_BYTES = 56 * 1024 * 1024

EPS = 1e-6
NEG_INF = -1e30
N_MOD = 9
HEAD_DIM = 128
N_KV_HEADS = 2
Q_PER_KV = 4
ATT_BLOCK = 128
GRID_W = 64
ROPE_BASE = 10000.0
HY_BANDS = 16
HY_DECAY_TARGET = 1e-2
HY_FAST_PCT = 0.3
HY_SLOW_PCT = 1.5
SG_GROUPS = 8
CHUNK = 128
HY_GROUP = SUBLANES


def _params(*sem):
    return pltpu.CompilerParams(dimension_semantics=sem, vmem_limit_bytes=VMEM_LIMIT_BYTES)


NORM_ROWS = 16


def _norm_mod_store(x_ref, g_ref, sh_ref, sc_ref, h_scr):
    g = g_ref[...]
    one_plus_scale = 1.0 + sc_ref[...]
    shift = sh_ref[...]

    def body(i, carry):
        rows = pl.ds(pl.multiple_of(i * NORM_ROWS, NORM_ROWS), NORM_ROWS)
        x = x_ref[rows, :]
        y = x * lax.rsqrt(jnp.mean(x * x, axis=-1, keepdims=True) + EPS) * g
        h_scr[rows, :] = (y * one_plus_scale + shift).astype(BF16)
        return carry

    lax.fori_loop(0, x_ref.shape[0] // NORM_ROWS, body, 0, unroll=8)


def _ada_kernel(s_ref, w_ref, b_ref, o_ref):
    s = s_ref[...]
    s = (s * jax.nn.sigmoid(s)).astype(BF16)
    o_ref[0] = jnp.dot(s, w_ref[0].astype(BF16), preferred_element_type=F32) + b_ref[0]


def ada_mods(cond8, ada_w, ada_b, tn=1024):
    depth, d, n = ada_w.shape
    return pl.pallas_call(
        _ada_kernel,
        grid=(depth, n // tn),
        in_specs=[
            pl.BlockSpec((SUBLANES, d), lambda l, j: (0, 0)),
            pl.BlockSpec((1, d, tn), lambda l, j: (l, 0, j)),
            pl.BlockSpec((1, 1, tn), lambda l, j: (l, 0, j)),
        ],
        out_specs=pl.BlockSpec((1, SUBLANES, tn), lambda l, j: (l, 0, j)),
        out_shape=jax.ShapeDtypeStruct((depth, SUBLANES, n), F32),
        compiler_params=_params("parallel", "parallel"),
        name="ada_mods",
    )(cond8, ada_w, ada_b.reshape(depth, 1, n))


def _ffn_kernel(x_ref, g_ref, sh_ref, sc_ref, gt_ref, wg_ref, wu_ref, wd_ref, *rest, final):
    if final:
        fg_ref, o_ref, h_scr = rest
    else:
        o_ref, h_scr = rest
    f = pl.program_id(1)

    @pl.when(f == 0)
    def _():
        _norm_mod_store(x_ref, g_ref, sh_ref, sc_ref, h_scr)
        o_ref[...] = jnp.zeros_like(o_ref)

    h = h_scr[...]
    gate_act = jnp.dot(h, wg_ref[...], preferred_element_type=F32)
    up = jnp.dot(h, wu_ref[...], preferred_element_type=F32)
    a = (gate_act * jax.nn.sigmoid(gate_act) * up).astype(BF16)
    o_ref[...] += jnp.dot(a, wd_ref[...], preferred_element_type=F32)

    @pl.when(f == pl.num_programs(1) - 1)
    def _():
        out = x_ref[...] + (0.5 * gt_ref[...]) * o_ref[...]
        if final:
            out = out * lax.rsqrt(jnp.mean(out * out, axis=-1, keepdims=True) + EPS) * fg_ref[...]
        o_ref[...] = out


def half_ffn(x, g, shift, scale, gate, wg, wu, wd, final_g=None, tm=512, tf=512):
    rows, d = x.shape
    dff = wg.shape[1]
    tm = min(tm, rows)
    vec = pl.BlockSpec((1, d), lambda i, f: (0, 0))
    in_specs = [
        pl.BlockSpec((tm, d), lambda i, f: (i, 0)),
        vec, vec, vec, vec,
        pl.BlockSpec((d, tf), lambda i, f: (0, f)),
        pl.BlockSpec((d, tf), lambda i, f: (0, f)),
        pl.BlockSpec((tf, d), lambda i, f: (f, 0)),
    ]
    args = [x, g, shift, scale, gate, wg, wu, wd]
    if final_g is not None:
        in_specs.append(vec)
        args.append(final_g)
    return pl.pallas_call(
        functools.partial(_ffn_kernel, final=final_g is not None),
        grid=(rows // tm, dff // tf),
        in_specs=in_specs,
        out_specs=pl.BlockSpec((tm, d), lambda i, f: (i, 0)),
        out_shape=jax.ShapeDtypeStruct((rows, d), F32),
        scratch_shapes=[pltpu.VMEM((tm, d), BF16)],
        compiler_params=_params("parallel", "arbitrary"),
        name="half_ffn",
    )(*args)


def _inproj_kernel(x_ref, g_ref, sh_ref, sc_ref, w_ref, o_ref, h_scr):
    @pl.when(pl.program_id(1) == 0)
    def _():
        _norm_mod_store(x_ref, g_ref, sh_ref, sc_ref, h_scr)

    o_ref[...] = jnp.dot(h_scr[...], w_ref[...], preferred_element_type=F32)


def in_proj(x, g, shift, scale, w, tm=1024, tn=512):
    rows, d = x.shape
    n = w.shape[1]
    tm = min(tm, rows)
    vec = pl.BlockSpec((1, d), lambda i, j: (0, 0))
    return pl.pallas_call(
        _inproj_kernel,
        grid=(rows // tm, n // tn),
        in_specs=[pl.BlockSpec((tm, d), lambda i, j: (i, 0)), vec, vec, vec,
                  pl.BlockSpec((d, tn), lambda i, j: (0, j))],
        out_specs=pl.BlockSpec((tm, tn), lambda i, j: (i, j)),
        out_shape=jax.ShapeDtypeStruct((rows, n), F32),
        scratch_shapes=[pltpu.VMEM((tm, d), BF16)],
        compiler_params=_params("parallel", "arbitrary"),
        name="in_proj",
    )(x, g, shift, scale, w)


HY_CHANNEL_TILE = 512


def _even_inproj_kernel(x_ref, g_ref, sh_ref, sc_ref, wt_ref, w_ref, zt_ref, qkv_ref, h_scr):
    _norm_mod_store(x_ref, g_ref, sh_ref, sc_ref, h_scr)
    tm = x_ref.shape[0]
    ct = HY_CHANNEL_TILE
    contract_last = (((1,), (1,)), ((), ()))
    for j in range(wt_ref.shape[0] // ct):
        zt = lax.dot_general(wt_ref[j * ct:(j + 1) * ct, :], h_scr[...], contract_last,
                             preferred_element_type=F32)
        for r in range(tm // LANES):
            zt_ref[j * ct // SUBLANES:(j + 1) * ct // SUBLANES, r * SUBLANES:(r + 1) * SUBLANES, :] = (
                zt[:, r * LANES:(r + 1) * LANES].reshape(ct // SUBLANES, SUBLANES, LANES))
    qkv_ref[...] = jnp.dot(h_scr[...], w_ref[...], preferred_element_type=F32)


def even_in_proj(x, g, shift, scale, wt, w, tm=512):
    rows, d = x.shape
    nc = wt.shape[0]
    n = w.shape[1]
    vec = pl.BlockSpec((1, d), lambda i: (0, 0))
    once = pl.Buffered(1)
    return pl.pallas_call(
        _even_inproj_kernel,
        grid=(rows // tm,),
        in_specs=[pl.BlockSpec((tm, d), lambda i: (i, 0)), vec, vec, vec,
                  pl.BlockSpec((nc, d), lambda i: (0, 0), pipeline_mode=once),
                  pl.BlockSpec((d, n), lambda i: (0, 0), pipeline_mode=once)],
        out_specs=[pl.BlockSpec((nc // SUBLANES, SUBLANES * tm // LANES, LANES), lambda i: (0, i, 0)),
                   pl.BlockSpec((tm, n), lambda i: (i, 0))],
        out_shape=[jax.ShapeDtypeStruct((nc // SUBLANES, SUBLANES * rows // LANES, LANES), F32),
                   jax.ShapeDtypeStruct((rows, n), F32)],
        scratch_shapes=[pltpu.VMEM((tm, d), BF16)],
        compiler_params=_params("parallel"),
        name="even_in_proj",
    )(x, g, shift, scale, wt, w)


def _filter_kernel(z_ref, t_ref, w1_ref, b1_ref, f1_ref, w2_ref, b2_ref, f2_ref, w3_ref, dl_ref, o_ref):
    hi = lax.Precision.HIGHEST
    ct = w3_ref.shape[0]
    for r in range(z_ref.shape[1] // LANES):
        sl = slice(r * LANES, (r + 1) * LANES)
        a1 = jnp.dot(w1_ref[...], z_ref[:, sl], precision=hi, preferred_element_type=F32)
        h1 = jnp.sin(f1_ref[...] * (a1 + b1_ref[...]))
        a2 = jnp.dot(w2_ref[...], h1, precision=hi, preferred_element_type=F32)
        h2 = jnp.sin(f2_ref[...] * (a2 + b2_ref[...]))
        h3 = jnp.dot(w3_ref[...], h2.astype(BF16), preferred_element_type=F32)
        dec = jnp.exp(-(t_ref[0:1, sl] * dl_ref[...]))
        o_ref[:, r * SUBLANES:(r + 1) * SUBLANES, :] = (h3 * dec).reshape(ct // SUBLANES, SUBLANES, LANES)


def hyena_filter_t(L, w1, b1, f1, w2, b2, f2, w3, tl=1024):
    hid = w1.shape[1]
    c2 = w3.shape[1]
    width = c2 // 2
    t = jnp.linspace(0.0, 1.0, L, dtype=F32)[:, None]
    w = (2.0 * math.pi / L) * jnp.arange(L, dtype=F32)[:, None]
    bands = jnp.linspace(1e-4, HY_BANDS - 1, HY_BANDS, dtype=F32)[None, :]
    z = jnp.concatenate([t, jnp.cos(bands * w), -jnp.sin(bands * w)], axis=-1)
    emb = z.shape[1]
    embp = -(-emb // SUBLANES) * SUBLANES
    zt = jnp.pad(z.T, ((0, embp - emb), (0, 0)))
    w1t = jnp.pad(w1.astype(F32).T, ((0, 0), (0, embp - emb)))
    lt = math.log(HY_DECAY_TARGET)
    deltas = jnp.abs(jnp.linspace(lt / HY_SLOW_PCT, lt / HY_FAST_PCT, width, dtype=F32))
    deltas2 = jnp.concatenate([deltas, deltas])
    col = lambda v: jnp.broadcast_to(v.astype(F32)[:, None], (v.shape[0], LANES))
    trow = jnp.broadcast_to(t.T, (SUBLANES, L))
    tl = min(tl, L)
    full = lambda shape: pl.BlockSpec(shape, lambda i: (0,) * len(shape))
    return pl.pallas_call(
        _filter_kernel,
        grid=(L // tl,),
        in_specs=[
            pl.BlockSpec((embp, tl), lambda i: (0, i)),
            pl.BlockSpec((SUBLANES, tl), lambda i: (0, i)),
            full((hid, embp)), full((hid, LANES)), full((hid, LANES)),
            full((hid, hid)), full((hid, LANES)), full((hid, LANES)),
            full((c2, hid)), full((c2, LANES)),
        ],
        out_specs=pl.BlockSpec((c2 // SUBLANES, SUBLANES * tl // LANES, LANES), lambda i: (0, i, 0)),
        out_shape=jax.ShapeDtypeStruct((c2 // SUBLANES, SUBLANES * L // LANES, LANES), F32),
        compiler_params=_params("parallel"),
        name="hyena_filter",
    )(zt, trow, w1t, col(b1), col(f1), w2.astype(F32).T, col(b2), col(f2), w3.T.astype(BF16), col(deltas2))


def _dft_constants(h1):
    n_outer = 2 * h1
    n = n_outer * LANES
    bf16_rows = 2 * SUBLANES
    kp = -(-(h1 + 1) // bf16_rows) * bf16_rows
    k1 = np.arange(kp)[:, None]
    n1 = np.arange(h1)[None, :]
    ang_a = 2.0 * np.pi * ((k1 * n1) % n_outer) / n_outer
    fa = np.concatenate([np.cos(ang_a), -np.sin(ang_a)], axis=0)
    n2 = np.arange(LANES)[None, :]
    ang_t = 2.0 * np.pi * ((k1 * n2) % n) / n
    twr, twi = np.cos(ang_t), -np.sin(ang_t)
    a = np.arange(LANES)
    ang_b = 2.0 * np.pi * ((a[:, None] * a[None, :]) % LANES) / LANES
    cb, sb = np.cos(ang_b), np.sin(ang_b)
    fb = np.block([[cb, -sb], [sb, cb]])
    gb = np.block([[cb, sb], [-sb, cb]])
    wk = np.where((k1 == 0) | (k1 == h1), 1.0, 2.0) * (k1 <= h1)
    ga = np.concatenate([(wk * np.cos(ang_a)).T, (-wk * np.sin(ang_a)).T], axis=1)
    as_bf = lambda m: jnp.asarray(m, dtype=F32).astype(BF16)
    return (as_bf(fa), as_bf(twr), as_bf(twi), as_bf(fb), as_bf(gb), as_bf(ga), kp, n)


def _hyena_kernel(scal_ref, z0_ref, z1_ref, zv_ref, hf_ref, hb_ref, fa_ref, twr_ref, twi_ref, fb_ref, gb_ref,
                  ga_ref, o_ref, *, h1, kp, inv_n):
    grp = pl.program_id(0)
    row = lax.broadcasted_iota(jnp.int32, (h1, LANES), 0)
    lane = lax.broadcasted_iota(jnp.int32, (h1, LANES), 1)
    first = (row == 0) & (lane == 0)
    last = (row == h1 - 1) & (lane == LANES - 1)

    def chan(ref, ci):
        return ref[0, pl.ds(ci, h1, stride=SUBLANES), :]

    def prev(x):
        r = pltpu.roll(x, 1, 1)
        r = jnp.where(lane == 0, pltpu.roll(r, 1, 0), r)
        return jnp.where(first, 0.0, r)

    def nxt(x):
        r = pltpu.roll(x, LANES - 1, 1)
        r = jnp.where(lane == LANES - 1, pltpu.roll(r, h1 - 1, 0), r)
        return jnp.where(last, 0.0, r)

    us, x0s, hfs, hbs, nrm = [], [], [], [], []
    for ci in range(HY_GROUP):
        c = grp * HY_GROUP + ci

        def sconv(ref, s):
            x = chan(ref, ci)
            return (scal_ref[3 * s, c] * prev(x) + scal_ref[3 * s + 1, c] * x
                    + scal_ref[3 * s + 2, c] * nxt(x) + scal_ref[9 + s, c])

        x0 = sconv(z0_ref, 0)
        x1 = sconv(z1_ref, 1)
        hv = sconv(zv_ref, 2)
        us.append(hv * x1)
        x0s.append(x0)
        hf = chan(hf_ref, ci)
        hb = jnp.where(first, 0.0, chan(hb_ref, ci))
        hfs.append(hf)
        hbs.append(hb)
        ssq = jnp.sum(hf * hf, keepdims=True) + jnp.sum(hb * hb, keepdims=True)
        nrm.append(lax.rsqrt(ssq + EPS))

    fa = fa_ref[...]
    twr, twi = twr_ref[...], twi_ref[...]

    def outer_fwd(mats):
        xc = jnp.concatenate([m.astype(BF16) for m in mats], axis=1)
        a = jnp.dot(fa, xc, preferred_element_type=F32).astype(BF16)
        out = []
        for ci in range(HY_GROUP):
            ar = a[:kp, ci * LANES:(ci + 1) * LANES]
            ai = a[kp:, ci * LANES:(ci + 1) * LANES]
            out.append(jnp.concatenate([ar * twr - ai * twi, ar * twi + ai * twr], axis=1))
        return out

    stacked = jnp.concatenate(outer_fwd(us) + outer_fwd(hfs) + outer_fwd(hbs), axis=0)
    spec = jnp.dot(stacked, fb_ref[...], preferred_element_type=F32).astype(BF16)

    prod = []
    for ci in range(HY_GROUP):
        xu = spec[ci * kp:(ci + 1) * kp]
        xf = spec[(HY_GROUP + ci) * kp:(HY_GROUP + ci + 1) * kp]
        xb = spec[(2 * HY_GROUP + ci) * kp:(2 * HY_GROUP + ci + 1) * kp]
        xr, xi = xu[:, :LANES], xu[:, LANES:]
        kr = xf[:, :LANES] + xb[:, :LANES]
        ki = xf[:, LANES:] - xb[:, LANES:]
        prod.append(jnp.concatenate([xr * kr - xi * ki, xr * ki + xi * kr], axis=1))
    inner = jnp.dot(jnp.concatenate(prod, axis=0), gb_ref[...], preferred_element_type=F32).astype(BF16)

    cols = []
    for ci in range(HY_GROUP):
        b = inner[ci * kp:(ci + 1) * kp]
        br, bi = b[:, :LANES], b[:, LANES:]
        cols.append(jnp.concatenate([br * twr + bi * twi, bi * twr - br * twi], axis=0))
    y = jnp.dot(ga_ref[...], jnp.concatenate(cols, axis=1), preferred_element_type=F32)

    for ci in range(HY_GROUP):
        c = grp * HY_GROUP + ci
        yc = y[:, ci * LANES:(ci + 1) * LANES] * (nrm[ci] * inv_n)
        o_ref[0, pl.ds(ci, h1, stride=SUBLANES), :] = x0s[ci] * (yc + us[ci] * scal_ref[12, c])


def hyena_mix(zt, ht, conv_w, conv_b, skip):
    ng3, r8, _ = zt.shape
    ng = ng3 // 3
    h1 = r8 // SUBLANES
    width = ng * SUBLANES
    fa, twr, twi, fb, gb, ga, kp, n = _dft_constants(h1)
    scal = jnp.concatenate([
        conv_w[:, 0:width], conv_w[:, width:2 * width], conv_w[:, 2 * width:3 * width],
        conv_b.reshape(3, width), skip.reshape(1, width)], axis=0).astype(F32)
    blk = lambda off: pl.BlockSpec((1, r8, LANES), lambda g, off=off: (g + off, 0, 0))
    full = lambda a: pl.BlockSpec(a.shape, lambda g: (0,) * a.ndim)
    return pl.pallas_call(
        functools.partial(_hyena_kernel, h1=h1, kp=kp, inv_n=1.0 / n),
        grid=(ng,),
        in_specs=[pl.BlockSpec(memory_space=pltpu.SMEM),
                  blk(0), blk(ng), blk(2 * ng), blk(0), blk(ng),
                  full(fa), full(twr), full(twi), full(fb), full(gb), full(ga)],
        out_specs=pl.BlockSpec((1, r8, LANES), lambda g: (g, 0, 0)),
        out_shape=jax.ShapeDtypeStruct((ng, r8, LANES), F32),
        compiler_params=_params("parallel"),
        name="hyena_mix",
    )(scal, zt, zt, zt, ht, ht, fa, twr, twi, fb, gb, ga)


def _attn_kernel(sink_ref, q_ref, kp_ref, k0_ref, kn_ref, vp_ref, v0_ref, vn_ref, ccp_ref, cc0_ref, ccn_ref,
                 ssp_ref, ss0_ref, ssn_ref, kc_ref, vc_ref, o_ref, *, scale):
    i = pl.program_id(0)
    nblk = pl.num_programs(0)
    hd, blk = HEAD_DIM, ATT_BLOCK
    nt = (((1,), (1,)), ((), ()))

    def rope(x, cc, ss):
        return x * cc + pltpu.roll(x, hd // 2, 1) * ss

    log2e = math.log2(math.e)
    c2 = scale * log2e
    rows = Q_PER_KV * blk
    off = lax.broadcasted_iota(jnp.int32, (rows, blk), 0) & (blk - 1)
    col = lax.broadcasted_iota(jnp.int32, (rows, blk), 1)
    keep_prev = col >= off + jnp.where(i > 0, 0, blk)
    keep_next = col <= off - jnp.where(i < nblk - 1, 0, blk)
    head_of_row = lax.broadcasted_iota(jnp.int32, (rows, 1), 0) // blk
    rowmax = lambda a: jnp.max(a, axis=-1, keepdims=True)
    rowsum = lambda a: jnp.sum(a, axis=-1, keepdims=True)

    for g in range(N_KV_HEADS):
        gs = slice(g * hd, (g + 1) * hd)
        kb = jnp.concatenate([
            rope(kp_ref[:, gs], ccp_ref[...], ssp_ref[...]),
            rope(k0_ref[:, gs], cc0_ref[...], ss0_ref[...]),
            rope(kn_ref[:, gs], ccn_ref[...], ssn_ref[...])], axis=0).astype(BF16)
        vb = jnp.concatenate([vp_ref[:, gs], v0_ref[:, gs], vn_ref[:, gs]], axis=0).astype(BF16)
        kcg = kc_ref[:, gs].astype(BF16)
        vcg = vc_ref[:, gs].astype(BF16)
        q4 = jnp.concatenate([
            rope(q_ref[:, (g * Q_PER_KV + h) * hd:(g * Q_PER_KV + h + 1) * hd], cc0_ref[...], ss0_ref[...])
            for h in range(Q_PER_KV)], axis=0).astype(BF16)
        s_loc = lax.dot_general(q4, kb, nt, preferred_element_type=F32) * c2
        s_ctx = lax.dot_general(q4, kcg, nt, preferred_element_type=F32) * c2
        s_prev = jnp.where(keep_prev, s_loc[:, :blk], NEG_INF)
        s_own = s_loc[:, blk:2 * blk]
        s_next = jnp.where(keep_next, s_loc[:, 2 * blk:], NEG_INF)
        sink = jnp.zeros((rows, 1), F32)
        for h in range(Q_PER_KV):
            sink = jnp.where(head_of_row == h, sink_ref[g * Q_PER_KV + h] * log2e, sink)
        ctx_tiles = [s_ctx[:, j * LANES:(j + 1) * LANES] for j in range(s_ctx.shape[1] // LANES)]
        m = jnp.maximum(rowmax(functools.reduce(jnp.maximum, [s_prev, s_own, s_next] + ctx_tiles)), sink)
        p_prev, p_own, p_next = jnp.exp2(s_prev - m), jnp.exp2(s_own - m), jnp.exp2(s_next - m)
        p_ctx = jnp.exp2(s_ctx - m)
        p_tiles = [p_prev, p_own, p_next] + [p_ctx[:, j * LANES:(j + 1) * LANES] for j in range(len(ctx_tiles))]
        den = jnp.exp2(sink - m) + rowsum(functools.reduce(jnp.add, p_tiles))
        p_loc = jnp.concatenate([p_prev, p_own, p_next], axis=1).astype(BF16)
        o = (jnp.dot(p_ctx.astype(BF16), vcg, preferred_element_type=F32)
             + jnp.dot(p_loc, vb, preferred_element_type=F32)) * (1.0 / den)
        for h in range(Q_PER_KV):
            o_ref[:, (g * Q_PER_KV + h) * hd:(g * Q_PER_KV + h + 1) * hd] = o[h * blk:(h + 1) * blk].astype(o_ref.dtype)


def window_attention(qkv, kvc, sink):
    L = qkv.shape[0]
    n_ctx = kvc.shape[0]
    nb = L // ATT_BLOCK
    kvw = N_KV_HEADS * HEAD_DIM
    qw = N_KV_HEADS * Q_PER_KV * HEAD_DIM
    kcol, vcol = qw // kvw, qw // kvw + 1
    t = jnp.arange(L, dtype=jnp.int32)
    rowp = (t // GRID_W).astype(F32)
    colp = (t % GRID_W).astype(F32)
    nq = HEAD_DIM // 4
    inv = ROPE_BASE ** (-jnp.arange(nq, dtype=F32) / nq)
    ang = jnp.concatenate([rowp[:, None] * inv, colp[:, None] * inv], axis=-1)
    cos, sin = jnp.cos(ang), jnp.sin(ang)
    cc = jnp.concatenate([cos, cos], axis=-1)
    ss = jnp.concatenate([-sin, sin], axis=-1)
    prv = lambda i: jnp.maximum(i - 1, 0)
    nxt = lambda i: jnp.minimum(i + 1, nb - 1)
    cur = lambda i: i
    kspec = lambda f: pl.BlockSpec((ATT_BLOCK, kvw), lambda i, f=f: (f(i), kcol))
    vspec = lambda f: pl.BlockSpec((ATT_BLOCK, kvw), lambda i, f=f: (f(i), vcol))
    tspec = lambda f: pl.BlockSpec((ATT_BLOCK, HEAD_DIM), lambda i, f=f: (f(i), 0))
    return pl.pallas_call(
        functools.partial(_attn_kernel, scale=HEAD_DIM ** -0.5),
        grid=(nb,),
        in_specs=[pl.BlockSpec(memory_space=pltpu.SMEM),
                  pl.BlockSpec((ATT_BLOCK, qw), lambda i: (i, 0)),
                  kspec(prv), kspec(cur), kspec(nxt), vspec(prv), vspec(cur), vspec(nxt),
                  tspec(prv), tspec(cur), tspec(nxt), tspec(prv), tspec(cur), tspec(nxt),
                  pl.BlockSpec((n_ctx, kvw), lambda i: (0, 0)),
                  pl.BlockSpec((n_ctx, kvw), lambda i: (0, 1))],
        out_specs=pl.BlockSpec((ATT_BLOCK, qw), lambda i: (i, 0)),
        out_shape=jax.ShapeDtypeStruct((L, qw), BF16),
        compiler_params=_params("parallel"),
        name="window_attention",
    )(sink.astype(F32), qkv, qkv, qkv, qkv, qkv, qkv, qkv, cc, cc, cc, ss, ss, ss, kvc, kvc)


def _odd_mixer_kernel(xf_ref, g_ref, sh_ref, sc_ref, win_ref, sgg_ref, ws_ref, bs_ref, wout_ref, x_ref, gt_ref,
                      o_ref, h_scr, vn_scr, sg_scr):
    @pl.when(pl.program_id(1) == 0)
    def _():
        tm, width = sg_scr.shape
        gd = width // SG_GROUPS
        _norm_mod_store(xf_ref, g_ref, sh_ref, sc_ref, h_scr)
        v = jax.nn.gelu(jnp.dot(h_scr[...], win_ref[:, width:], preferred_element_type=F32), approximate=True)
        vn_scr[...] = (v * lax.rsqrt(jnp.mean(v * v, axis=-1, keepdims=True) + EPS) * sgg_ref[...]).astype(BF16)
        for g in range(SG_GROUPS):
            cs = slice(g * gd, (g + 1) * gd)
            u = jax.nn.gelu(jnp.dot(h_scr[...], win_ref[:, cs], preferred_element_type=F32), approximate=True)
            bias = jnp.concatenate([bs_ref[g]] * (gd // LANES), axis=1)
            for ch in range(tm // CHUNK):
                rs = slice(ch * CHUNK, (ch + 1) * CHUNK)
                mixed = jnp.dot(ws_ref[g], vn_scr[rs, cs], preferred_element_type=F32) + bias
                sg_scr[rs, cs] = (u[rs] * mixed).astype(BF16)

    acc = jnp.dot(sg_scr[...], wout_ref[...], preferred_element_type=F32)
    o_ref[...] = x_ref[...] + gt_ref[...] * acc


def odd_mixer(x, g, shift, scale, gate, w_in, sg_g, ws, bs, w_out, tm=512, tn=512):
    rows, d = x.shape
    width = w_in.shape[1] // 2
    n = w_out.shape[1]
    bsb = jnp.broadcast_to(bs.astype(F32)[:, :, None], (SG_GROUPS, CHUNK, LANES))
    vec = lambda w: pl.BlockSpec((1, w), lambda i, j: (0, 0))
    once = pl.Buffered(1)
    return pl.pallas_call(
        _odd_mixer_kernel,
        grid=(rows // tm, n // tn),
        in_specs=[pl.BlockSpec((tm, d), lambda i, j: (i, 0)), vec(d), vec(d), vec(d),
                  pl.BlockSpec((d, 2 * width), lambda i, j: (0, 0), pipeline_mode=once),
                  vec(width),
                  pl.BlockSpec((SG_GROUPS, CHUNK, CHUNK), lambda i, j: (0, 0, 0)),
                  pl.BlockSpec((SG_GROUPS, CHUNK, LANES), lambda i, j: (0, 0, 0)),
                  pl.BlockSpec((width, tn), lambda i, j: (0, j)),
                  pl.BlockSpec((tm, tn), lambda i, j: (i, j)),
                  pl.BlockSpec((1, tn), lambda i, j: (0, j))],
        out_specs=pl.BlockSpec((tm, tn), lambda i, j: (i, j)),
        out_shape=jax.ShapeDtypeStruct((rows, n), F32),
        scratch_shapes=[pltpu.VMEM((tm, d), BF16), pltpu.VMEM((tm, width), BF16), pltpu.VMEM((tm, width), BF16)],
        compiler_params=_params("parallel", "arbitrary"),
        name="odd_mixer",
    )(x, g, shift, scale, w_in, sg_g, ws.astype(BF16), bsb, w_out, x, gate)


def _outproj_even_kernel(yt_ref, ya_ref, w_ref, x_ref, gt_ref, o_ref, lhs_scr):
    ng = yt_ref.shape[0]
    c = ng * SUBLANES
    for r in range(yt_ref.shape[1] // SUBLANES):
        sub = yt_ref[:, r * SUBLANES:(r + 1) * SUBLANES, :].reshape(c, LANES)
        lhs_scr[r * LANES:(r + 1) * LANES, 0:c] = sub.T.astype(BF16)
    lhs_scr[:, c:] = ya_ref[...]
    acc = jnp.dot(lhs_scr[...], w_ref[...], preferred_element_type=F32)
    o_ref[...] = x_ref[...] + gt_ref[...] * acc


def out_proj_even(yt, ya, w, x, gate, tm=512):
    rows, aw = ya.shape
    ng = yt.shape[0]
    k, n = w.shape
    return pl.pallas_call(
        _outproj_even_kernel,
        grid=(rows // tm,),
        in_specs=[pl.BlockSpec((ng, SUBLANES * tm // LANES, LANES), lambda i: (0, i, 0)),
                  pl.BlockSpec((tm, aw), lambda i: (i, 0)),
                  pl.BlockSpec((k, n), lambda i: (0, 0), pipeline_mode=pl.Buffered(1)),
                  pl.BlockSpec((tm, n), lambda i: (i, 0)),
                  pl.BlockSpec((1, n), lambda i: (0, 0))],
        out_specs=pl.BlockSpec((tm, n), lambda i: (i, 0)),
        out_shape=jax.ShapeDtypeStruct((rows, n), F32),
        scratch_shapes=[pltpu.VMEM((tm, k), BF16)],
        compiler_params=_params("parallel"),
        name="out_proj_even",
    )(yt, ya, w, x, gate)


def kernel(x, c, ctx, c_ctx, ada_w, ada_b, norm_g, ffn_wg, ffn_wu, ffn_wd, ev_w_in, ev_conv_w, ev_conv_b,
           hy_w1, hy_b1, hy_f1, hy_w2, hy_b2, hy_f2, hy_w3, hy_skip, att_sink, ev_w_out, od_w_in, sg_g, sg_ws,
           sg_bs, od_w_out, final_g):
    assert x.shape[0] == 1 and ada_w.shape[0] == 2, "written for batch 1, depth 2 (even layer then odd layer)"
    _, L, d = x.shape
    hy_width = hy_skip.shape[1]
    hy_in = 3 * hy_width
    kv_w = N_KV_HEADS * HEAD_DIM
    q_end = hy_in + (d - hy_width)

    xs = x[0]
    xc = ctx[0]
    cond8 = jnp.zeros((SUBLANES, d), F32).at[0].set(c[0]).at[1].set(c_ctx)
    mods = ada_mods(cond8, ada_w, ada_b)
    row = lambda v: v.reshape(1, d)
    bf = lambda w: w.astype(BF16)

    mod = mods[0, 0].reshape(N_MOD, 1, d)
    mc = mods[0, 1].reshape(N_MOD, 1, d)
    g = norm_g[0]
    wg, wu, wd = bf(ffn_wg[0, 0]), bf(ffn_wu[0, 0]), bf(ffn_wd[0, 0])
    xs = half_ffn(xs, row(g[0]), mod[0], mod[1], mod[2], wg, wu, wd)
    xc = half_ffn(xc, row(g[0]), mc[0], mc[1], mc[2], wg, wu, wd)
    w_in = ev_w_in[0]
    zt, qkv = even_in_proj(xs, row(g[1]), mod[3], mod[4], bf(w_in[:, :hy_in].T), bf(w_in[:, hy_in:]))
    kvc = in_proj(xc, row(g[1]), mc[3], mc[4], bf(w_in[:, q_end:]))
    ht = hyena_filter_t(L, hy_w1[0], hy_b1[0], hy_f1[0], hy_w2[0], hy_b2[0], hy_f2[0], hy_w3[0])
    yt = hyena_mix(zt, ht, ev_conv_w[0], ev_conv_b[0], hy_skip[0])
    ya = window_attention(qkv, kvc, att_sink[0])
    xs = out_proj_even(yt, ya, bf(ev_w_out[0]), xs, mod[5])
    xs = half_ffn(xs, row(g[2]), mod[6], mod[7], mod[8], bf(ffn_wg[0, 1]), bf(ffn_wu[0, 1]), bf(ffn_wd[0, 1]))

    mod = mods[1, 0].reshape(N_MOD, 1, d)
    g = norm_g[1]
    xs = half_ffn(xs, row(g[0]), mod[0], mod[1], mod[2], bf(ffn_wg[1, 0]), bf(ffn_wu[1, 0]), bf(ffn_wd[1, 0]))
    xs = odd_mixer(xs, row(g[1]), mod[3], mod[4], mod[5], bf(od_w_in[0]), row(sg_g[0]), sg_ws[0], sg_bs[0],
                   bf(od_w_out[0]))
    xs = half_ffn(xs, row(g[2]), mod[6], mod[7], mod[8], bf(ffn_wg[1, 1]), bf(ffn_wu[1, 1]), bf(ffn_wd[1, 1]),
                  final_g=row(final_g))
    return xs[None]
```

```python
import functools
import math

import numpy as np
import jax
import jax.numpy as jnp
from jax import lax
from jax.experimental import pallas as pl
from jax.experimental.pallas import tpu as pltpu

F32 = jnp.float32
BF16 = jnp.bfloat16

LANES = 128
SUBLANES = 8
VMEM_LIMIT_BYTES = 56 * 1024 * 1024

EPS = 1e-6
NEG_INF = -1e30
N_MOD = 9
HEAD_DIM = 128
N_KV_HEADS = 2
Q_PER_KV = 4
ATT_BLOCK = 128
GRID_W = 64
ROPE_BASE = 10000.0
HY_BANDS = 16
HY_DECAY_TARGET = 1e-2
HY_FAST_PCT = 0.3
HY_SLOW_PCT = 1.5
SG_GROUPS = 8
CHUNK = 128
HY_GROUP = SUBLANES


def _params(*sem):
    return pltpu.CompilerParams(dimension_semantics=sem, vmem_limit_bytes=VMEM_LIMIT_BYTES)


NORM_ROWS = 16


def _norm_mod_store(x_ref, g_ref, sh_ref, sc_ref, h_scr):
    g = g_ref[...]
    one_plus_scale = 1.0 + sc_ref[...]
    shift = sh_ref[...]

    def body(i, carry):
        rows = pl.ds(pl.multiple_of(i * NORM_ROWS, NORM_ROWS), NORM_ROWS)
        x = x_ref[rows, :]
        y = x * lax.rsqrt(jnp.mean(x * x, axis=-1, keepdims=True) + EPS) * g
        h_scr[rows, :] = (y * one_plus_scale + shift).astype(BF16)
        return carry

    lax.fori_loop(0, x_ref.shape[0] // NORM_ROWS, body, 0, unroll=8)


def _ada_kernel(s_ref, w_ref, b_ref, o_ref):
    s = s_ref[...]
    s = (s * jax.nn.sigmoid(s)).astype(BF16)
    o_ref[0] = jnp.dot(s, w_ref[0].astype(BF16), preferred_element_type=F32) + b_ref[0]


def ada_mods(cond8, ada_w, ada_b, tn=1024):
    depth, d, n = ada_w.shape
    return pl.pallas_call(
        _ada_kernel,
        grid=(depth, n // tn),
        in_specs=[
            pl.BlockSpec((SUBLANES, d), lambda l, j: (0, 0)),
            pl.BlockSpec((1, d, tn), lambda l, j: (l, 0, j)),
            pl.BlockSpec((1, 1, tn), lambda l, j: (l, 0, j)),
        ],
        out_specs=pl.BlockSpec((1, SUBLANES, tn), lambda l, j: (l, 0, j)),
        out_shape=jax.ShapeDtypeStruct((depth, SUBLANES, n), F32),
        compiler_params=_params("parallel", "parallel"),
        name="ada_mods",
    )(cond8, ada_w, ada_b.reshape(depth, 1, n))


FFN_ROW_CHUNK = 512


def _ffn_kernel(x_ref, g_ref, sh_ref, sc_ref, gt_ref, wg_ref, wu_ref, wd_ref, *rest, final):
    if final:
        fg_ref, o_ref, h_scr = rest
    else:
        o_ref, h_scr = rest
    f = pl.program_id(1)

    @pl.when(f == 0)
    def _():
        _norm_mod_store(x_ref, g_ref, sh_ref, sc_ref, h_scr)
        o_ref[...] = jnp.zeros_like(o_ref)

    tm = x_ref.shape[0]
    rc = min(tm, FFN_ROW_CHUNK)
    for r in range(tm // rc):
        rs = slice(r * rc, (r + 1) * rc)
        h = h_scr[rs, :]
        gate_act = jnp.dot(h, wg_ref[...], preferred_element_type=F32)
        up = jnp.dot(h, wu_ref[...], preferred_element_type=F32)
        a = (gate_act * jax.nn.sigmoid(gate_act) * up).astype(BF16)
        o_ref[rs, :] += jnp.dot(a, wd_ref[...], preferred_element_type=F32)

    @pl.when(f == pl.num_programs(1) - 1)
    def _():
        half_gate = 0.5 * gt_ref[...]

        def body(i, carry):
            rows = pl.ds(pl.multiple_of(i * NORM_ROWS, NORM_ROWS), NORM_ROWS)
            out = x_ref[rows, :] + half_gate * o_ref[rows, :]
            if final:
                out = out * lax.rsqrt(jnp.mean(out * out, axis=-1, keepdims=True) + EPS) * fg_ref[...]
            o_ref[rows, :] = out
            return carry

        lax.fori_loop(0, tm // NORM_ROWS, body, 0, unroll=8)


def half_ffn(x, g, shift, scale, gate, wg, wu, wd, layer, which, final_g=None, tm=1024, tf=512):
    rows, d = x.shape
    dff = wg.shape[-1]
    tm = min(tm, rows)
    vec = pl.BlockSpec((1, d), lambda i, f: (0, 0))
    in_specs = [
        pl.BlockSpec((tm, d), lambda i, f: (i, 0)),
        vec, vec, vec, vec,
        pl.BlockSpec((None, None, d, tf), lambda i, f: (layer, which, 0, f)),
        pl.BlockSpec((None, None, d, tf), lambda i, f: (layer, which, 0, f)),
        pl.BlockSpec((None, None, tf, d), lambda i, f: (layer, which, f, 0)),
    ]
    args = [x, g, shift, scale, gate, wg, wu, wd]
    if final_g is not None:
        in_specs.append(vec)
        args.append(final_g)
    return pl.pallas_call(
        functools.partial(_ffn_kernel, final=final_g is not None),
        grid=(rows // tm, dff // tf),
        in_specs=in_specs,
        out_specs=pl.BlockSpec((tm, d), lambda i, f: (i, 0)),
        out_shape=jax.ShapeDtypeStruct((rows, d), F32),
        scratch_shapes=[pltpu.VMEM((tm, d), BF16)],
        compiler_params=_params("parallel", "arbitrary"),
        name="half_ffn",
    )(*args)


def _inproj_kernel(x_ref, g_ref, sh_ref, sc_ref, w_ref, o_ref, h_scr):
    @pl.when(pl.program_id(1) == 0)
    def _():
        _norm_mod_store(x_ref, g_ref, sh_ref, sc_ref, h_scr)

    o_ref[...] = jnp.dot(h_scr[...], w_ref[...], preferred_element_type=F32)


def in_proj(x, g, shift, scale, w, tm=1024, tn=512):
    rows, d = x.shape
    n = w.shape[1]
    tm = min(tm, rows)
    vec = pl.BlockSpec((1, d), lambda i, j: (0, 0))
    return pl.pallas_call(
        _inproj_kernel,
        grid=(rows // tm, n // tn),
        in_specs=[pl.BlockSpec((tm, d), lambda i, j: (i, 0)), vec, vec, vec,
                  pl.BlockSpec((d, tn), lambda i, j: (0, j))],
        out_specs=pl.BlockSpec((tm, tn), lambda i, j: (i, j)),
        out_shape=jax.ShapeDtypeStruct((rows, n), F32),
        scratch_shapes=[pltpu.VMEM((tm, d), BF16)],
        compiler_params=_params("parallel", "arbitrary"),
        name="in_proj",
    )(x, g, shift, scale, w)


HY_CHANNEL_TILE = 512


def _even_inproj_kernel(x_ref, g_ref, sh_ref, sc_ref, w_ref, zt_ref, qkv_ref, h_scr):
    _norm_mod_store(x_ref, g_ref, sh_ref, sc_ref, h_scr)
    tm = x_ref.shape[0]
    ct = HY_CHANNEL_TILE
    nc = zt_ref.shape[0] * SUBLANES
    for j in range(nc // ct):
        zt = lax.dot_general(w_ref[:, j * ct:(j + 1) * ct], h_scr[...], (((0,), (1,)), ((), ())),
                             preferred_element_type=F32)
        for r in range(tm // LANES):
            zt_ref[j * ct // SUBLANES:(j + 1) * ct // SUBLANES, r * SUBLANES:(r + 1) * SUBLANES, :] = (
                zt[:, r * LANES:(r + 1) * LANES].reshape(ct // SUBLANES, SUBLANES, LANES))
    qkv_ref[...] = jnp.dot(h_scr[...], w_ref[:, nc:], preferred_element_type=F32)


def even_in_proj(x, g, shift, scale, w, nc, tm=512):
    rows, d = x.shape
    n = w.shape[1] - nc
    vec = pl.BlockSpec((1, d), lambda i: (0, 0))
    return pl.pallas_call(
        _even_inproj_kernel,
        grid=(rows // tm,),
        in_specs=[pl.BlockSpec((tm, d), lambda i: (i, 0)), vec, vec, vec,
                  pl.BlockSpec((d, nc + n), lambda i: (0, 0), pipeline_mode=pl.Buffered(1))],
        out_specs=[pl.BlockSpec((nc // SUBLANES, SUBLANES * tm // LANES, LANES), lambda i: (0, i, 0)),
                   pl.BlockSpec((tm, n), lambda i: (i, 0))],
        out_shape=[jax.ShapeDtypeStruct((nc // SUBLANES, SUBLANES * rows // LANES, LANES), F32),
                   jax.ShapeDtypeStruct((rows, n), F32)],
        scratch_shapes=[pltpu.VMEM((tm, d), BF16)],
        compiler_params=_params("parallel"),
        name="even_in_proj",
    )(x, g, shift, scale, w)


def _filter_kernel(z_ref, t_ref, w1_ref, b1_ref, f1_ref, w2_ref, b2_ref, f2_ref, w3_ref, dl_ref, o_ref):
    hi = lax.Precision.HIGHEST
    ct = w3_ref.shape[0]
    for r in range(z_ref.shape[1] // LANES):
        sl = slice(r * LANES, (r + 1) * LANES)
        a1 = jnp.dot(w1_ref[...], z_ref[:, sl], precision=hi, preferred_element_type=F32)
        h1 = jnp.sin(f1_ref[...] * (a1 + b1_ref[...]))
        a2 = jnp.dot(w2_ref[...], h1, precision=hi, preferred_element_type=F32)
        h2 = jnp.sin(f2_ref[...] * (a2 + b2_ref[...]))
        h3 = jnp.dot(w3_ref[...], h2.astype(BF16), preferred_element_type=F32)
        dec = jnp.exp(-(t_ref[0:1, sl] * dl_ref[...]))
        o_ref[:, r * SUBLANES:(r + 1) * SUBLANES, :] = (h3 * dec).reshape(ct // SUBLANES, SUBLANES, LANES)


def hyena_filter_t(L, w1, b1, f1, w2, b2, f2, w3, tl=1024):
    hid = w1.shape[1]
    c2 = w3.shape[1]
    width = c2 // 2
    t = jnp.linspace(0.0, 1.0, L, dtype=F32)[:, None]
    w = (2.0 * math.pi / L) * jnp.arange(L, dtype=F32)[:, None]
    bands = jnp.linspace(1e-4, HY_BANDS - 1, HY_BANDS, dtype=F32)[None, :]
    z = jnp.concatenate([t, jnp.cos(bands * w), -jnp.sin(bands * w)], axis=-1)
    emb = z.shape[1]
    embp = -(-emb // SUBLANES) * SUBLANES
    zt = jnp.pad(z.T, ((0, embp - emb), (0, 0)))
    w1t = jnp.pad(w1.astype(F32).T, ((0, 0), (0, embp - emb)))
    lt = math.log(HY_DECAY_TARGET)
    deltas = jnp.abs(jnp.linspace(lt / HY_SLOW_PCT, lt / HY_FAST_PCT, width, dtype=F32))
    deltas2 = jnp.concatenate([deltas, deltas])
    col = lambda v: jnp.broadcast_to(v.astype(F32)[:, None], (v.shape[0], LANES))
    trow = jnp.broadcast_to(t.T, (SUBLANES, L))
    tl = min(tl, L)
    full = lambda shape: pl.BlockSpec(shape, lambda i: (0,) * len(shape))
    return pl.pallas_call(
        _filter_kernel,
        grid=(L // tl,),
        in_specs=[
            pl.BlockSpec((embp, tl), lambda i: (0, i)),
            pl.BlockSpec((SUBLANES, tl), lambda i: (0, i)),
            full((hid, embp)), full((hid, LANES)), full((hid, LANES)),
            full((hid, hid)), full((hid, LANES)), full((hid, LANES)),
            full((c2, hid)), full((c2, LANES)),
        ],
        out_specs=pl.BlockSpec((c2 // SUBLANES, SUBLANES * tl // LANES, LANES), lambda i: (0, i, 0)),
        out_shape=jax.ShapeDtypeStruct((c2 // SUBLANES, SUBLANES * L // LANES, LANES), F32),
        compiler_params=_params("parallel"),
        name="hyena_filter",
    )(zt, trow, w1t, col(b1), col(f1), w2.astype(F32).T, col(b2), col(f2), w3.T.astype(BF16), col(deltas2))


def _dft_constants(h1):
    n_outer = 2 * h1
    n = n_outer * LANES
    bf16_rows = 2 * SUBLANES
    kp = -(-(h1 + 1) // bf16_rows) * bf16_rows
    k1 = np.arange(kp)[:, None]
    n1 = np.arange(h1)[None, :]
    ang_a = 2.0 * np.pi * ((k1 * n1) % n_outer) / n_outer
    fa = np.concatenate([np.cos(ang_a), -np.sin(ang_a)], axis=0)
    n2 = np.arange(LANES)[None, :]
    ang_t = 2.0 * np.pi * ((k1 * n2) % n) / n
    twr, twi = np.cos(ang_t), -np.sin(ang_t)
    a = np.arange(LANES)
    ang_b = 2.0 * np.pi * ((a[:, None] * a[None, :]) % LANES) / LANES
    cb, sb = np.cos(ang_b), np.sin(ang_b)
    fb = np.block([[cb, -sb], [sb, cb]])
    gb = np.block([[cb, sb], [-sb, cb]])
    wk = np.where((k1 == 0) | (k1 == h1), 1.0, 2.0) * (k1 <= h1)
    ga = np.concatenate([(wk * np.cos(ang_a)).T, (-wk * np.sin(ang_a)).T], axis=1)
    as_bf = lambda m: jnp.asarray(m, dtype=F32).astype(BF16)
    return (as_bf(fa), as_bf(twr), as_bf(twi), as_bf(fb), as_bf(gb), as_bf(ga), kp, n)


def _hyena_kernel(scal_ref, z0_ref, z1_ref, zv_ref, hf_ref, hb_ref, fa_ref, twr_ref, twi_ref, fb_ref, gb_ref,
                  ga_ref, o_ref, *, h1, kp, inv_n):
    grp = pl.program_id(0)
    row = lax.broadcasted_iota(jnp.int32, (h1, LANES), 0)
    lane = lax.broadcasted_iota(jnp.int32, (h1, LANES), 1)
    first = (row == 0) & (lane == 0)
    last = (row == h1 - 1) & (lane == LANES - 1)

    def chan(ref, ci):
        return ref[0, pl.ds(ci, h1, stride=SUBLANES), :]

    def prev(x):
        r = pltpu.roll(x, 1, 1)
        r = jnp.where(lane == 0, pltpu.roll(r, 1, 0), r)
        return jnp.where(first, 0.0, r)

    def nxt(x):
        r = pltpu.roll(x, LANES - 1, 1)
        r = jnp.where(lane == LANES - 1, pltpu.roll(r, h1 - 1, 0), r)
        return jnp.where(last, 0.0, r)

    us, x0s, hfs, hbs, nrm = [], [], [], [], []
    for ci in range(HY_GROUP):
        c = grp * HY_GROUP + ci

        def sconv(ref, s):
            x = chan(ref, ci)
            return (scal_ref[3 * s, c] * prev(x) + scal_ref[3 * s + 1, c] * x
                    + scal_ref[3 * s + 2, c] * nxt(x) + scal_ref[9 + s, c])

        x0 = sconv(z0_ref, 0)
        x1 = sconv(z1_ref, 1)
        hv = sconv(zv_ref, 2)
        us.append(hv * x1)
        x0s.append(x0)
        hf = chan(hf_ref, ci)
        hb = jnp.where(first, 0.0, chan(hb_ref, ci))
        hfs.append(hf)
        hbs.append(hb)
        ssq = jnp.sum(hf * hf, keepdims=True) + jnp.sum(hb * hb, keepdims=True)
        nrm.append(lax.rsqrt(ssq + EPS))

    fa = fa_ref[...]
    twr, twi = twr_ref[...], twi_ref[...]

    def outer_fwd(mats):
        xc = jnp.concatenate([m.astype(BF16) for m in mats], axis=1)
        a = jnp.dot(fa, xc, preferred_element_type=F32).astype(BF16)
        out = []
        for ci in range(HY_GROUP):
            ar = a[:kp, ci * LANES:(ci + 1) * LANES]
            ai = a[kp:, ci * LANES:(ci + 1) * LANES]
            out.append(jnp.concatenate([ar * twr - ai * twi, ar * twi + ai * twr], axis=1))
        return out

    stacked = jnp.concatenate(outer_fwd(us) + outer_fwd(hfs) + outer_fwd(hbs), axis=0)
    spec = jnp.dot(stacked, fb_ref[...], preferred_element_type=F32).astype(BF16)

    prod = []
    for ci in range(HY_GROUP):
        xu = spec[ci * kp:(ci + 1) * kp]
        xf = spec[(HY_GROUP + ci) * kp:(HY_GROUP + ci + 1) * kp]
        xb = spec[(2 * HY_GROUP + ci) * kp:(2 * HY_GROUP + ci + 1) * kp]
        xr, xi = xu[:, :LANES], xu[:, LANES:]
        kr = xf[:, :LANES] + xb[:, :LANES]
        ki = xf[:, LANES:] - xb[:, LANES:]
        prod.append(jnp.concatenate([xr * kr - xi * ki, xr * ki + xi * kr], axis=1))
    inner = jnp.dot(jnp.concatenate(prod, axis=0), gb_ref[...], preferred_element_type=F32).astype(BF16)

    cols = []
    for ci in range(HY_GROUP):
        b = inner[ci * kp:(ci + 1) * kp]
        br, bi = b[:, :LANES], b[:, LANES:]
        cols.append(jnp.concatenate([br * twr + bi * twi, bi * twr - br * twi], axis=0))
    y = jnp.dot(ga_ref[...], jnp.concatenate(cols, axis=1), preferred_element_type=F32)

    for ci in range(HY_GROUP):
        c = grp * HY_GROUP + ci
        yc = y[:, ci * LANES:(ci + 1) * LANES] * (nrm[ci] * inv_n)
        o_ref[0, pl.ds(ci, h1, stride=SUBLANES), :] = x0s[ci] * (yc + us[ci] * scal_ref[12, c])


def hyena_mix(zt, ht, conv_w, conv_b, skip):
    ng3, r8, _ = zt.shape
    ng = ng3 // 3
    h1 = r8 // SUBLANES
    width = ng * SUBLANES
    fa, twr, twi, fb, gb, ga, kp, n = _dft_constants(h1)
    scal = jnp.concatenate([
        conv_w[:, 0:width], conv_w[:, width:2 * width], conv_w[:, 2 * width:3 * width],
        conv_b.reshape(3, width), skip.reshape(1, width)], axis=0).astype(F32)
    blk = lambda off: pl.BlockSpec((1, r8, LANES), lambda g, off=off: (g + off, 0, 0))
    full = lambda a: pl.BlockSpec(a.shape, lambda g: (0,) * a.ndim)
    return pl.pallas_call(
        functools.partial(_hyena_kernel, h1=h1, kp=kp, inv_n=1.0 / n),
        grid=(ng,),
        in_specs=[pl.BlockSpec(memory_space=pltpu.SMEM),
                  blk(0), blk(ng), blk(2 * ng), blk(0), blk(ng),
                  full(fa), full(twr), full(twi), full(fb), full(gb), full(ga)],
        out_specs=pl.BlockSpec((1, r8, LANES), lambda g: (g, 0, 0)),
        out_shape=jax.ShapeDtypeStruct((ng, r8, LANES), F32),
        compiler_params=_params("parallel"),
        name="hyena_mix",
    )(scal, zt, zt, zt, ht, ht, fa, twr, twi, fb, gb, ga)


def _attn_kernel(sink_ref, q_ref, kp_ref, k0_ref, kn_ref, vp_ref, v0_ref, vn_ref, ccp_ref, cc0_ref, ccn_ref,
                 ssp_ref, ss0_ref, ssn_ref, kc_ref, vc_ref, o_ref, *, scale):
    i = pl.program_id(0)
    nblk = pl.num_programs(0)
    hd, blk = HEAD_DIM, ATT_BLOCK
    nt = (((1,), (1,)), ((), ()))

    def rope(x, cc, ss):
        return x * cc + pltpu.roll(x, hd // 2, 1) * ss

    log2e = math.log2(math.e)
    c2 = scale * log2e
    rows = Q_PER_KV * blk
    off = lax.broadcasted_iota(jnp.int32, (rows, blk), 0) & (blk - 1)
    col = lax.broadcasted_iota(jnp.int32, (rows, blk), 1)
    keep_prev = col >= off + jnp.where(i > 0, 0, blk)
    keep_next = col <= off - jnp.where(i < nblk - 1, 0, blk)
    head_of_row = lax.broadcasted_iota(jnp.int32, (rows, 1), 0) // blk
    rowmax = lambda a: jnp.max(a, axis=-1, keepdims=True)
    rowsum = lambda a: jnp.sum(a, axis=-1, keepdims=True)

    for g in range(N_KV_HEADS):
        gs = slice(g * hd, (g + 1) * hd)
        kb = jnp.concatenate([
            rope(kp_ref[:, gs], ccp_ref[...], ssp_ref[...]),
            rope(k0_ref[:, gs], cc0_ref[...], ss0_ref[...]),
            rope(kn_ref[:, gs], ccn_ref[...], ssn_ref[...])], axis=0).astype(BF16)
        vb = jnp.concatenate([vp_ref[:, gs], v0_ref[:, gs], vn_ref[:, gs]], axis=0).astype(BF16)
        kcg = kc_ref[:, gs].astype(BF16)
        vcg = vc_ref[:, gs].astype(BF16)
        q4 = jnp.concatenate([
            rope(q_ref[:, (g * Q_PER_KV + h) * hd:(g * Q_PER_KV + h + 1) * hd], cc0_ref[...], ss0_ref[...])
            for h in range(Q_PER_KV)], axis=0).astype(BF16)
        s_loc = lax.dot_general(q4, kb, nt, preferred_element_type=F32) * c2
        s_ctx = lax.dot_general(q4, kcg, nt, preferred_element_type=F32) * c2
        s_prev = jnp.where(keep_prev, s_loc[:, :blk], NEG_INF)
        s_own = s_loc[:, blk:2 * blk]
        s_next = jnp.where(keep_next, s_loc[:, 2 * blk:], NEG_INF)
        sink = jnp.zeros((rows, 1), F32)
        for h in range(Q_PER_KV):
            sink = jnp.where(head_of_row == h, sink_ref[g * Q_PER_KV + h] * log2e, sink)
        ctx_tiles = [s_ctx[:, j * LANES:(j + 1) * LANES] for j in range(s_ctx.shape[1] // LANES)]
        m = jnp.maximum(rowmax(functools.reduce(jnp.maximum, [s_prev, s_own, s_next] + ctx_tiles)), sink)
        p_prev, p_own, p_next = jnp.exp2(s_prev - m), jnp.exp2(s_own - m), jnp.exp2(s_next - m)
        p_ctx = jnp.exp2(s_ctx - m)
        p_tiles = [p_prev, p_own, p_next] + [p_ctx[:, j * LANES:(j + 1) * LANES] for j in range(len(ctx_tiles))]
        den = jnp.exp2(sink - m) + rowsum(functools.reduce(jnp.add, p_tiles))
        p_loc = jnp.concatenate([p_prev, p_own, p_next], axis=1).astype(BF16)
        o = (jnp.dot(p_ctx.astype(BF16), vcg, preferred_element_type=F32)
             + jnp.dot(p_loc, vb, preferred_element_type=F32)) * (1.0 / den)
        for h in range(Q_PER_KV):
            o_ref[:, (g * Q_PER_KV + h) * hd:(g * Q_PER_KV + h + 1) * hd] = o[h * blk:(h + 1) * blk].astype(o_ref.dtype)


def window_attention(qkv, kvc, sink):
    L = qkv.shape[0]
    n_ctx = kvc.shape[0]
    nb = L // ATT_BLOCK
    kvw = N_KV_HEADS * HEAD_DIM
    qw = N_KV_HEADS * Q_PER_KV * HEAD_DIM
    kcol, vcol = qw // kvw, qw // kvw + 1
    t = jnp.arange(L, dtype=jnp.int32)
    rowp = (t // GRID_W).astype(F32)
    colp = (t % GRID_W).astype(F32)
    nq = HEAD_DIM // 4
    inv = ROPE_BASE ** (-jnp.arange(nq, dtype=F32) / nq)
    ang = jnp.concatenate([rowp[:, None] * inv, colp[:, None] * inv], axis=-1)
    cos, sin = jnp.cos(ang), jnp.sin(ang)
    cc = jnp.concatenate([cos, cos], axis=-1)
    ss = jnp.concatenate([-sin, sin], axis=-1)
    prv = lambda i: jnp.maximum(i - 1, 0)
    nxt = lambda i: jnp.minimum(i + 1, nb - 1)
    cur = lambda i: i
    kspec = lambda f: pl.BlockSpec((ATT_BLOCK, kvw), lambda i, f=f: (f(i), kcol))
    vspec = lambda f: pl.BlockSpec((ATT_BLOCK, kvw), lambda i, f=f: (f(i), vcol))
    tspec = lambda f: pl.BlockSpec((ATT_BLOCK, HEAD_DIM), lambda i, f=f: (f(i), 0))
    return pl.pallas_call(
        functools.partial(_attn_kernel, scale=HEAD_DIM ** -0.5),
        grid=(nb,),
        in_specs=[pl.BlockSpec(memory_space=pltpu.SMEM),
                  pl.BlockSpec((ATT_BLOCK, qw), lambda i: (i, 0)),
                  kspec(prv), kspec(cur), kspec(nxt), vspec(prv), vspec(cur), vspec(nxt),
                  tspec(prv), tspec(cur), tspec(nxt), tspec(prv), tspec(cur), tspec(nxt),
                  pl.BlockSpec((n_ctx, kvw), lambda i: (0, 0)),
                  pl.BlockSpec((n_ctx, kvw), lambda i: (0, 1))],
        out_specs=pl.BlockSpec((ATT_BLOCK, qw), lambda i: (i, 0)),
        out_shape=jax.ShapeDtypeStruct((L, qw), BF16),
        compiler_params=_params("parallel"),
        name="window_attention",
    )(sink.astype(F32), qkv, qkv, qkv, qkv, qkv, qkv, qkv, cc, cc, cc, ss, ss, ss, kvc, kvc)


def _odd_mixer_kernel(xf_ref, g_ref, sh_ref, sc_ref, win_ref, sgg_ref, ws_ref, bs_ref, wout_ref, x_ref, gt_ref,
                      o_ref, h_scr, vn_scr, sg_scr):
    @pl.when(pl.program_id(1) == 0)
    def _():
        tm, width = sg_scr.shape
        gd = width // SG_GROUPS
        _norm_mod_store(xf_ref, g_ref, sh_ref, sc_ref, h_scr)
        v = jax.nn.gelu(jnp.dot(h_scr[...], win_ref[:, width:], preferred_element_type=F32), approximate=True)
        vn_scr[...] = (v * lax.rsqrt(jnp.mean(v * v, axis=-1, keepdims=True) + EPS) * sgg_ref[...]).astype(BF16)
        for g in range(SG_GROUPS):
            cs = slice(g * gd, (g + 1) * gd)
            u = jax.nn.gelu(jnp.dot(h_scr[...], win_ref[:, cs], preferred_element_type=F32), approximate=True)
            bias = jnp.concatenate([bs_ref[g]] * (gd // LANES), axis=1)
            for ch in range(tm // CHUNK):
                rs = slice(ch * CHUNK, (ch + 1) * CHUNK)
                mixed = jnp.dot(ws_ref[g], vn_scr[rs, cs], preferred_element_type=F32) + bias
                sg_scr[rs, cs] = (u[rs] * mixed).astype(BF16)

    acc = jnp.dot(sg_scr[...], wout_ref[...], preferred_element_type=F32)
    o_ref[...] = x_ref[...] + gt_ref[...] * acc


def odd_mixer(x, g, shift, scale, gate, w_in, sg_g, ws, bs, w_out, tm=512, tn=512):
    rows, d = x.shape
    width = w_in.shape[1] // 2
    n = w_out.shape[1]
    bsb = jnp.broadcast_to(bs.astype(F32)[:, :, None], (SG_GROUPS, CHUNK, LANES))
    vec = lambda w: pl.BlockSpec((1, w), lambda i, j: (0, 0))
    once = pl.Buffered(1)
    return pl.pallas_call(
        _odd_mixer_kernel,
        grid=(rows // tm, n // tn),
        in_specs=[pl.BlockSpec((tm, d), lambda i, j: (i, 0)), vec(d), vec(d), vec(d),
                  pl.BlockSpec((d, 2 * width), lambda i, j: (0, 0), pipeline_mode=once),
                  vec(width),
                  pl.BlockSpec((SG_GROUPS, CHUNK, CHUNK), lambda i, j: (0, 0, 0)),
                  pl.BlockSpec((SG_GROUPS, CHUNK, LANES), lambda i, j: (0, 0, 0)),
                  pl.BlockSpec((width, tn), lambda i, j: (0, j)),
                  pl.BlockSpec((tm, tn), lambda i, j: (i, j)),
                  pl.BlockSpec((1, tn), lambda i, j: (0, j))],
        out_specs=pl.BlockSpec((tm, tn), lambda i, j: (i, j)),
        out_shape=jax.ShapeDtypeStruct((rows, n), F32),
        scratch_shapes=[pltpu.VMEM((tm, d), BF16), pltpu.VMEM((tm, width), BF16), pltpu.VMEM((tm, width), BF16)],
        compiler_params=_params("parallel", "arbitrary"),
        name="odd_mixer",
    )(x, g, shift, scale, w_in, sg_g, ws.astype(BF16), bsb, w_out, x, gate)


def _outproj_even_kernel(yt_ref, ya_ref, w_ref, x_ref, gt_ref, o_ref, lhs_scr):
    ng = yt_ref.shape[0]
    c = ng * SUBLANES
    for r in range(yt_ref.shape[1] // SUBLANES):
        sub = yt_ref[:, r * SUBLANES:(r + 1) * SUBLANES, :].reshape(c, LANES)
        lhs_scr[r * LANES:(r + 1) * LANES, 0:c] = sub.T.astype(BF16)
    lhs_scr[:, c:] = ya_ref[...]
    acc = jnp.dot(lhs_scr[...], w_ref[...], preferred_element_type=F32)
    o_ref[...] = x_ref[...] + gt_ref[...] * acc


def out_proj_even(yt, ya, w, x, gate, tm=512):
    rows, aw = ya.shape
    ng = yt.shape[0]
    k, n = w.shape
    return pl.pallas_call(
        _outproj_even_kernel,
        grid=(rows // tm,),
        in_specs=[pl.BlockSpec((ng, SUBLANES * tm // LANES, LANES), lambda i: (0, i, 0)),
                  pl.BlockSpec((tm, aw), lambda i: (i, 0)),
                  pl.BlockSpec((k, n), lambda i: (0, 0), pipeline_mode=pl.Buffered(1)),
                  pl.BlockSpec((tm, n), lambda i: (i, 0)),
                  pl.BlockSpec((1, n), lambda i: (0, 0))],
        out_specs=pl.BlockSpec((tm, n), lambda i: (i, 0)),
        out_shape=jax.ShapeDtypeStruct((rows, n), F32),
        scratch_shapes=[pltpu.VMEM((tm, k), BF16)],
        compiler_params=_params("parallel"),
        name="out_proj_even",
    )(yt, ya, w, x, gate)


def kernel(x, c, ctx, c_ctx, ada_w, ada_b, norm_g, ffn_wg, ffn_wu, ffn_wd, ev_w_in, ev_conv_w, ev_conv_b,
           hy_w1, hy_b1, hy_f1, hy_w2, hy_b2, hy_f2, hy_w3, hy_skip, att_sink, ev_w_out, od_w_in, sg_g, sg_ws,
           sg_bs, od_w_out, final_g):
    assert x.shape[0] == 1 and ada_w.shape[0] == 2, "written for batch 1, depth 2 (even layer then odd layer)"
    _, L, d = x.shape
    hy_width = hy_skip.shape[1]
    hy_in = 3 * hy_width
    kv_w = N_KV_HEADS * HEAD_DIM
    q_end = hy_in + (d - hy_width)

    xs = x[0]
    xc = ctx[0]
    cond8 = jnp.zeros((SUBLANES, d), F32).at[0].set(c[0]).at[1].set(c_ctx)
    mods = ada_mods(cond8, ada_w, ada_b)
    row = lambda v: v.reshape(1, d)
    bf = lambda w: w.astype(BF16)

    mod = mods[0, 0].reshape(N_MOD, 1, d)
    mc = mods[0, 1].reshape(N_MOD, 1, d)
    g = norm_g[0]
    wg, wu, wd = bf(ffn_wg), bf(ffn_wu), bf(ffn_wd)
    xs = half_ffn(xs, row(g[0]), mod[0], mod[1], mod[2], wg, wu, wd, 0, 0)
    xc = half_ffn(xc, row(g[0]), mc[0], mc[1], mc[2], wg, wu, wd, 0, 0)
    w_in = bf(ev_w_in[0])
    zt, qkv = even_in_proj(xs, row(g[1]), mod[3], mod[4], w_in, hy_in)
    kvc = in_proj(xc, row(g[1]), mc[3], mc[4], w_in[:, q_end:])
    ht = hyena_filter_t(L, hy_w1[0], hy_b1[0], hy_f1[0], hy_w2[0], hy_b2[0], hy_f2[0], hy_w3[0])
    yt = hyena_mix(zt, ht, ev_conv_w[0], ev_conv_b[0], hy_skip[0])
    ya = window_attention(qkv, kvc, att_sink[0])
    xs = out_proj_even(yt, ya, bf(ev_w_out[0]), xs, mod[5])
    xs = half_ffn(xs, row(g[2]), mod[6], mod[7], mod[8], wg, wu, wd, 0, 1)

    mod = mods[1, 0].reshape(N_MOD, 1, d)
    g = norm_g[1]
    xs = half_ffn(xs, row(g[0]), mod[0], mod[1], mod[2], wg, wu, wd, 1, 0)
    xs = odd_mixer(xs, row(g[1]), mod[3], mod[4], mod[5], bf(od_w_in[0]), row(sg_g[0]), sg_ws[0], sg_bs[0],
                   bf(od_w_out[0]))
    xs = half_ffn(xs, row(g[2]), mod[6], mod[7], mod[8], wg, wu, wd, 1, 1, final_g=row(final_g))
    return xs[None]
```

```python
import functools
import math

import numpy as np
import jax
import jax.numpy as jnp
from jax import lax
from jax.experimental import pallas as pl
from jax.experimental.pallas import tpu as pltpu

F32 = jnp.float32
BF16 = jnp.bfloat16

LANES = 128
SUBLANES = 8
VMEM_LIMIT_BYTES = 56 * 1024 * 1024

EPS = 1e-6
NEG_INF = -1e30
N_MOD = 9
HEAD_DIM = 128
N_KV_HEADS = 2
Q_PER_KV = 4
ATT_BLOCK = 128
GRID_W = 64
ROPE_BASE = 10000.0
HY_BANDS = 16
HY_DECAY_TARGET = 1e-2
HY_FAST_PCT = 0.3
HY_SLOW_PCT = 1.5
SG_GROUPS = 8
CHUNK = 128
HY_GROUP = SUBLANES


def _params(*sem):
    return pltpu.CompilerParams(dimension_semantics=sem, vmem_limit_bytes=VMEM_LIMIT_BYTES)


NORM_ROWS = 16


def _norm_mod_store(x_ref, g_ref, sh_ref, sc_ref, h_scr):
    g = g_ref[...]
    one_plus_scale = 1.0 + sc_ref[...]
    shift = sh_ref[...]

    def body(i, carry):
        rows = pl.ds(pl.multiple_of(i * NORM_ROWS, NORM_ROWS), NORM_ROWS)
        x = x_ref[rows, :]
        y = x * lax.rsqrt(jnp.mean(x * x, axis=-1, keepdims=True) + EPS) * g
        h_scr[rows, :] = (y * one_plus_scale + shift).astype(BF16)
        return carry

    lax.fori_loop(0, x_ref.shape[0] // NORM_ROWS, body, 0, unroll=8)


def _ada_kernel(s_ref, w_ref, b_ref, o_ref):
    s = s_ref[...]
    s = (s * jax.nn.sigmoid(s)).astype(BF16)
    o_ref[0] = jnp.dot(s, w_ref[0].astype(BF16), preferred_element_type=F32) + b_ref[0]


def ada_mods(cond8, ada_w, ada_b, tn=1024):
    depth, d, n = ada_w.shape
    return pl.pallas_call(
        _ada_kernel,
        grid=(depth, n // tn),
        in_specs=[
            pl.BlockSpec((SUBLANES, d), lambda l, j: (0, 0)),
            pl.BlockSpec((1, d, tn), lambda l, j: (l, 0, j)),
            pl.BlockSpec((1, 1, tn), lambda l, j: (l, 0, j)),
        ],
        out_specs=pl.BlockSpec((1, SUBLANES, tn), lambda l, j: (l, 0, j)),
        out_shape=jax.ShapeDtypeStruct((depth, SUBLANES, n), F32),
        compiler_params=_params("parallel", "parallel"),
        name="ada_mods",
    )(cond8, ada_w, ada_b.reshape(depth, 1, n))


FFN_ROW_CHUNK = 512


def _ffn_kernel(x_ref, g_ref, sh_ref, sc_ref, gt_ref, wg_ref, wu_ref, wd_ref, *rest, final):
    if final:
        fg_ref, o_ref, h_scr = rest
    else:
        o_ref, h_scr = rest
    f = pl.program_id(1)

    @pl.when(f == 0)
    def _():
        _norm_mod_store(x_ref, g_ref, sh_ref, sc_ref, h_scr)
        o_ref[...] = jnp.zeros_like(o_ref)

    tm = x_ref.shape[0]
    rc = min(tm, FFN_ROW_CHUNK)
    for r in range(tm // rc):
        rs = slice(r * rc, (r + 1) * rc)
        h = h_scr[rs, :]
        gate_act = jnp.dot(h, wg_ref[...], preferred_element_type=F32)
        up = jnp.dot(h, wu_ref[...], preferred_element_type=F32)
        a = (gate_act * jax.nn.sigmoid(gate_act) * up).astype(BF16)
        o_ref[rs, :] += jnp.dot(a, wd_ref[...], preferred_element_type=F32)

    @pl.when(f == pl.num_programs(1) - 1)
    def _():
        half_gate = 0.5 * gt_ref[...]
        group = 8

        def body(i, carry):
            chunk = lambda k: pl.ds(pl.multiple_of((i * group + k) * NORM_ROWS, NORM_ROWS), NORM_ROWS)
            resid = lambda rows: x_ref[rows, :] + half_gate * o_ref[rows, :]
            if final:
                inv = [lax.rsqrt(jnp.mean(jnp.square(resid(chunk(k))), axis=-1, keepdims=True) + EPS)
                       for k in range(group)]
                for k in range(group):
                    o_ref[chunk(k), :] = resid(chunk(k)) * inv[k] * fg_ref[...]
            else:
                for k in range(group):
                    o_ref[chunk(k), :] = resid(chunk(k))
            return carry

        lax.fori_loop(0, tm // (NORM_ROWS * group), body, 0)


def half_ffn(x, g, shift, scale, gate, wg, wu, wd, layer, which, final_g=None, tm=1024, tf=512):
    rows, d = x.shape
    dff = wg.shape[-1]
    tm = min(tm, rows)
    vec = pl.BlockSpec((1, d), lambda i, f: (0, 0))
    in_specs = [
        pl.BlockSpec((tm, d), lambda i, f: (i, 0)),
        vec, vec, vec, vec,
        pl.BlockSpec((None, None, d, tf), lambda i, f: (layer, which, 0, f)),
        pl.BlockSpec((None, None, d, tf), lambda i, f: (layer, which, 0, f)),
        pl.BlockSpec((None, None, tf, d), lambda i, f: (layer, which, f, 0)),
    ]
    args = [x, g, shift, scale, gate, wg, wu, wd]
    if final_g is not None:
        in_specs.append(vec)
        args.append(final_g)
    return pl.pallas_call(
        functools.partial(_ffn_kernel, final=final_g is not None),
        grid=(rows // tm, dff // tf),
        in_specs=in_specs,
        out_specs=pl.BlockSpec((tm, d), lambda i, f: (i, 0)),
        out_shape=jax.ShapeDtypeStruct((rows, d), F32),
        scratch_shapes=[pltpu.VMEM((tm, d), BF16)],
        compiler_params=_params("parallel", "arbitrary"),
        name="half_ffn",
    )(*args)


def _inproj_kernel(x_ref, g_ref, sh_ref, sc_ref, w_ref, o_ref, h_scr):
    @pl.when(pl.program_id(1) == 0)
    def _():
        _norm_mod_store(x_ref, g_ref, sh_ref, sc_ref, h_scr)

    o_ref[...] = jnp.dot(h_scr[...], w_ref[...], preferred_element_type=F32)


def in_proj(x, g, shift, scale, w, tm=1024, tn=512):
    rows, d = x.shape
    n = w.shape[1]
    tm = min(tm, rows)
    vec = pl.BlockSpec((1, d), lambda i, j: (0, 0))
    return pl.pallas_call(
        _inproj_kernel,
        grid=(rows // tm, n // tn),
        in_specs=[pl.BlockSpec((tm, d), lambda i, j: (i, 0)), vec, vec, vec,
                  pl.BlockSpec((d, tn), lambda i, j: (0, j))],
        out_specs=pl.BlockSpec((tm, tn), lambda i, j: (i, j)),
        out_shape=jax.ShapeDtypeStruct((rows, n), F32),
        scratch_shapes=[pltpu.VMEM((tm, d), BF16)],
        compiler_params=_params("parallel", "arbitrary"),
        name="in_proj",
    )(x, g, shift, scale, w)


HY_CHANNEL_TILE = 512


def _even_inproj_kernel(x_ref, g_ref, sh_ref, sc_ref, w_ref, zt_ref, qkv_ref, h_scr):
    _norm_mod_store(x_ref, g_ref, sh_ref, sc_ref, h_scr)
    tm = x_ref.shape[0]
    ct = HY_CHANNEL_TILE
    nc = zt_ref.shape[0] * SUBLANES
    for j in range(nc // ct):
        zt = lax.dot_general(w_ref[:, j * ct:(j + 1) * ct], h_scr[...], (((0,), (1,)), ((), ())),
                             preferred_element_type=F32)
        for r in range(tm // LANES):
            zt_ref[j * ct // SUBLANES:(j + 1) * ct // SUBLANES, r * SUBLANES:(r + 1) * SUBLANES, :] = (
                zt[:, r * LANES:(r + 1) * LANES].reshape(ct // SUBLANES, SUBLANES, LANES))
    qkv_ref[...] = jnp.dot(h_scr[...], w_ref[:, nc:], preferred_element_type=F32)


def even_in_proj(x, g, shift, scale, w, nc, tm=512):
    rows, d = x.shape
    n = w.shape[1] - nc
    vec = pl.BlockSpec((1, d), lambda i: (0, 0))
    return pl.pallas_call(
        _even_inproj_kernel,
        grid=(rows // tm,),
        in_specs=[pl.BlockSpec((tm, d), lambda i: (i, 0)), vec, vec, vec,
                  pl.BlockSpec((d, nc + n), lambda i: (0, 0), pipeline_mode=pl.Buffered(1))],
        out_specs=[pl.BlockSpec((nc // SUBLANES, SUBLANES * tm // LANES, LANES), lambda i: (0, i, 0)),
                   pl.BlockSpec((tm, n), lambda i: (i, 0))],
        out_shape=[jax.ShapeDtypeStruct((nc // SUBLANES, SUBLANES * rows // LANES, LANES), F32),
                   jax.ShapeDtypeStruct((rows, n), F32)],
        scratch_shapes=[pltpu.VMEM((tm, d), BF16)],
        compiler_params=_params("parallel"),
        name="even_in_proj",
    )(x, g, shift, scale, w)


def _filter_kernel(z_ref, t_ref, w1_ref, b1_ref, f1_ref, w2_ref, b2_ref, f2_ref, w3_ref, dl_ref, o_ref):
    hi = lax.Precision.HIGHEST
    ct = w3_ref.shape[0]
    for r in range(z_ref.shape[1] // LANES):
        sl = slice(r * LANES, (r + 1) * LANES)
        a1 = jnp.dot(w1_ref[...], z_ref[:, sl], precision=hi, preferred_element_type=F32)
        h1 = jnp.sin(f1_ref[...] * (a1 + b1_ref[...]))
        a2 = jnp.dot(w2_ref[...], h1, precision=hi, preferred_element_type=F32)
        h2 = jnp.sin(f2_ref[...] * (a2 + b2_ref[...]))
        h3 = jnp.dot(w3_ref[...], h2.astype(BF16), preferred_element_type=F32)
        dec = jnp.exp(-(t_ref[0:1, sl] * dl_ref[...]))
        o_ref[:, r * SUBLANES:(r + 1) * SUBLANES, :] = (h3 * dec).reshape(ct // SUBLANES, SUBLANES, LANES)


def hyena_filter_t(L, w1, b1, f1, w2, b2, f2, w3, tl=1024):
    hid = w1.shape[1]
    c2 = w3.shape[1]
    width = c2 // 2
    t = jnp.linspace(0.0, 1.0, L, dtype=F32)[:, None]
    w = (2.0 * math.pi / L) * jnp.arange(L, dtype=F32)[:, None]
    bands = jnp.linspace(1e-4, HY_BANDS - 1, HY_BANDS, dtype=F32)[None, :]
    z = jnp.concatenate([t, jnp.cos(bands * w), -jnp.sin(bands * w)], axis=-1)
    emb = z.shape[1]
    embp = -(-emb // SUBLANES) * SUBLANES
    zt = jnp.pad(z.T, ((0, embp - emb), (0, 0)))
    w1t = jnp.pad(w1.astype(F32).T, ((0, 0), (0, embp - emb)))
    lt = math.log(HY_DECAY_TARGET)
    deltas = jnp.abs(jnp.linspace(lt / HY_SLOW_PCT, lt / HY_FAST_PCT, width, dtype=F32))
    deltas2 = jnp.concatenate([deltas, deltas])
    col = lambda v: jnp.broadcast_to(v.astype(F32)[:, None], (v.shape[0], LANES))
    trow = jnp.broadcast_to(t.T, (SUBLANES, L))
    tl = min(tl, L)
    full = lambda shape: pl.BlockSpec(shape, lambda i: (0,) * len(shape))
    return pl.pallas_call(
        _filter_kernel,
        grid=(L // tl,),
        in_specs=[
            pl.BlockSpec((embp, tl), lambda i: (0, i)),
            pl.BlockSpec((SUBLANES, tl), lambda i: (0, i)),
            full((hid, embp)), full((hid, LANES)), full((hid, LANES)),
            full((hid, hid)), full((hid, LANES)), full((hid, LANES)),
            full((c2, hid)), full((c2, LANES)),
        ],
        out_specs=pl.BlockSpec((c2 // SUBLANES, SUBLANES * tl // LANES, LANES), lambda i: (0, i, 0)),
        out_shape=jax.ShapeDtypeStruct((c2 // SUBLANES, SUBLANES * L // LANES, LANES), F32),
        compiler_params=_params("parallel"),
        name="hyena_filter",
    )(zt, trow, w1t, col(b1), col(f1), w2.astype(F32).T, col(b2), col(f2), w3.T.astype(BF16), col(deltas2))


def _dft_constants(h1):
    n_outer = 2 * h1
    n = n_outer * LANES
    bf16_rows = 2 * SUBLANES
    kp = -(-(h1 + 1) // bf16_rows) * bf16_rows
    k1 = np.arange(kp)[:, None]
    n1 = np.arange(h1)[None, :]
    ang_a = 2.0 * np.pi * ((k1 * n1) % n_outer) / n_outer
    fa = np.concatenate([np.cos(ang_a), -np.sin(ang_a)], axis=0)
    n2 = np.arange(LANES)[None, :]
    ang_t = 2.0 * np.pi * ((k1 * n2) % n) / n
    twr, twi = np.cos(ang_t), -np.sin(ang_t)
    a = np.arange(LANES)
    ang_b = 2.0 * np.pi * ((a[:, None] * a[None, :]) % LANES) / LANES
    cb, sb = np.cos(ang_b), np.sin(ang_b)
    fb = np.block([[cb, -sb], [sb, cb]])
    gb = np.block([[cb, sb], [-sb, cb]])
    wk = np.where((k1 == 0) | (k1 == h1), 1.0, 2.0) * (k1 <= h1)
    ga = np.concatenate([(wk * np.cos(ang_a)).T, (-wk * np.sin(ang_a)).T], axis=1)
    as_bf = lambda m: jnp.asarray(m, dtype=F32).astype(BF16)
    return (as_bf(fa), as_bf(twr), as_bf(twi), as_bf(fb), as_bf(gb), as_bf(ga), kp, n)


def _hyena_kernel(scal_ref, z0_ref, z1_ref, zv_ref, hf_ref, hb_ref, fa_ref, twr_ref, twi_ref, fb_ref, gb_ref,
                  ga_ref, o_ref, *, h1, kp, inv_n):
    grp = pl.program_id(0)
    row = lax.broadcasted_iota(jnp.int32, (h1, LANES), 0)
    lane = lax.broadcasted_iota(jnp.int32, (h1, LANES), 1)
    first = (row == 0) & (lane == 0)
    last = (row == h1 - 1) & (lane == LANES - 1)

    def chan(ref, ci):
        return ref[0, pl.ds(ci, h1, stride=SUBLANES), :]

    def prev(x):
        r = pltpu.roll(x, 1, 1)
        r = jnp.where(lane == 0, pltpu.roll(r, 1, 0), r)
        return jnp.where(first, 0.0, r)

    def nxt(x):
        r = pltpu.roll(x, LANES - 1, 1)
        r = jnp.where(lane == LANES - 1, pltpu.roll(r, h1 - 1, 0), r)
        return jnp.where(last, 0.0, r)

    us, x0s, hfs, hbs, nrm = [], [], [], [], []
    for ci in range(HY_GROUP):
        c = grp * HY_GROUP + ci

        def sconv(ref, s):
            x = chan(ref, ci)
            return (scal_ref[3 * s, c] * prev(x) + scal_ref[3 * s + 1, c] * x
                    + scal_ref[3 * s + 2, c] * nxt(x) + scal_ref[9 + s, c])

        x0 = sconv(z0_ref, 0)
        x1 = sconv(z1_ref, 1)
        hv = sconv(zv_ref, 2)
        us.append(hv * x1)
        x0s.append(x0)
        hf = chan(hf_ref, ci)
        hb = jnp.where(first, 0.0, chan(hb_ref, ci))
        hfs.append(hf)
        hbs.append(hb)
        ssq = jnp.sum(hf * hf, keepdims=True) + jnp.sum(hb * hb, keepdims=True)
        nrm.append(lax.rsqrt(ssq + EPS))

    fa = fa_ref[...]
    twr, twi = twr_ref[...], twi_ref[...]

    def outer_fwd(mats):
        xc = jnp.concatenate([m.astype(BF16) for m in mats], axis=1)
        a = jnp.dot(fa, xc, preferred_element_type=F32).astype(BF16)
        out = []
        for ci in range(HY_GROUP):
            ar = a[:kp, ci * LANES:(ci + 1) * LANES]
            ai = a[kp:, ci * LANES:(ci + 1) * LANES]
            out.append(jnp.concatenate([ar * twr - ai * twi, ar * twi + ai * twr], axis=1))
        return out

    stacked = jnp.concatenate(outer_fwd(us) + outer_fwd(hfs) + outer_fwd(hbs), axis=0)
    spec = jnp.dot(stacked, fb_ref[...], preferred_element_type=F32).astype(BF16)

    prod = []
    for ci in range(HY_GROUP):
        xu = spec[ci * kp:(ci + 1) * kp]
        xf = spec[(HY_GROUP + ci) * kp:(HY_GROUP + ci + 1) * kp]
        xb = spec[(2 * HY_GROUP + ci) * kp:(2 * HY_GROUP + ci + 1) * kp]
        xr, xi = xu[:, :LANES], xu[:, LANES:]
        kr = xf[:, :LANES] + xb[:, :LANES]
        ki = xf[:, LANES:] - xb[:, LANES:]
        prod.append(jnp.concatenate([xr * kr - xi * ki, xr * ki + xi * kr], axis=1))
    inner = jnp.dot(jnp.concatenate(prod, axis=0), gb_ref[...], preferred_element_type=F32).astype(BF16)

    cols = []
    for ci in range(HY_GROUP):
        b = inner[ci * kp:(ci + 1) * kp]
        br, bi = b[:, :LANES], b[:, LANES:]
        cols.append(jnp.concatenate([br * twr + bi * twi, bi * twr - br * twi], axis=0))
    y = jnp.dot(ga_ref[...], jnp.concatenate(cols, axis=1), preferred_element_type=F32)

    for ci in range(HY_GROUP):
        c = grp * HY_GROUP + ci
        yc = y[:, ci * LANES:(ci + 1) * LANES] * (nrm[ci] * inv_n)
        o_ref[0, pl.ds(ci, h1, stride=SUBLANES), :] = x0s[ci] * (yc + us[ci] * scal_ref[12, c])


def hyena_mix(zt, ht, conv_w, conv_b, skip):
    ng3, r8, _ = zt.shape
    ng = ng3 // 3
    h1 = r8 // SUBLANES
    width = ng * SUBLANES
    fa, twr, twi, fb, gb, ga, kp, n = _dft_constants(h1)
    scal = jnp.concatenate([
        conv_w[:, 0:width], conv_w[:, width:2 * width], conv_w[:, 2 * width:3 * width],
        conv_b.reshape(3, width), skip.reshape(1, width)], axis=0).astype(F32)
    blk = lambda off: pl.BlockSpec((1, r8, LANES), lambda g, off=off: (g + off, 0, 0))
    full = lambda a: pl.BlockSpec(a.shape, lambda g: (0,) * a.ndim)
    return pl.pallas_call(
        functools.partial(_hyena_kernel, h1=h1, kp=kp, inv_n=1.0 / n),
        grid=(ng,),
        in_specs=[pl.BlockSpec(memory_space=pltpu.SMEM),
                  blk(0), blk(ng), blk(2 * ng), blk(0), blk(ng),
                  full(fa), full(twr), full(twi), full(fb), full(gb), full(ga)],
        out_specs=pl.BlockSpec((1, r8, LANES), lambda g: (g, 0, 0)),
        out_shape=jax.ShapeDtypeStruct((ng, r8, LANES), F32),
        compiler_params=_params("parallel"),
        name="hyena_mix",
    )(scal, zt, zt, zt, ht, ht, fa, twr, twi, fb, gb, ga)


def _attn_kernel(sink_ref, q_ref, kp_ref, k0_ref, kn_ref, vp_ref, v0_ref, vn_ref, ccp_ref, cc0_ref, ccn_ref,
                 ssp_ref, ss0_ref, ssn_ref, kc_ref, vc_ref, o_ref, *, scale):
    i = pl.program_id(0)
    nblk = pl.num_programs(0)
    hd, blk = HEAD_DIM, ATT_BLOCK
    nt = (((1,), (1,)), ((), ()))

    def rope(x, cc, ss):
        return x * cc + pltpu.roll(x, hd // 2, 1) * ss

    log2e = math.log2(math.e)
    c2 = scale * log2e
    rows = Q_PER_KV * blk
    off = lax.broadcasted_iota(jnp.int32, (rows, blk), 0) & (blk - 1)
    col = lax.broadcasted_iota(jnp.int32, (rows, blk), 1)
    keep_prev = col >= off + jnp.where(i > 0, 0, blk)
    keep_next = col <= off - jnp.where(i < nblk - 1, 0, blk)
    head_of_row = lax.broadcasted_iota(jnp.int32, (rows, 1), 0) // blk
    rowmax = lambda a: jnp.max(a, axis=-1, keepdims=True)
    rowsum = lambda a: jnp.sum(a, axis=-1, keepdims=True)

    for g in range(N_KV_HEADS):
        gs = slice(g * hd, (g + 1) * hd)
        kb = jnp.concatenate([
            rope(kp_ref[:, gs], ccp_ref[...], ssp_ref[...]),
            rope(k0_ref[:, gs], cc0_ref[...], ss0_ref[...]),
            rope(kn_ref[:, gs], ccn_ref[...], ssn_ref[...])], axis=0).astype(BF16)
        vb = jnp.concatenate([vp_ref[:, gs], v0_ref[:, gs], vn_ref[:, gs]], axis=0).astype(BF16)
        kcg = kc_ref[:, gs].astype(BF16)
        vcg = vc_ref[:, gs].astype(BF16)
        q4 = jnp.concatenate([
            rope(q_ref[:, (g * Q_PER_KV + h) * hd:(g * Q_PER_KV + h + 1) * hd], cc0_ref[...], ss0_ref[...])
            for h in range(Q_PER_KV)], axis=0).astype(BF16)
        s_loc = lax.dot_general(q4, kb, nt, preferred_element_type=F32) * c2
        s_ctx = lax.dot_general(q4, kcg, nt, preferred_element_type=F32) * c2
        s_prev = jnp.where(keep_prev, s_loc[:, :blk], NEG_INF)
        s_own = s_loc[:, blk:2 * blk]
        s_next = jnp.where(keep_next, s_loc[:, 2 * blk:], NEG_INF)
        sink = jnp.zeros((rows, 1), F32)
        for h in range(Q_PER_KV):
            sink = jnp.where(head_of_row == h, sink_ref[g * Q_PER_KV + h] * log2e, sink)
        ctx_tiles = [s_ctx[:, j * LANES:(j + 1) * LANES] for j in range(s_ctx.shape[1] // LANES)]
        m = jnp.maximum(rowmax(functools.reduce(jnp.maximum, [s_prev, s_own, s_next] + ctx_tiles)), sink)
        p_prev, p_own, p_next = jnp.exp2(s_prev - m), jnp.exp2(s_own - m), jnp.exp2(s_next - m)
        p_ctx = jnp.exp2(s_ctx - m)
        p_tiles = [p_prev, p_own, p_next] + [p_ctx[:, j * LANES:(j + 1) * LANES] for j in range(len(ctx_tiles))]
        den = jnp.exp2(sink - m) + rowsum(functools.reduce(jnp.add, p_tiles))
        p_loc = jnp.concatenate([p_prev, p_own, p_next], axis=1).astype(BF16)
        o = (jnp.dot(p_ctx.astype(BF16), vcg, preferred_element_type=F32)
             + jnp.dot(p_loc, vb, preferred_element_type=F32)) * (1.0 / den)
        for h in range(Q_PER_KV):
            o_ref[:, (g * Q_PER_KV + h) * hd:(g * Q_PER_KV + h + 1) * hd] = o[h * blk:(h + 1) * blk].astype(o_ref.dtype)


def window_attention(qkv, kvc, sink):
    L = qkv.shape[0]
    n_ctx = kvc.shape[0]
    nb = L // ATT_BLOCK
    kvw = N_KV_HEADS * HEAD_DIM
    qw = N_KV_HEADS * Q_PER_KV * HEAD_DIM
    kcol, vcol = qw // kvw, qw // kvw + 1
    t = jnp.arange(L, dtype=jnp.int32)
    rowp = (t // GRID_W).astype(F32)
    colp = (t % GRID_W).astype(F32)
    nq = HEAD_DIM // 4
    inv = ROPE_BASE ** (-jnp.arange(nq, dtype=F32) / nq)
    ang = jnp.concatenate([rowp[:, None] * inv, colp[:, None] * inv], axis=-1)
    cos, sin = jnp.cos(ang), jnp.sin(ang)
    cc = jnp.concatenate([cos, cos], axis=-1)
    ss = jnp.concatenate([-sin, sin], axis=-1)
    prv = lambda i: jnp.maximum(i - 1, 0)
    nxt = lambda i: jnp.minimum(i + 1, nb - 1)
    cur = lambda i: i
    kspec = lambda f: pl.BlockSpec((ATT_BLOCK, kvw), lambda i, f=f: (f(i), kcol))
    vspec = lambda f: pl.BlockSpec((ATT_BLOCK, kvw), lambda i, f=f: (f(i), vcol))
    tspec = lambda f: pl.BlockSpec((ATT_BLOCK, HEAD_DIM), lambda i, f=f: (f(i), 0))
    return pl.pallas_call(
        functools.partial(_attn_kernel, scale=HEAD_DIM ** -0.5),
        grid=(nb,),
        in_specs=[pl.BlockSpec(memory_space=pltpu.SMEM),
                  pl.BlockSpec((ATT_BLOCK, qw), lambda i: (i, 0)),
                  kspec(prv), kspec(cur), kspec(nxt), vspec(prv), vspec(cur), vspec(nxt),
                  tspec(prv), tspec(cur), tspec(nxt), tspec(prv), tspec(cur), tspec(nxt),
                  pl.BlockSpec((n_ctx, kvw), lambda i: (0, 0)),
                  pl.BlockSpec((n_ctx, kvw), lambda i: (0, 1))],
        out_specs=pl.BlockSpec((ATT_BLOCK, qw), lambda i: (i, 0)),
        out_shape=jax.ShapeDtypeStruct((L, qw), BF16),
        compiler_params=_params("parallel"),
        name="window_attention",
    )(sink.astype(F32), qkv, qkv, qkv, qkv, qkv, qkv, qkv, cc, cc, cc, ss, ss, ss, kvc, kvc)


ODD_OUT_TILE = 512


def _odd_mixer_kernel(x_ref, g_ref, sh_ref, sc_ref, win_ref, sgg_ref, ws_ref, bs_ref, wout_ref, gt_ref,
                      o_ref, h_scr, vn_scr, sg_scr):
    tm, width = sg_scr.shape
    gd = width // SG_GROUPS
    _norm_mod_store(x_ref, g_ref, sh_ref, sc_ref, h_scr)
    v = jax.nn.gelu(jnp.dot(h_scr[...], win_ref[:, width:], preferred_element_type=F32), approximate=True)
    vn_scr[...] = (v * lax.rsqrt(jnp.mean(v * v, axis=-1, keepdims=True) + EPS) * sgg_ref[...]).astype(BF16)
    for g in range(SG_GROUPS):
        cs = slice(g * gd, (g + 1) * gd)
        u = jax.nn.gelu(jnp.dot(h_scr[...], win_ref[:, cs], preferred_element_type=F32), approximate=True)
        bias = jnp.concatenate([bs_ref[g]] * (gd // LANES), axis=1)
        for ch in range(tm // CHUNK):
            rs = slice(ch * CHUNK, (ch + 1) * CHUNK)
            mixed = jnp.dot(ws_ref[g], vn_scr[rs, cs], preferred_element_type=F32) + bias
            sg_scr[rs, cs] = (u[rs] * mixed).astype(BF16)
    for j in range(o_ref.shape[1] // ODD_OUT_TILE):
        cs = slice(j * ODD_OUT_TILE, (j + 1) * ODD_OUT_TILE)
        acc = jnp.dot(sg_scr[...], wout_ref[:, cs], preferred_element_type=F32)
        o_ref[:, cs] = x_ref[:, cs] + gt_ref[:, cs] * acc


def odd_mixer(x, g, shift, scale, gate, w_in, sg_g, ws, bs, w_out, tm=512):
    rows, d = x.shape
    width = w_in.shape[1] // 2
    n = w_out.shape[1]
    bsb = jnp.broadcast_to(bs.astype(F32)[:, :, None], (SG_GROUPS, CHUNK, LANES))
    vec = lambda w: pl.BlockSpec((1, w), lambda i: (0, 0))
    once = pl.Buffered(1)
    return pl.pallas_call(
        _odd_mixer_kernel,
        grid=(rows // tm,),
        in_specs=[pl.BlockSpec((tm, d), lambda i: (i, 0)), vec(d), vec(d), vec(d),
                  pl.BlockSpec((d, 2 * width), lambda i: (0, 0), pipeline_mode=once),
                  vec(width),
                  pl.BlockSpec((SG_GROUPS, CHUNK, CHUNK), lambda i: (0, 0, 0)),
                  pl.BlockSpec((SG_GROUPS, CHUNK, LANES), lambda i: (0, 0, 0)),
                  pl.BlockSpec((width, n), lambda i: (0, 0), pipeline_mode=once),
                  vec(n)],
        out_specs=pl.BlockSpec((tm, n), lambda i: (i, 0)),
        out_shape=jax.ShapeDtypeStruct((rows, n), F32),
        scratch_shapes=[pltpu.VMEM((tm, d), BF16), pltpu.VMEM((tm, width), BF16), pltpu.VMEM((tm, width), BF16)],
        compiler_params=_params("parallel"),
        name="odd_mixer",
    )(x, g, shift, scale, w_in, sg_g, ws.astype(BF16), bsb, w_out, gate)


def _outproj_even_kernel(yt_ref, ya_ref, w_ref, x_ref, gt_ref, o_ref, lhs_scr):
    ng = yt_ref.shape[0]
    c = ng * SUBLANES
    for r in range(yt_ref.shape[1] // SUBLANES):
        sub = yt_ref[:, r * SUBLANES:(r + 1) * SUBLANES, :].reshape(c, LANES)
        lhs_scr[r * LANES:(r + 1) * LANES, 0:c] = sub.T.astype(BF16)
    lhs_scr[:, c:] = ya_ref[...]
    acc = jnp.dot(lhs_scr[...], w_ref[...], preferred_element_type=F32)
    o_ref[...] = x_ref[...] + gt_ref[...] * acc


def out_proj_even(yt, ya, w, x, gate, tm=512):
    rows, aw = ya.shape
    ng = yt.shape[0]
    k, n = w.shape
    return pl.pallas_call(
        _outproj_even_kernel,
        grid=(rows // tm,),
        in_specs=[pl.BlockSpec((ng, SUBLANES * tm // LANES, LANES), lambda i: (0, i, 0)),
                  pl.BlockSpec((tm, aw), lambda i: (i, 0)),
                  pl.BlockSpec((k, n), lambda i: (0, 0), pipeline_mode=pl.Buffered(1)),
                  pl.BlockSpec((tm, n), lambda i: (i, 0)),
                  pl.BlockSpec((1, n), lambda i: (0, 0))],
        out_specs=pl.BlockSpec((tm, n), lambda i: (i, 0)),
        out_shape=jax.ShapeDtypeStruct((rows, n), F32),
        scratch_shapes=[pltpu.VMEM((tm, k), BF16)],
        compiler_params=_params("parallel"),
        name="out_proj_even",
    )(yt, ya, w, x, gate)


def kernel(x, c, ctx, c_ctx, ada_w, ada_b, norm_g, ffn_wg, ffn_wu, ffn_wd, ev_w_in, ev_conv_w, ev_conv_b,
           hy_w1, hy_b1, hy_f1, hy_w2, hy_b2, hy_f2, hy_w3, hy_skip, att_sink, ev_w_out, od_w_in, sg_g, sg_ws,
           sg_bs, od_w_out, final_g):
    assert x.shape[0] == 1 and ada_w.shape[0] == 2, "written for batch 1, depth 2 (even layer then odd layer)"
    _, L, d = x.shape
    hy_width = hy_skip.shape[1]
    hy_in = 3 * hy_width
    kv_w = N_KV_HEADS * HEAD_DIM
    q_end = hy_in + (d - hy_width)

    xs = x[0]
    xc = ctx[0]
    cond8 = jnp.zeros((SUBLANES, d), F32).at[0].set(c[0]).at[1].set(c_ctx)
    mods = ada_mods(cond8, ada_w, ada_b)
    row = lambda v: v.reshape(1, d)
    bf = lambda w: w.astype(BF16)

    mod = mods[0, 0].reshape(N_MOD, 1, d)
    mc = mods[0, 1].reshape(N_MOD, 1, d)
    g = norm_g[0]
    wg, wu, wd = bf(ffn_wg), bf(ffn_wu), bf(ffn_wd)
    xs = half_ffn(xs, row(g[0]), mod[0], mod[1], mod[2], wg, wu, wd, 0, 0)
    xc = half_ffn(xc, row(g[0]), mc[0], mc[1], mc[2], wg, wu, wd, 0, 0)
    w_in = bf(ev_w_in[0])
    zt, qkv = even_in_proj(xs, row(g[1]), mod[3], mod[4], w_in, hy_in)
    kvc = in_proj(xc, row(g[1]), mc[3], mc[4], w_in[:, q_end:])
    ht = hyena_filter_t(L, hy_w1[0], hy_b1[0], hy_f1[0], hy_w2[0], hy_b2[0], hy_f2[0], hy_w3[0])
    yt = hyena_mix(zt, ht, ev_conv_w[0], ev_conv_b[0], hy_skip[0])
    ya = window_attention(qkv, kvc, att_sink[0])
    xs = out_proj_even(yt, ya, bf(ev_w_out[0]), xs, mod[5])
    xs = half_ffn(xs, row(g[2]), mod[6], mod[7], mod[8], wg, wu, wd, 0, 1)

    mod = mods[1, 0].reshape(N_MOD, 1, d)
    g = norm_g[1]
    xs = half_ffn(xs, row(g[0]), mod[0], mod[1], mod[2], wg, wu, wd, 1, 0)
    xs = odd_mixer(xs, row(g[1]), mod[3], mod[4], mod[5], bf(od_w_in[0]), row(sg_g[0]), sg_ws[0], sg_bs[0],
                   bf(od_w_out[0]))
    xs = half_ffn(xs, row(g[2]), mod[6], mod[7], mod[8], wg, wu, wd, 1, 1, final_g=row(final_g))
    return xs[None]
```

```python
import functools
import math

import numpy as np
import jax
import jax.numpy as jnp
from jax import lax
from jax.experimental import pallas as pl
from jax.experimental.pallas import tpu as pltpu

F32 = jnp.float32
BF16 = jnp.bfloat16

LANES = 128
SUBLANES = 8
VMEM_LIMIT_BYTES = 58 * 1024 * 1024

EPS = 1e-6
NEG_INF = -1e30
N_MOD = 9
HEAD_DIM = 128
N_KV_HEADS = 2
Q_PER_KV = 4
ATT_BLOCK = 128
GRID_W = 64
ROPE_BASE = 10000.0
HY_BANDS = 16
HY_DECAY_TARGET = 1e-2
HY_FAST_PCT = 0.3
HY_SLOW_PCT = 1.5
SG_GROUPS = 8
CHUNK = 128
HY_GROUP = SUBLANES


def _params(*sem):
    return pltpu.CompilerParams(dimension_semantics=sem, vmem_limit_bytes=VMEM_LIMIT_BYTES)


NORM_ROWS = 16


def _norm_mod_store(x_ref, g_ref, sh_ref, sc_ref, h_scr):
    g = g_ref[...]
    one_plus_scale = 1.0 + sc_ref[...]
    shift = sh_ref[...]

    def body(i, carry):
        rows = pl.ds(pl.multiple_of(i * NORM_ROWS, NORM_ROWS), NORM_ROWS)
        x = x_ref[rows, :]
        y = x * lax.rsqrt(jnp.mean(x * x, axis=-1, keepdims=True) + EPS) * g
        h_scr[rows, :] = (y * one_plus_scale + shift).astype(BF16)
        return carry

    lax.fori_loop(0, x_ref.shape[0] // NORM_ROWS, body, 0, unroll=8)


def _ada_kernel(s_ref, w_ref, b_ref, o_ref):
    s = s_ref[...]
    s = (s * jax.nn.sigmoid(s)).astype(BF16)
    o_ref[0] = jnp.dot(s, w_ref[0].astype(BF16), preferred_element_type=F32) + b_ref[0]


def ada_mods(cond8, ada_w, ada_b, tn=1024):
    depth, d, n = ada_w.shape
    return pl.pallas_call(
        _ada_kernel,
        grid=(depth, n // tn),
        in_specs=[
            pl.BlockSpec((SUBLANES, d), lambda l, j: (0, 0)),
            pl.BlockSpec((1, d, tn), lambda l, j: (l, 0, j)),
            pl.BlockSpec((1, 1, tn), lambda l, j: (l, 0, j)),
        ],
        out_specs=pl.BlockSpec((1, SUBLANES, tn), lambda l, j: (l, 0, j)),
        out_shape=jax.ShapeDtypeStruct((depth, SUBLANES, n), F32),
        compiler_params=_params("parallel", "parallel"),
        name="ada_mods",
    )(cond8, ada_w, ada_b.reshape(depth, 1, n))


FFN_ROW_CHUNK = 512


def _ffn_kernel(x_ref, g_ref, sh_ref, sc_ref, gt_ref, wg_ref, wu_ref, wd_ref, *rest, final):
    if final:
        fg_ref, o_ref, h_scr = rest
    else:
        o_ref, h_scr = rest
    f = pl.program_id(1)

    @pl.when(f == 0)
    def _():
        _norm_mod_store(x_ref, g_ref, sh_ref, sc_ref, h_scr)
        o_ref[...] = jnp.zeros_like(o_ref)

    tm = x_ref.shape[0]
    rc = min(tm, FFN_ROW_CHUNK)
    wd = wd_ref[...].astype(BF16)
    for r in range(tm // rc):
        rs = slice(r * rc, (r + 1) * rc)
        h = h_scr[rs, :]
        gate_act = jnp.dot(h, wg_ref[...], preferred_element_type=F32)
        up = jnp.dot(h, wu_ref[...], preferred_element_type=F32)
        a = (gate_act * jax.nn.sigmoid(gate_act) * up).astype(BF16)
        o_ref[rs, :] += jnp.dot(a, wd, preferred_element_type=F32)

    @pl.when(f == pl.num_programs(1) - 1)
    def _():
        half_gate = 0.5 * gt_ref[...]
        group = 8

        def body(i, carry):
            chunk = lambda k: pl.ds(pl.multiple_of((i * group + k) * NORM_ROWS, NORM_ROWS), NORM_ROWS)
            resid = lambda rows: x_ref[rows, :] + half_gate * o_ref[rows, :]
            if final:
                inv = [lax.rsqrt(jnp.mean(jnp.square(resid(chunk(k))), axis=-1, keepdims=True) + EPS)
                       for k in range(group)]
                for k in range(group):
                    o_ref[chunk(k), :] = resid(chunk(k)) * inv[k] * fg_ref[...]
            else:
                for k in range(group):
                    o_ref[chunk(k), :] = resid(chunk(k))
            return carry

        lax.fori_loop(0, tm // (NORM_ROWS * group), body, 0)


def half_ffn(x, g, shift, scale, gate, wg, wu, wd, layer, which, final_g=None, tm=1024, tf=512):
    rows, d = x.shape
    dff = wg.shape[-1]
    tm = min(tm, rows)
    vec = pl.BlockSpec((1, d), lambda i, f: (0, 0))
    in_specs = [
        pl.BlockSpec((tm, d), lambda i, f: (i, 0)),
        vec, vec, vec, vec,
        pl.BlockSpec((None, None, d, tf), lambda i, f: (layer, which, 0, f)),
        pl.BlockSpec((None, None, d, tf), lambda i, f: (layer, which, 0, f)),
        pl.BlockSpec((None, None, tf, d), lambda i, f: (layer, which, f, 0)),
    ]
    args = [x, g, shift, scale, gate, wg, wu, wd]
    if final_g is not None:
        in_specs.append(vec)
        args.append(final_g)
    return pl.pallas_call(
        functools.partial(_ffn_kernel, final=final_g is not None),
        grid=(rows // tm, dff // tf),
        in_specs=in_specs,
        out_specs=pl.BlockSpec((tm, d), lambda i, f: (i, 0)),
        out_shape=jax.ShapeDtypeStruct((rows, d), F32),
        scratch_shapes=[pltpu.VMEM((tm, d), BF16)],
        compiler_params=_params("parallel", "arbitrary"),
        name="half_ffn",
    )(*args)


def _inproj_kernel(x_ref, g_ref, sh_ref, sc_ref, w_ref, o_ref, h_scr):
    @pl.when(pl.program_id(1) == 0)
    def _():
        _norm_mod_store(x_ref, g_ref, sh_ref, sc_ref, h_scr)

    o_ref[...] = jnp.dot(h_scr[...], w_ref[...], preferred_element_type=F32)


def in_proj(x, g, shift, scale, w, tm=1024, tn=512):
    rows, d = x.shape
    n = w.shape[1]
    tm = min(tm, rows)
    vec = pl.BlockSpec((1, d), lambda i, j: (0, 0))
    return pl.pallas_call(
        _inproj_kernel,
        grid=(rows // tm, n // tn),
        in_specs=[pl.BlockSpec((tm, d), lambda i, j: (i, 0)), vec, vec, vec,
                  pl.BlockSpec((d, tn), lambda i, j: (0, j))],
        out_specs=pl.BlockSpec((tm, tn), lambda i, j: (i, j)),
        out_shape=jax.ShapeDtypeStruct((rows, n), F32),
        scratch_shapes=[pltpu.VMEM((tm, d), BF16)],
        compiler_params=_params("parallel", "arbitrary"),
        name="in_proj",
    )(x, g, shift, scale, w)


HY_CHANNEL_TILE = 512


def _even_inproj_kernel(x_ref, g_ref, sh_ref, sc_ref, w_ref, zt_ref, qkv_ref, h_scr):
    _norm_mod_store(x_ref, g_ref, sh_ref, sc_ref, h_scr)
    tm = x_ref.shape[0]
    ct = HY_CHANNEL_TILE
    nc = zt_ref.shape[0] * SUBLANES
    for j in range(nc // ct):
        zt = lax.dot_general(w_ref[:, j * ct:(j + 1) * ct], h_scr[...], (((0,), (1,)), ((), ())),
                             preferred_element_type=F32)
        for r in range(tm // LANES):
            zt_ref[j * ct // SUBLANES:(j + 1) * ct // SUBLANES, r * SUBLANES:(r + 1) * SUBLANES, :] = (
                zt[:, r * LANES:(r + 1) * LANES].reshape(ct // SUBLANES, SUBLANES, LANES))
    qkv_ref[...] = jnp.dot(h_scr[...], w_ref[:, nc:], preferred_element_type=F32)


def even_in_proj(x, g, shift, scale, w, nc, tm=512):
    rows, d = x.shape
    n = w.shape[1] - nc
    vec = pl.BlockSpec((1, d), lambda i: (0, 0))
    return pl.pallas_call(
        _even_inproj_kernel,
        grid=(rows // tm,),
        in_specs=[pl.BlockSpec((tm, d), lambda i: (i, 0)), vec, vec, vec,
                  pl.BlockSpec((d, nc + n), lambda i: (0, 0), pipeline_mode=pl.Buffered(1))],
        out_specs=[pl.BlockSpec((nc // SUBLANES, SUBLANES * tm // LANES, LANES), lambda i: (0, i, 0)),
                   pl.BlockSpec((tm, n), lambda i: (i, 0))],
        out_shape=[jax.ShapeDtypeStruct((nc // SUBLANES, SUBLANES * rows // LANES, LANES), F32),
                   jax.ShapeDtypeStruct((rows, n), F32)],
        scratch_shapes=[pltpu.VMEM((tm, d), BF16)],
        compiler_params=_params("parallel"),
        name="even_in_proj",
    )(x, g, shift, scale, w)


def _filter_kernel(z_ref, t_ref, w1_ref, b1_ref, f1_ref, w2_ref, b2_ref, f2_ref, w3_ref, dl_ref, o_ref):
    hi = lax.Precision.HIGHEST
    ct = w3_ref.shape[0]
    for r in range(z_ref.shape[1] // LANES):
        sl = slice(r * LANES, (r + 1) * LANES)
        a1 = jnp.dot(w1_ref[...], z_ref[:, sl], precision=hi, preferred_element_type=F32)
        h1 = jnp.sin(f1_ref[...] * (a1 + b1_ref[...]))
        a2 = jnp.dot(w2_ref[...], h1, precision=hi, preferred_element_type=F32)
        h2 = jnp.sin(f2_ref[...] * (a2 + b2_ref[...]))
        h3 = jnp.dot(w3_ref[...], h2.astype(BF16), preferred_element_type=F32)
        dec = jnp.exp(-(t_ref[0:1, sl] * dl_ref[...]))
        o_ref[:, r * SUBLANES:(r + 1) * SUBLANES, :] = (h3 * dec).reshape(ct // SUBLANES, SUBLANES, LANES)


def hyena_filter_t(L, w1, b1, f1, w2, b2, f2, w3, tl=1024):
    hid = w1.shape[1]
    c2 = w3.shape[1]
    width = c2 // 2
    t = jnp.linspace(0.0, 1.0, L, dtype=F32)[:, None]
    w = (2.0 * math.pi / L) * jnp.arange(L, dtype=F32)[:, None]
    bands = jnp.linspace(1e-4, HY_BANDS - 1, HY_BANDS, dtype=F32)[None, :]
    z = jnp.concatenate([t, jnp.cos(bands * w), -jnp.sin(bands * w)], axis=-1)
    emb = z.shape[1]
    embp = -(-emb // SUBLANES) * SUBLANES
    zt = jnp.pad(z.T, ((0, embp - emb), (0, 0)))
    w1t = jnp.pad(w1.astype(F32).T, ((0, 0), (0, embp - emb)))
    lt = math.log(HY_DECAY_TARGET)
    deltas = jnp.abs(jnp.linspace(lt / HY_SLOW_PCT, lt / HY_FAST_PCT, width, dtype=F32))
    deltas2 = jnp.concatenate([deltas, deltas])
    col = lambda v: jnp.broadcast_to(v.astype(F32)[:, None], (v.shape[0], LANES))
    trow = jnp.broadcast_to(t.T, (SUBLANES, L))
    tl = min(tl, L)
    full = lambda shape: pl.BlockSpec(shape, lambda i: (0,) * len(shape))
    return pl.pallas_call(
        _filter_kernel,
        grid=(L // tl,),
        in_specs=[
            pl.BlockSpec((embp, tl), lambda i: (0, i)),
            pl.BlockSpec((SUBLANES, tl), lambda i: (0, i)),
            full((hid, embp)), full((hid, LANES)), full((hid, LANES)),
            full((hid, hid)), full((hid, LANES)), full((hid, LANES)),
            full((c2, hid)), full((c2, LANES)),
        ],
        out_specs=pl.BlockSpec((c2 // SUBLANES, SUBLANES * tl // LANES, LANES), lambda i: (0, i, 0)),
        out_shape=jax.ShapeDtypeStruct((c2 // SUBLANES, SUBLANES * L // LANES, LANES), F32),
        compiler_params=_params("parallel"),
        name="hyena_filter",
    )(zt, trow, w1t, col(b1), col(f1), w2.astype(F32).T, col(b2), col(f2), w3.T.astype(BF16), col(deltas2))


def _dft_constants(h1):
    n_outer = 2 * h1
    n = n_outer * LANES
    bf16_rows = 2 * SUBLANES
    kp = -(-(h1 + 1) // bf16_rows) * bf16_rows
    k1 = np.arange(kp)[:, None]
    n1 = np.arange(h1)[None, :]
    ang_a = 2.0 * np.pi * ((k1 * n1) % n_outer) / n_outer
    fa = np.concatenate([np.cos(ang_a), -np.sin(ang_a)], axis=0)
    n2 = np.arange(LANES)[None, :]
    ang_t = 2.0 * np.pi * ((k1 * n2) % n) / n
    twr, twi = np.cos(ang_t), -np.sin(ang_t)
    a = np.arange(LANES)
    ang_b = 2.0 * np.pi * ((a[:, None] * a[None, :]) % LANES) / LANES
    cb, sb = np.cos(ang_b), np.sin(ang_b)
    fb = np.block([[cb, -sb], [sb, cb]])
    gb = np.block([[cb, sb], [-sb, cb]])
    wk = np.where((k1 == 0) | (k1 == h1), 1.0, 2.0) * (k1 <= h1)
    ga = np.concatenate([(wk * np.cos(ang_a)).T, (-wk * np.sin(ang_a)).T], axis=1)
    as_bf = lambda m: jnp.asarray(m, dtype=F32).astype(BF16)
    return (as_bf(fa), as_bf(twr), as_bf(twi), as_bf(fb), as_bf(gb), as_bf(ga), kp, n)


def _hyena_kernel(scal_ref, z0_ref, z1_ref, zv_ref, hf_ref, hb_ref, fa_ref, twr_ref, twi_ref, fb_ref, gb_ref,
                  ga_ref, o_ref, *, h1, kp, inv_n):
    grp = pl.program_id(0)
    row = lax.broadcasted_iota(jnp.int32, (h1, LANES), 0)
    lane = lax.broadcasted_iota(jnp.int32, (h1, LANES), 1)
    first = (row == 0) & (lane == 0)
    last = (row == h1 - 1) & (lane == LANES - 1)

    def chan(ref, ci):
        return ref[0, pl.ds(ci, h1, stride=SUBLANES), :]

    def prev(x):
        r = pltpu.roll(x, 1, 1)
        r = jnp.where(lane == 0, pltpu.roll(r, 1, 0), r)
        return jnp.where(first, 0.0, r)

    def nxt(x):
        r = pltpu.roll(x, LANES - 1, 1)
        r = jnp.where(lane == LANES - 1, pltpu.roll(r, h1 - 1, 0), r)
        return jnp.where(last, 0.0, r)

    us, x0s, hfs, hbs, nrm = [], [], [], [], []
    for ci in range(HY_GROUP):
        c = grp * HY_GROUP + ci

        def sconv(ref, s):
            x = chan(ref, ci)
            return (scal_ref[3 * s, c] * prev(x) + scal_ref[3 * s + 1, c] * x
                    + scal_ref[3 * s + 2, c] * nxt(x) + scal_ref[9 + s, c])

        x0 = sconv(z0_ref, 0)
        x1 = sconv(z1_ref, 1)
        hv = sconv(zv_ref, 2)
        us.append(hv * x1)
        x0s.append(x0)
        hf = chan(hf_ref, ci)
        hb = jnp.where(first, 0.0, chan(hb_ref, ci))
        hfs.append(hf)
        hbs.append(hb)
        ssq = jnp.sum(hf * hf, keepdims=True) + jnp.sum(hb * hb, keepdims=True)
        nrm.append(lax.rsqrt(ssq + EPS))

    fa = fa_ref[...]
    twr, twi = twr_ref[...], twi_ref[...]

    def outer_fwd(mats):
        xc = jnp.concatenate([m.astype(BF16) for m in mats], axis=1)
        a = jnp.dot(fa, xc, preferred_element_type=F32).astype(BF16)
        out = []
        for ci in range(HY_GROUP):
            ar = a[:kp, ci * LANES:(ci + 1) * LANES]
            ai = a[kp:, ci * LANES:(ci + 1) * LANES]
            out.append(jnp.concatenate([ar * twr - ai * twi, ar * twi + ai * twr], axis=1))
        return out

    stacked = jnp.concatenate(outer_fwd(us) + outer_fwd(hfs) + outer_fwd(hbs), axis=0)
    spec = jnp.dot(stacked, fb_ref[...], preferred_element_type=F32).astype(BF16)

    prod = []
    for ci in range(HY_GROUP):
        xu = spec[ci * kp:(ci + 1) * kp]
        xf = spec[(HY_GROUP + ci) * kp:(HY_GROUP + ci + 1) * kp]
        xb = spec[(2 * HY_GROUP + ci) * kp:(2 * HY_GROUP + ci + 1) * kp]
        xr, xi = xu[:, :LANES], xu[:, LANES:]
        kr = xf[:, :LANES] + xb[:, :LANES]
        ki = xf[:, LANES:] - xb[:, LANES:]
        prod.append(jnp.concatenate([xr * kr - xi * ki, xr * ki + xi * kr], axis=1))
    inner = jnp.dot(jnp.concatenate(prod, axis=0), gb_ref[...], preferred_element_type=F32).astype(BF16)

    cols = []
    for ci in range(HY_GROUP):
        b = inner[ci * kp:(ci + 1) * kp]
        br, bi = b[:, :LANES], b[:, LANES:]
        cols.append(jnp.concatenate([br * twr + bi * twi, bi * twr - br * twi], axis=0))
    y = jnp.dot(ga_ref[...], jnp.concatenate(cols, axis=1), preferred_element_type=F32)

    for ci in range(HY_GROUP):
        c = grp * HY_GROUP + ci
        yc = y[:, ci * LANES:(ci + 1) * LANES] * (nrm[ci] * inv_n)
        o_ref[0, pl.ds(ci, h1, stride=SUBLANES), :] = x0s[ci] * (yc + us[ci] * scal_ref[12, c])


def hyena_mix(zt, ht, conv_w, conv_b, skip):
    ng3, r8, _ = zt.shape
    ng = ng3 // 3
    h1 = r8 // SUBLANES
    width = ng * SUBLANES
    fa, twr, twi, fb, gb, ga, kp, n = _dft_constants(h1)
    scal = jnp.concatenate([
        conv_w[:, 0:width], conv_w[:, width:2 * width], conv_w[:, 2 * width:3 * width],
        conv_b.reshape(3, width), skip.reshape(1, width)], axis=0).astype(F32)
    blk = lambda off: pl.BlockSpec((1, r8, LANES), lambda g, off=off: (g + off, 0, 0))
    full = lambda a: pl.BlockSpec(a.shape, lambda g: (0,) * a.ndim)
    return pl.pallas_call(
        functools.partial(_hyena_kernel, h1=h1, kp=kp, inv_n=1.0 / n),
        grid=(ng,),
        in_specs=[pl.BlockSpec(memory_space=pltpu.SMEM),
                  blk(0), blk(ng), blk(2 * ng), blk(0), blk(ng),
                  full(fa), full(twr), full(twi), full(fb), full(gb), full(ga)],
        out_specs=pl.BlockSpec((1, r8, LANES), lambda g: (g, 0, 0)),
        out_shape=jax.ShapeDtypeStruct((ng, r8, LANES), F32),
        compiler_params=_params("parallel"),
        name="hyena_mix",
    )(scal, zt, zt, zt, ht, ht, fa, twr, twi, fb, gb, ga)


ATT_Q_BLOCKS = 2


def _attn_kernel(sink_ref, q_ref, *refs, scale):
    nkb = ATT_Q_BLOCKS + 2
    k_refs, v_refs = refs[0:nkb], refs[nkb:2 * nkb]
    cc_refs, ss_refs = refs[2 * nkb:3 * nkb], refs[3 * nkb:4 * nkb]
    kc_ref, vc_ref, o_ref = refs[4 * nkb:]
    step = pl.program_id(0)
    nsteps = pl.num_programs(0)
    hd, blk = HEAD_DIM, ATT_BLOCK
    nt = (((1,), (1,)), ((), ()))

    def rope(x, cc, ss):
        return x * cc + pltpu.roll(x, hd // 2, 1) * ss

    log2e = math.log2(math.e)
    c2 = scale * log2e
    rows = Q_PER_KV * blk
    off = lax.broadcasted_iota(jnp.int32, (rows, blk), 0) & (blk - 1)
    col = lax.broadcasted_iota(jnp.int32, (rows, blk), 1)
    head_of_row = lax.broadcasted_iota(jnp.int32, (rows, 1), 0) // blk
    rowmax = lambda a: jnp.max(a, axis=-1, keepdims=True)
    rowsum = lambda a: jnp.sum(a, axis=-1, keepdims=True)

    for g in range(N_KV_HEADS):
        gs = slice(g * hd, (g + 1) * hd)
        rk = [rope(k_refs[b][:, gs], cc_refs[b][...], ss_refs[b][...]).astype(BF16) for b in range(nkb)]
        vv = [v_refs[b][:, gs].astype(BF16) for b in range(nkb)]
        kcg = kc_ref[:, gs].astype(BF16)
        vcg = vc_ref[:, gs].astype(BF16)
        sink = jnp.zeros((rows, 1), F32)
        for h in range(Q_PER_KV):
            sink = jnp.where(head_of_row == h, sink_ref[g * Q_PER_KV + h] * log2e, sink)
        for sb in range(ATT_Q_BLOCKS):
            qrows = slice(sb * blk, (sb + 1) * blk)
            kb = jnp.concatenate(rk[sb:sb + 3], axis=0)
            vb = jnp.concatenate(vv[sb:sb + 3], axis=0)
            q4 = jnp.concatenate([
                rope(q_ref[qrows, (g * Q_PER_KV + h) * hd:(g * Q_PER_KV + h + 1) * hd],
                     cc_refs[sb + 1][...], ss_refs[sb + 1][...])
                for h in range(Q_PER_KV)], axis=0).astype(BF16)
            keep_prev = col >= (off + jnp.where(step > 0, 0, blk) if sb == 0 else off)
            keep_next = col <= (off - jnp.where(step < nsteps - 1, 0, blk) if sb == ATT_Q_BLOCKS - 1 else off)
            s_loc = lax.dot_general(q4, kb, nt, preferred_element_type=F32) * c2
            s_ctx = lax.dot_general(q4, kcg, nt, preferred_element_type=F32) * c2
            s_prev = jnp.where(keep_prev, s_loc[:, :blk], NEG_INF)
            s_own = s_loc[:, blk:2 * blk]
            s_next = jnp.where(keep_next, s_loc[:, 2 * blk:], NEG_INF)
            ctx_tiles = [s_ctx[:, j * LANES:(j + 1) * LANES] for j in range(s_ctx.shape[1] // LANES)]
            m = jnp.maximum(rowmax(functools.reduce(jnp.maximum, [s_prev, s_own, s_next] + ctx_tiles)), sink)
            p_prev, p_own, p_next = jnp.exp2(s_prev - m), jnp.exp2(s_own - m), jnp.exp2(s_next - m)
            p_ctx = jnp.exp2(s_ctx - m)
            p_tiles = [p_prev, p_own, p_next] + [p_ctx[:, j * LANES:(j + 1) * LANES] for j in range(len(ctx_tiles))]
            den = jnp.exp2(sink - m) + rowsum(functools.reduce(jnp.add, p_tiles))
            p_loc = jnp.concatenate([p_prev, p_own, p_next], axis=1).astype(BF16)
            o = (jnp.dot(p_ctx.astype(BF16), vcg, preferred_element_type=F32)
                 + jnp.dot(p_loc, vb, preferred_element_type=F32)) * (1.0 / den)
            for h in range(Q_PER_KV):
                o_ref[qrows, (g * Q_PER_KV + h) * hd:(g * Q_PER_KV + h + 1) * hd] = (
                    o[h * blk:(h + 1) * blk].astype(o_ref.dtype))


def window_attention(qkv, kvc, sink):
    L = qkv.shape[0]
    n_ctx = kvc.shape[0]
    nb = L // ATT_BLOCK
    kvw = N_KV_HEADS * HEAD_DIM
    qw = N_KV_HEADS * Q_PER_KV * HEAD_DIM
    kcol, vcol = qw // kvw, qw // kvw + 1
    t = jnp.arange(L, dtype=jnp.int32)
    rowp = (t // GRID_W).astype(F32)
    colp = (t % GRID_W).astype(F32)
    nq = HEAD_DIM // 4
    inv = ROPE_BASE ** (-jnp.arange(nq, dtype=F32) / nq)
    ang = jnp.concatenate([rowp[:, None] * inv, colp[:, None] * inv], axis=-1)
    cos, sin = jnp.cos(ang), jnp.sin(ang)
    cc = jnp.concatenate([cos, cos], axis=-1)
    ss = jnp.concatenate([-sin, sin], axis=-1)
    nkb = ATT_Q_BLOCKS + 2
    kblock = lambda b: (lambda i: jnp.clip(i * ATT_Q_BLOCKS - 1 + b, 0, nb - 1))
    kspecs = [pl.BlockSpec((ATT_BLOCK, kvw), lambda i, f=kblock(b): (f(i), kcol)) for b in range(nkb)]
    vspecs = [pl.BlockSpec((ATT_BLOCK, kvw), lambda i, f=kblock(b): (f(i), vcol)) for b in range(nkb)]
    tspecs = [pl.BlockSpec((ATT_BLOCK, HEAD_DIM), lambda i, f=kblock(b): (f(i), 0)) for b in range(nkb)]
    tq = ATT_Q_BLOCKS * ATT_BLOCK
    return pl.pallas_call(
        functools.partial(_attn_kernel, scale=HEAD_DIM ** -0.5),
        grid=(L // tq,),
        in_specs=[pl.BlockSpec(memory_space=pltpu.SMEM),
                  pl.BlockSpec((tq, qw), lambda i: (i, 0))]
                 + kspecs + vspecs + tspecs + tspecs
                 + [pl.BlockSpec((n_ctx, kvw), lambda i: (0, 0)),
                    pl.BlockSpec((n_ctx, kvw), lambda i: (0, 1))],
        out_specs=pl.BlockSpec((tq, qw), lambda i: (i, 0)),
        out_shape=jax.ShapeDtypeStruct((L, qw), BF16),
        compiler_params=_params("parallel"),
        name="window_attention",
    )(sink.astype(F32), qkv, *([qkv] * (2 * nkb)), *([cc] * nkb), *([ss] * nkb), kvc, kvc)


ODD_OUT_TILE = 512


def _odd_mixer_kernel(x_ref, g_ref, sh_ref, sc_ref, win_ref, sgg_ref, ws_ref, bs_ref, wout_ref, gt_ref,
                      o_ref, h_scr, vn_scr, sg_scr):
    tm, width = sg_scr.shape
    gd = width // SG_GROUPS
    _norm_mod_store(x_ref, g_ref, sh_ref, sc_ref, h_scr)
    v = jax.nn.gelu(jnp.dot(h_scr[...], win_ref[:, width:], preferred_element_type=F32), approximate=True)
    vn_scr[...] = (v * lax.rsqrt(jnp.mean(v * v, axis=-1, keepdims=True) + EPS) * sgg_ref[...]).astype(BF16)
    for g in range(SG_GROUPS):
        cs = slice(g * gd, (g + 1) * gd)
        u = jax.nn.gelu(jnp.dot(h_scr[...], win_ref[:, cs], preferred_element_type=F32), approximate=True)
        bias = jnp.concatenate([bs_ref[g]] * (gd // LANES), axis=1)
        for ch in range(tm // CHUNK):
            rs = slice(ch * CHUNK, (ch + 1) * CHUNK)
            mixed = jnp.dot(ws_ref[g], vn_scr[rs, cs], preferred_element_type=F32) + bias
            sg_scr[rs, cs] = (u[rs] * mixed).astype(BF16)
    for j in range(o_ref.shape[1] // ODD_OUT_TILE):
        cs = slice(j * ODD_OUT_TILE, (j + 1) * ODD_OUT_TILE)
        acc = jnp.dot(sg_scr[...], wout_ref[:, cs], preferred_element_type=F32)
        o_ref[:, cs] = x_ref[:, cs] + gt_ref[:, cs] * acc


def odd_mixer(x, g, shift, scale, gate, w_in, sg_g, ws, bs, w_out, tm=512):
    rows, d = x.shape
    width = w_in.shape[1] // 2
    n = w_out.shape[1]
    bsb = jnp.broadcast_to(bs.astype(F32)[:, :, None], (SG_GROUPS, CHUNK, LANES))
    vec = lambda w: pl.BlockSpec((1, w), lambda i: (0, 0))
    once = pl.Buffered(1)
    return pl.pallas_call(
        _odd_mixer_kernel,
        grid=(rows // tm,),
        in_specs=[pl.BlockSpec((tm, d), lambda i: (i, 0)), vec(d), vec(d), vec(d),
                  pl.BlockSpec((d, 2 * width), lambda i: (0, 0), pipeline_mode=once),
                  vec(width),
                  pl.BlockSpec((SG_GROUPS, CHUNK, CHUNK), lambda i: (0, 0, 0)),
                  pl.BlockSpec((SG_GROUPS, CHUNK, LANES), lambda i: (0, 0, 0)),
                  pl.BlockSpec((width, n), lambda i: (0, 0), pipeline_mode=once),
                  vec(n)],
        out_specs=pl.BlockSpec((tm, n), lambda i: (i, 0)),
        out_shape=jax.ShapeDtypeStruct((rows, n), F32),
        scratch_shapes=[pltpu.VMEM((tm, d), BF16), pltpu.VMEM((tm, width), BF16), pltpu.VMEM((tm, width), BF16)],
        compiler_params=_params("parallel"),
        name="odd_mixer",
    )(x, g, shift, scale, w_in, sg_g, ws.astype(BF16), bsb, w_out, gate)


def _outproj_even_kernel(yt_ref, ya_ref, w_ref, x_ref, gt_ref, o_ref, lhs_scr):
    ng = yt_ref.shape[0]
    c = ng * SUBLANES
    for r in range(yt_ref.shape[1] // SUBLANES):
        sub = yt_ref[:, r * SUBLANES:(r + 1) * SUBLANES, :].reshape(c, LANES)
        lhs_scr[r * LANES:(r + 1) * LANES, 0:c] = sub.T.astype(BF16)
    lhs_scr[:, c:] = ya_ref[...]
    acc = jnp.dot(lhs_scr[...], w_ref[...], preferred_element_type=F32)
    o_ref[...] = x_ref[...] + gt_ref[...] * acc


def out_proj_even(yt, ya, w, x, gate, tm=512):
    rows, aw = ya.shape
    ng = yt.shape[0]
    k, n = w.shape
    return pl.pallas_call(
        _outproj_even_kernel,
        grid=(rows // tm,),
        in_specs=[pl.BlockSpec((ng, SUBLANES * tm // LANES, LANES), lambda i: (0, i, 0)),
                  pl.BlockSpec((tm, aw), lambda i: (i, 0)),
                  pl.BlockSpec((k, n), lambda i: (0, 0), pipeline_mode=pl.Buffered(1)),
                  pl.BlockSpec((tm, n), lambda i: (i, 0)),
                  pl.BlockSpec((1, n), lambda i: (0, 0))],
        out_specs=pl.BlockSpec((tm, n), lambda i: (i, 0)),
        out_shape=jax.ShapeDtypeStruct((rows, n), F32),
        scratch_shapes=[pltpu.VMEM((tm, k), BF16)],
        compiler_params=_params("parallel"),
        name="out_proj_even",
    )(yt, ya, w, x, gate)


def kernel(x, c, ctx, c_ctx, ada_w, ada_b, norm_g, ffn_wg, ffn_wu, ffn_wd, ev_w_in, ev_conv_w, ev_conv_b,
           hy_w1, hy_b1, hy_f1, hy_w2, hy_b2, hy_f2, hy_w3, hy_skip, att_sink, ev_w_out, od_w_in, sg_g, sg_ws,
           sg_bs, od_w_out, final_g):
    assert x.shape[0] == 1 and ada_w.shape[0] == 2, "written for batch 1, depth 2 (even layer then odd layer)"
    _, L, d = x.shape
    hy_width = hy_skip.shape[1]
    hy_in = 3 * hy_width
    kv_w = N_KV_HEADS * HEAD_DIM
    q_end = hy_in + (d - hy_width)

    xs = x[0]
    xc = ctx[0]
    cond8 = jnp.zeros((SUBLANES, d), F32).at[0].set(c[0]).at[1].set(c_ctx)
    mods = ada_mods(cond8, ada_w, ada_b)
    row = lambda v: v.reshape(1, d)
    bf = lambda w: w.astype(BF16)

    mod = mods[0, 0].reshape(N_MOD, 1, d)
    mc = mods[0, 1].reshape(N_MOD, 1, d)
    g = norm_g[0]
    wg, wu, wd = bf(ffn_wg), bf(ffn_wu), ffn_wd
    xs = half_ffn(xs, row(g[0]), mod[0], mod[1], mod[2], wg, wu, wd, 0, 0)
    xc = half_ffn(xc, row(g[0]), mc[0], mc[1], mc[2], wg, wu, wd, 0, 0)
    w_in = bf(ev_w_in[0])
    zt, qkv = even_in_proj(xs, row(g[1]), mod[3], mod[4], w_in, hy_in)
    kvc = in_proj(xc, row(g[1]), mc[3], mc[4], w_in[:, q_end:])
    ht = hyena_filter_t(L, hy_w1[0], hy_b1[0], hy_f1[0], hy_w2[0], hy_b2[0], hy_f2[0], hy_w3[0])
    yt = hyena_mix(zt, ht, ev_conv_w[0], ev_conv_b[0], hy_skip[0])
    ya = window_attention(qkv, kvc, att_sink[0])
    xs = out_proj_even(yt, ya, bf(ev_w_out[0]), xs, mod[5])
    xs = half_ffn(xs, row(g[2]), mod[6], mod[7], mod[8], wg, wu, wd, 0, 1)

    mod = mods[1, 0].reshape(N_MOD, 1, d)
    g = norm_g[1]
    xs = half_ffn(xs, row(g[0]), mod[0], mod[1], mod[2], wg, wu, wd, 1, 0)
    xs = odd_mixer(xs, row(g[1]), mod[3], mod[4], mod[5], bf(od_w_in[0]), row(sg_g[0]), sg_ws[0], sg_bs[0],
                   bf(od_w_out[0]))
    xs = half_ffn(xs, row(g[2]), mod[6], mod[7], mod[8], wg, wu, wd, 1, 1, final_g=row(final_g))
    return xs[None]
```

```python
import functools
import math

import numpy as np
import jax
import jax.numpy as jnp
from jax import lax
from jax.experimental import pallas as pl
from jax.experimental.pallas import tpu as pltpu

F32 = jnp.float32
BF16 = jnp.bfloat16

LANES = 128
SUBLANES = 8
VMEM_LIMIT_BYTES = 58 * 1024 * 1024

EPS = 1e-6
NEG_INF = -1e30
N_MOD = 9
HEAD_DIM = 128
N_KV_HEADS = 2
Q_PER_KV = 4
ATT_BLOCK = 128
GRID_W = 64
ROPE_BASE = 10000.0
HY_BANDS = 16
HY_DECAY_TARGET = 1e-2
HY_FAST_PCT = 0.3
HY_SLOW_PCT = 1.5
SG_GROUPS = 8
CHUNK = 128
HY_GROUP = 2 * SUBLANES


def _params(*sem):
    return pltpu.CompilerParams(dimension_semantics=sem, vmem_limit_bytes=VMEM_LIMIT_BYTES)


NORM_ROWS = 16


def _norm_mod_store(x_ref, g_ref, sh_ref, sc_ref, h_scr):
    g = g_ref[...]
    one_plus_scale = 1.0 + sc_ref[...]
    shift = sh_ref[...]

    def body(i, carry):
        rows = pl.ds(pl.multiple_of(i * NORM_ROWS, NORM_ROWS), NORM_ROWS)
        x = x_ref[rows, :]
        y = x * lax.rsqrt(jnp.mean(x * x, axis=-1, keepdims=True) + EPS) * g
        h_scr[rows, :] = (y * one_plus_scale + shift).astype(BF16)
        return carry

    lax.fori_loop(0, x_ref.shape[0] // NORM_ROWS, body, 0, unroll=8)


def _ada_kernel(s_ref, w_ref, b_ref, o_ref):
    s = s_ref[...]
    s = (s * jax.nn.sigmoid(s)).astype(BF16)
    o_ref[0] = jnp.dot(s, w_ref[0].astype(BF16), preferred_element_type=F32) + b_ref[0]


def ada_mods(cond8, ada_w, ada_b, tn=1024):
    depth, d, n = ada_w.shape
    return pl.pallas_call(
        _ada_kernel,
        grid=(depth, n // tn),
        in_specs=[
            pl.BlockSpec((SUBLANES, d), lambda l, j: (0, 0)),
            pl.BlockSpec((1, d, tn), lambda l, j: (l, 0, j)),
            pl.BlockSpec((1, 1, tn), lambda l, j: (l, 0, j)),
        ],
        out_specs=pl.BlockSpec((1, SUBLANES, tn), lambda l, j: (l, 0, j)),
        out_shape=jax.ShapeDtypeStruct((depth, SUBLANES, n), F32),
        compiler_params=_params("parallel", "parallel"),
        name="ada_mods",
    )(cond8, ada_w, ada_b.reshape(depth, 1, n))


FFN_ROW_CHUNK = 512


def _ffn_kernel(x_ref, g_ref, sh_ref, sc_ref, gt_ref, wg_ref, wu_ref, wd_ref, *rest, final):
    if final:
        fg_ref, o_ref, h_scr = rest
    else:
        o_ref, h_scr = rest
    f = pl.program_id(1)

    @pl.when(f == 0)
    def _():
        _norm_mod_store(x_ref, g_ref, sh_ref, sc_ref, h_scr)
        o_ref[...] = jnp.zeros_like(o_ref)

    tm = x_ref.shape[0]
    rc = min(tm, FFN_ROW_CHUNK)
    wd = wd_ref[...].astype(BF16)
    for r in range(tm // rc):
        rs = slice(r * rc, (r + 1) * rc)
        h = h_scr[rs, :]
        gate_act = jnp.dot(h, wg_ref[...], preferred_element_type=F32)
        up = jnp.dot(h, wu_ref[...], preferred_element_type=F32)
        a = (gate_act * jax.nn.sigmoid(gate_act) * up).astype(BF16)
        o_ref[rs, :] += jnp.dot(a, wd, preferred_element_type=F32)

    @pl.when(f == pl.num_programs(1) - 1)
    def _():
        half_gate = 0.5 * gt_ref[...]
        group = 8

        def body(i, carry):
            chunk = lambda k: pl.ds(pl.multiple_of((i * group + k) * NORM_ROWS, NORM_ROWS), NORM_ROWS)
            resid = lambda rows: x_ref[rows, :] + half_gate * o_ref[rows, :]
            if final:
                inv = [lax.rsqrt(jnp.mean(jnp.square(resid(chunk(k))), axis=-1, keepdims=True) + EPS)
                       for k in range(group)]
                for k in range(group):
                    o_ref[chunk(k), :] = resid(chunk(k)) * inv[k] * fg_ref[...]
            else:
                for k in range(group):
                    o_ref[chunk(k), :] = resid(chunk(k))
            return carry

        lax.fori_loop(0, tm // (NORM_ROWS * group), body, 0)


def half_ffn(x, g, shift, scale, gate, wg, wu, wd, layer, which, final_g=None, tm=1024, tf=512):
    rows, d = x.shape
    dff = wg.shape[-1]
    tm = min(tm, rows)
    vec = pl.BlockSpec((1, d), lambda i, f: (0, 0))
    in_specs = [
        pl.BlockSpec((tm, d), lambda i, f: (i, 0)),
        vec, vec, vec, vec,
        pl.BlockSpec((None, None, d, tf), lambda i, f: (layer, which, 0, f)),
        pl.BlockSpec((None, None, d, tf), lambda i, f: (layer, which, 0, f)),
        pl.BlockSpec((None, None, tf, d), lambda i, f: (layer, which, f, 0)),
    ]
    args = [x, g, shift, scale, gate, wg, wu, wd]
    if final_g is not None:
        in_specs.append(vec)
        args.append(final_g)
    return pl.pallas_call(
        functools.partial(_ffn_kernel, final=final_g is not None),
        grid=(rows // tm, dff // tf),
        in_specs=in_specs,
        out_specs=pl.BlockSpec((tm, d), lambda i, f: (i, 0)),
        out_shape=jax.ShapeDtypeStruct((rows, d), F32),
        scratch_shapes=[pltpu.VMEM((tm, d), BF16)],
        compiler_params=_params("parallel", "arbitrary"),
        name="half_ffn",
    )(*args)


def _inproj_kernel(x_ref, g_ref, sh_ref, sc_ref, w_ref, o_ref, h_scr):
    @pl.when(pl.program_id(1) == 0)
    def _():
        _norm_mod_store(x_ref, g_ref, sh_ref, sc_ref, h_scr)

    o_ref[...] = jnp.dot(h_scr[...], w_ref[...], preferred_element_type=F32)


def in_proj(x, g, shift, scale, w, tm=1024, tn=512):
    rows, d = x.shape
    n = w.shape[1]
    tm = min(tm, rows)
    vec = pl.BlockSpec((1, d), lambda i, j: (0, 0))
    return pl.pallas_call(
        _inproj_kernel,
        grid=(rows // tm, n // tn),
        in_specs=[pl.BlockSpec((tm, d), lambda i, j: (i, 0)), vec, vec, vec,
                  pl.BlockSpec((d, tn), lambda i, j: (0, j))],
        out_specs=pl.BlockSpec((tm, tn), lambda i, j: (i, j)),
        out_shape=jax.ShapeDtypeStruct((rows, n), F32),
        scratch_shapes=[pltpu.VMEM((tm, d), BF16)],
        compiler_params=_params("parallel", "arbitrary"),
        name="in_proj",
    )(x, g, shift, scale, w)


HY_CHANNEL_TILE = 512


def _even_inproj_kernel(x_ref, g_ref, sh_ref, sc_ref, w_ref, zt_ref, qkv_ref, h_scr):
    _norm_mod_store(x_ref, g_ref, sh_ref, sc_ref, h_scr)
    tm = x_ref.shape[0]
    ct = HY_CHANNEL_TILE
    nc = zt_ref.shape[0] * SUBLANES
    for j in range(nc // ct):
        zt = lax.dot_general(w_ref[:, j * ct:(j + 1) * ct], h_scr[...], (((0,), (1,)), ((), ())),
                             preferred_element_type=F32)
        for r in range(tm // LANES):
            zt_ref[j * ct // SUBLANES:(j + 1) * ct // SUBLANES, r * SUBLANES:(r + 1) * SUBLANES, :] = (
                zt[:, r * LANES:(r + 1) * LANES].reshape(ct // SUBLANES, SUBLANES, LANES))
    qkv_ref[...] = jnp.dot(h_scr[...], w_ref[:, nc:], preferred_element_type=F32)


def even_in_proj(x, g, shift, scale, w, nc, tm=512):
    rows, d = x.shape
    n = w.shape[1] - nc
    vec = pl.BlockSpec((1, d), lambda i: (0, 0))
    return pl.pallas_call(
        _even_inproj_kernel,
        grid=(rows // tm,),
        in_specs=[pl.BlockSpec((tm, d), lambda i: (i, 0)), vec, vec, vec,
                  pl.BlockSpec((d, nc + n), lambda i: (0, 0), pipeline_mode=pl.Buffered(1))],
        out_specs=[pl.BlockSpec((nc // SUBLANES, SUBLANES * tm // LANES, LANES), lambda i: (0, i, 0)),
                   pl.BlockSpec((tm, n), lambda i: (i, 0))],
        out_shape=[jax.ShapeDtypeStruct((nc // SUBLANES, SUBLANES * rows // LANES, LANES), F32),
                   jax.ShapeDtypeStruct((rows, n), F32)],
        scratch_shapes=[pltpu.VMEM((tm, d), BF16)],
        compiler_params=_params("parallel"),
        name="even_in_proj",
    )(x, g, shift, scale, w)


def _filter_kernel(z_ref, t_ref, w1_ref, b1_ref, f1_ref, w2_ref, b2_ref, f2_ref, w3_ref, dl_ref, o_ref):
    hi = lax.Precision.HIGHEST
    ct = w3_ref.shape[0]
    for r in range(z_ref.shape[1] // LANES):
        sl = slice(r * LANES, (r + 1) * LANES)
        a1 = jnp.dot(w1_ref[...], z_ref[:, sl], precision=hi, preferred_element_type=F32)
        h1 = jnp.sin(f1_ref[...] * (a1 + b1_ref[...]))
        a2 = jnp.dot(w2_ref[...], h1, precision=hi, preferred_element_type=F32)
        h2 = jnp.sin(f2_ref[...] * (a2 + b2_ref[...]))
        h3 = jnp.dot(w3_ref[...], h2.astype(BF16), preferred_element_type=F32)
        dec = jnp.exp(-(t_ref[0:1, sl] * dl_ref[...]))
        o_ref[:, r * SUBLANES:(r + 1) * SUBLANES, :] = (h3 * dec).reshape(ct // SUBLANES, SUBLANES, LANES)


def hyena_filter_t(L, w1, b1, f1, w2, b2, f2, w3, tl=1024):
    hid = w1.shape[1]
    c2 = w3.shape[1]
    width = c2 // 2
    t = np.linspace(0.0, 1.0, L, dtype=np.float32)[:, None]
    w = np.float32(2.0 * math.pi / L) * np.arange(L, dtype=np.float32)[:, None]
    bands = np.linspace(1e-4, HY_BANDS - 1, HY_BANDS, dtype=np.float32)[None, :]
    z = np.concatenate([t, np.cos(bands * w), -np.sin(bands * w)], axis=-1)
    emb = z.shape[1]
    embp = -(-emb // SUBLANES) * SUBLANES
    zt = jnp.asarray(np.pad(z.T, ((0, embp - emb), (0, 0))), F32)
    w1t = jnp.pad(w1.astype(F32).T, ((0, 0), (0, embp - emb)))
    lt = math.log(HY_DECAY_TARGET)
    deltas = np.abs(np.linspace(lt / HY_SLOW_PCT, lt / HY_FAST_PCT, width, dtype=np.float32))
    deltas2 = jnp.asarray(np.concatenate([deltas, deltas]), F32)
    col = lambda v: jnp.broadcast_to(v.astype(F32)[:, None], (v.shape[0], LANES))
    trow = jnp.asarray(np.broadcast_to(t.T, (SUBLANES, L)), F32)
    tl = min(tl, L)
    full = lambda shape: pl.BlockSpec(shape, lambda i: (0,) * len(shape))
    return pl.pallas_call(
        _filter_kernel,
        grid=(L // tl,),
        in_specs=[
            pl.BlockSpec((embp, tl), lambda i: (0, i)),
            pl.BlockSpec((SUBLANES, tl), lambda i: (0, i)),
            full((hid, embp)), full((hid, LANES)), full((hid, LANES)),
            full((hid, hid)), full((hid, LANES)), full((hid, LANES)),
            full((c2, hid)), full((c2, LANES)),
        ],
        out_specs=pl.BlockSpec((c2 // SUBLANES, SUBLANES * tl // LANES, LANES), lambda i: (0, i, 0)),
        out_shape=jax.ShapeDtypeStruct((c2 // SUBLANES, SUBLANES * L // LANES, LANES), F32),
        compiler_params=_params("parallel"),
        name="hyena_filter",
    )(zt, trow, w1t, col(b1), col(f1), w2.astype(F32).T, col(b2), col(f2), w3.T.astype(BF16), col(deltas2))


def _dft_constants(h1):
    n_outer = 2 * h1
    n = n_outer * LANES
    bf16_rows = 2 * SUBLANES
    kp = -(-(h1 + 1) // bf16_rows) * bf16_rows
    k1 = np.arange(kp)[:, None]
    n1 = np.arange(h1)[None, :]
    ang_a = 2.0 * np.pi * ((k1 * n1) % n_outer) / n_outer
    fa = np.concatenate([np.cos(ang_a), -np.sin(ang_a)], axis=0)
    n2 = np.arange(LANES)[None, :]
    ang_t = 2.0 * np.pi * ((k1 * n2) % n) / n
    twr, twi = np.cos(ang_t), -np.sin(ang_t)
    a = np.arange(LANES)
    ang_b = 2.0 * np.pi * ((a[:, None] * a[None, :]) % LANES) / LANES
    cb, sb = np.cos(ang_b), np.sin(ang_b)
    fb = np.block([[cb, -sb], [sb, cb]])
    gb = np.block([[cb, sb], [-sb, cb]])
    wk = np.where((k1 == 0) | (k1 == h1), 1.0, 2.0) * (k1 <= h1)
    ga = np.concatenate([(wk * np.cos(ang_a)).T, (-wk * np.sin(ang_a)).T], axis=1)
    as_bf = lambda m: jnp.asarray(m, dtype=F32).astype(BF16)
    return (as_bf(fa), as_bf(twr), as_bf(twi), as_bf(fb), as_bf(gb), as_bf(ga), kp, n)


def _hyena_kernel(scal_ref, z0_ref, z1_ref, zv_ref, hf_ref, hb_ref, fa_ref, twr_ref, twi_ref, fb_ref, gb_ref,
                  ga_ref, o_ref, *, h1, kp, inv_n):
    grp = pl.program_id(0)
    row = lax.broadcasted_iota(jnp.int32, (h1, LANES), 0)
    lane = lax.broadcasted_iota(jnp.int32, (h1, LANES), 1)
    first = (row == 0) & (lane == 0)
    last = (row == h1 - 1) & (lane == LANES - 1)

    def chan(ref, ci):
        return ref[ci // SUBLANES, pl.ds(ci % SUBLANES, h1, stride=SUBLANES), :]

    def prev(x):
        r = pltpu.roll(x, 1, 1)
        r = jnp.where(lane == 0, pltpu.roll(r, 1, 0), r)
        return jnp.where(first, 0.0, r)

    def nxt(x):
        r = pltpu.roll(x, LANES - 1, 1)
        r = jnp.where(lane == LANES - 1, pltpu.roll(r, h1 - 1, 0), r)
        return jnp.where(last, 0.0, r)

    us, x0s, hfs, hbs, nrm = [], [], [], [], []
    for ci in range(HY_GROUP):
        c = grp * HY_GROUP + ci

        def sconv(ref, s):
            x = chan(ref, ci)
            return (scal_ref[3 * s, c] * prev(x) + scal_ref[3 * s + 1, c] * x
                    + scal_ref[3 * s + 2, c] * nxt(x) + scal_ref[9 + s, c])

        x0 = sconv(z0_ref, 0)
        x1 = sconv(z1_ref, 1)
        hv = sconv(zv_ref, 2)
        us.append(hv * x1)
        x0s.append(x0)
        hf = chan(hf_ref, ci)
        hb = jnp.where(first, 0.0, chan(hb_ref, ci))
        hfs.append(hf)
        hbs.append(hb)
        ssq = jnp.sum(hf * hf, keepdims=True) + jnp.sum(hb * hb, keepdims=True)
        nrm.append(lax.rsqrt(ssq + EPS))

    fa = fa_ref[...]
    twr, twi = twr_ref[...], twi_ref[...]

    def outer_fwd(mats):
        xc = jnp.concatenate([m.astype(BF16) for m in mats], axis=1)
        a = jnp.dot(fa, xc, preferred_element_type=F32).astype(BF16)
        out = []
        for ci in range(HY_GROUP):
            ar = a[:kp, ci * LANES:(ci + 1) * LANES]
            ai = a[kp:, ci * LANES:(ci + 1) * LANES]
            out.append(jnp.concatenate([ar * twr - ai * twi, ar * twi + ai * twr], axis=1))
        return out

    stacked = jnp.concatenate(outer_fwd(us) + outer_fwd(hfs) + outer_fwd(hbs), axis=0)
    spec = jnp.dot(stacked, fb_ref[...], preferred_element_type=F32).astype(BF16)

    prod = []
    for ci in range(HY_GROUP):
        xu = spec[ci * kp:(ci + 1) * kp]
        xf = spec[(HY_GROUP + ci) * kp:(HY_GROUP + ci + 1) * kp]
        xb = spec[(2 * HY_GROUP + ci) * kp:(2 * HY_GROUP + ci + 1) * kp]
        xr, xi = xu[:, :LANES], xu[:, LANES:]
        kr = xf[:, :LANES] + xb[:, :LANES]
        ki = xf[:, LANES:] - xb[:, LANES:]
        prod.append(jnp.concatenate([xr * kr - xi * ki, xr * ki + xi * kr], axis=1))
    inner = jnp.dot(jnp.concatenate(prod, axis=0), gb_ref[...], preferred_element_type=F32).astype(BF16)

    cols = []
    for ci in range(HY_GROUP):
        b = inner[ci * kp:(ci + 1) * kp]
        br, bi = b[:, :LANES], b[:, LANES:]
        cols.append(jnp.concatenate([br * twr + bi * twi, bi * twr - br * twi], axis=0))
    y = jnp.dot(ga_ref[...], jnp.concatenate(cols, axis=1), preferred_element_type=F32)

    for ci in range(HY_GROUP):
        c = grp * HY_GROUP + ci
        yc = y[:, ci * LANES:(ci + 1) * LANES] * (nrm[ci] * inv_n)
        o_ref[ci // SUBLANES, pl.ds(ci % SUBLANES, h1, stride=SUBLANES), :] = (
            x0s[ci] * (yc + us[ci] * scal_ref[12, c]))


def hyena_mix(zt, ht, conv_w, conv_b, skip):
    ng3, r8, _ = zt.shape
    ng = ng3 // 3
    h1 = r8 // SUBLANES
    width = ng * SUBLANES
    fa, twr, twi, fb, gb, ga, kp, n = _dft_constants(h1)
    scal = jnp.concatenate([
        conv_w[:, 0:width], conv_w[:, width:2 * width], conv_w[:, 2 * width:3 * width],
        conv_b.reshape(3, width), skip.reshape(1, width)], axis=0).astype(F32)
    gps = HY_GROUP // SUBLANES
    steps = ng // gps
    blk = lambda stream: pl.BlockSpec((gps, r8, LANES), lambda g, stream=stream: (g + stream * steps, 0, 0))
    full = lambda a: pl.BlockSpec(a.shape, lambda g: (0,) * a.ndim)
    return pl.pallas_call(
        functools.partial(_hyena_kernel, h1=h1, kp=kp, inv_n=1.0 / n),
        grid=(steps,),
        in_specs=[pl.BlockSpec(memory_space=pltpu.SMEM),
                  blk(0), blk(1), blk(2), blk(0), blk(1),
                  full(fa), full(twr), full(twi), full(fb), full(gb), full(ga)],
        out_specs=pl.BlockSpec((gps, r8, LANES), lambda g: (g, 0, 0)),
        out_shape=jax.ShapeDtypeStruct((ng, r8, LANES), F32),
        compiler_params=_params("parallel"),
        name="hyena_mix",
    )(scal, zt, zt, zt, ht, ht, fa, twr, twi, fb, gb, ga)


ATT_Q_BLOCKS = 4


def _attn_kernel(sink_ref, q_ref, *refs, scale):
    nkb = ATT_Q_BLOCKS + 2
    k_refs, v_refs = refs[0:nkb], refs[nkb:2 * nkb]
    cc_refs, ss_refs = refs[2 * nkb:3 * nkb], refs[3 * nkb:4 * nkb]
    kc_ref, vc_ref, o_ref = refs[4 * nkb:]
    step = pl.program_id(0)
    nsteps = pl.num_programs(0)
    hd, blk = HEAD_DIM, ATT_BLOCK
    nt = (((1,), (1,)), ((), ()))

    def rope(x, cc, ss):
        return x * cc + pltpu.roll(x, hd // 2, 1) * ss

    log2e = math.log2(math.e)
    c2 = scale * log2e
    rows = Q_PER_KV * blk
    off = lax.broadcasted_iota(jnp.int32, (rows, blk), 0) & (blk - 1)
    col = lax.broadcasted_iota(jnp.int32, (rows, blk), 1)
    head_of_row = lax.broadcasted_iota(jnp.int32, (rows, 1), 0) // blk
    rowmax = lambda a: jnp.max(a, axis=-1, keepdims=True)
    rowsum = lambda a: jnp.sum(a, axis=-1, keepdims=True)

    for g in range(N_KV_HEADS):
        gs = slice(g * hd, (g + 1) * hd)
        rk = [rope(k_refs[b][:, gs], cc_refs[b][...], ss_refs[b][...]).astype(BF16) for b in range(nkb)]
        vv = [v_refs[b][:, gs].astype(BF16) for b in range(nkb)]
        kcg = kc_ref[:, gs].astype(BF16)
        vcg = vc_ref[:, gs].astype(BF16)
        sink = jnp.zeros((rows, 1), F32)
        for h in range(Q_PER_KV):
            sink = jnp.where(head_of_row == h, sink_ref[g * Q_PER_KV + h] * log2e, sink)
        for sb in range(ATT_Q_BLOCKS):
            qrows = slice(sb * blk, (sb + 1) * blk)
            kb = jnp.concatenate(rk[sb:sb + 3], axis=0)
            vb = jnp.concatenate(vv[sb:sb + 3], axis=0)
            q4 = jnp.concatenate([
                rope(q_ref[qrows, (g * Q_PER_KV + h) * hd:(g * Q_PER_KV + h + 1) * hd],
                     cc_refs[sb + 1][...], ss_refs[sb + 1][...]) * c2
                for h in range(Q_PER_KV)], axis=0).astype(BF16)
            keep_prev = col >= (off + jnp.where(step > 0, 0, blk) if sb == 0 else off)
            keep_next = col <= (off - jnp.where(step < nsteps - 1, 0, blk) if sb == ATT_Q_BLOCKS - 1 else off)
            s_loc = lax.dot_general(q4, kb, nt, preferred_element_type=F32)
            s_ctx = lax.dot_general(q4, kcg, nt, preferred_element_type=F32)
            s_prev = jnp.where(keep_prev, s_loc[:, :blk], NEG_INF)
            s_own = s_loc[:, blk:2 * blk]
            s_next = jnp.where(keep_next, s_loc[:, 2 * blk:], NEG_INF)
            ctx_tiles = [s_ctx[:, j * LANES:(j + 1) * LANES] for j in range(s_ctx.shape[1] // LANES)]
            m = jnp.maximum(rowmax(functools.reduce(jnp.maximum, [s_prev, s_own, s_next] + ctx_tiles)), sink)
            p_prev, p_own, p_next = jnp.exp2(s_prev - m), jnp.exp2(s_own - m), jnp.exp2(s_next - m)
            p_ctx = jnp.exp2(s_ctx - m)
            p_tiles = [p_prev, p_own, p_next] + [p_ctx[:, j * LANES:(j + 1) * LANES] for j in range(len(ctx_tiles))]
            den = jnp.exp2(sink - m) + rowsum(functools.reduce(jnp.add, p_tiles))
            p_loc = jnp.concatenate([p_prev, p_own, p_next], axis=1).astype(BF16)
            o = (jnp.dot(p_ctx.astype(BF16), vcg, preferred_element_type=F32)
                 + jnp.dot(p_loc, vb, preferred_element_type=F32)) * (1.0 / den)
            for h in range(Q_PER_KV):
                o_ref[qrows, (g * Q_PER_KV + h) * hd:(g * Q_PER_KV + h + 1) * hd] = (
                    o[h * blk:(h + 1) * blk].astype(o_ref.dtype))


def window_attention(qkv, kvc, sink):
    L = qkv.shape[0]
    n_ctx = kvc.shape[0]
    nb = L // ATT_BLOCK
    kvw = N_KV_HEADS * HEAD_DIM
    qw = N_KV_HEADS * Q_PER_KV * HEAD_DIM
    kcol, vcol = qw // kvw, qw // kvw + 1
    t = np.arange(L)
    rowp = (t // GRID_W).astype(np.float32)
    colp = (t % GRID_W).astype(np.float32)
    nq = HEAD_DIM // 4
    inv = (ROPE_BASE ** (-np.arange(nq, dtype=np.float32) / nq)).astype(np.float32)
    ang = np.concatenate([rowp[:, None] * inv, colp[:, None] * inv], axis=-1)
    cos, sin = np.cos(ang), np.sin(ang)
    cc = jnp.asarray(np.concatenate([cos, cos], axis=-1), F32)
    ss = jnp.asarray(np.concatenate([-sin, sin], axis=-1), F32)
    nkb = ATT_Q_BLOCKS + 2
    kblock = lambda b: (lambda i: jnp.clip(i * ATT_Q_BLOCKS - 1 + b, 0, nb - 1))
    kspecs = [pl.BlockSpec((ATT_BLOCK, kvw), lambda i, f=kblock(b): (f(i), kcol)) for b in range(nkb)]
    vspecs = [pl.BlockSpec((ATT_BLOCK, kvw), lambda i, f=kblock(b): (f(i), vcol)) for b in range(nkb)]
    tspecs = [pl.BlockSpec((ATT_BLOCK, HEAD_DIM), lambda i, f=kblock(b): (f(i), 0)) for b in range(nkb)]
    tq = ATT_Q_BLOCKS * ATT_BLOCK
    return pl.pallas_call(
        functools.partial(_attn_kernel, scale=HEAD_DIM ** -0.5),
        grid=(L // tq,),
        in_specs=[pl.BlockSpec(memory_space=pltpu.SMEM),
                  pl.BlockSpec((tq, qw), lambda i: (i, 0))]
                 + kspecs + vspecs + tspecs + tspecs
                 + [pl.BlockSpec((n_ctx, kvw), lambda i: (0, 0)),
                    pl.BlockSpec((n_ctx, kvw), lambda i: (0, 1))],
        out_specs=pl.BlockSpec((tq, qw), lambda i: (i, 0)),
        out_shape=jax.ShapeDtypeStruct((L, qw), BF16),
        compiler_params=_params("parallel"),
        name="window_attention",
    )(sink.astype(F32), qkv, *([qkv] * (2 * nkb)), *([cc] * nkb), *([ss] * nkb), kvc, kvc)


ODD_OUT_TILE = 512


def _odd_mixer_kernel(x_ref, g_ref, sh_ref, sc_ref, win_ref, sgg_ref, ws_ref, bs_ref, wout_ref, gt_ref,
                      o_ref, h_scr, vn_scr, sg_scr):
    tm, width = sg_scr.shape
    gd = width // SG_GROUPS
    _norm_mod_store(x_ref, g_ref, sh_ref, sc_ref, h_scr)
    v = jax.nn.gelu(jnp.dot(h_scr[...], win_ref[:, width:], preferred_element_type=F32), approximate=True)
    vn_scr[...] = (v * lax.rsqrt(jnp.mean(v * v, axis=-1, keepdims=True) + EPS) * sgg_ref[...]).astype(BF16)
    for g in range(SG_GROUPS):
        cs = slice(g * gd, (g + 1) * gd)
        u = jax.nn.gelu(jnp.dot(h_scr[...], win_ref[:, cs], preferred_element_type=F32), approximate=True)
        bias = jnp.concatenate([bs_ref[g]] * (gd // LANES), axis=1)
        for ch in range(tm // CHUNK):
            rs = slice(ch * CHUNK, (ch + 1) * CHUNK)
            mixed = jnp.dot(ws_ref[g], vn_scr[rs, cs], preferred_element_type=F32) + bias
            sg_scr[rs, cs] = (u[rs] * mixed).astype(BF16)
    for j in range(o_ref.shape[1] // ODD_OUT_TILE):
        cs = slice(j * ODD_OUT_TILE, (j + 1) * ODD_OUT_TILE)
        acc = jnp.dot(sg_scr[...], wout_ref[:, cs], preferred_element_type=F32)
        o_ref[:, cs] = x_ref[:, cs] + gt_ref[:, cs] * acc


def odd_mixer(x, g, shift, scale, gate, w_in, sg_g, ws, bs, w_out, tm=512):
    rows, d = x.shape
    width = w_in.shape[1] // 2
    n = w_out.shape[1]
    bsb = jnp.broadcast_to(bs.astype(F32)[:, :, None], (SG_GROUPS, CHUNK, LANES))
    vec = lambda w: pl.BlockSpec((1, w), lambda i: (0, 0))
    once = pl.Buffered(1)
    return pl.pallas_call(
        _odd_mixer_kernel,
        grid=(rows // tm,),
        in_specs=[pl.BlockSpec((tm, d), lambda i: (i, 0)), vec(d), vec(d), vec(d),
                  pl.BlockSpec((d, 2 * width), lambda i: (0, 0), pipeline_mode=once),
                  vec(width),
                  pl.BlockSpec((SG_GROUPS, CHUNK, CHUNK), lambda i: (0, 0, 0)),
                  pl.BlockSpec((SG_GROUPS, CHUNK, LANES), lambda i: (0, 0, 0)),
                  pl.BlockSpec((width, n), lambda i: (0, 0), pipeline_mode=once),
                  vec(n)],
        out_specs=pl.BlockSpec((tm, n), lambda i: (i, 0)),
        out_shape=jax.ShapeDtypeStruct((rows, n), F32),
        scratch_shapes=[pltpu.VMEM((tm, d), BF16), pltpu.VMEM((tm, width), BF16), pltpu.VMEM((tm, width), BF16)],
        compiler_params=_params("parallel"),
        name="odd_mixer",
    )(x, g, shift, scale, w_in, sg_g, ws.astype(BF16), bsb, w_out, gate)


def _outproj_even_kernel(yt_ref, ya_ref, w_ref, x_ref, gt_ref, o_ref, lhs_scr):
    ng = yt_ref.shape[0]
    c = ng * SUBLANES
    for r in range(yt_ref.shape[1] // SUBLANES):
        sub = yt_ref[:, r * SUBLANES:(r + 1) * SUBLANES, :].reshape(c, LANES)
        lhs_scr[r * LANES:(r + 1) * LANES, 0:c] = sub.T.astype(BF16)
    lhs_scr[:, c:] = ya_ref[...]
    acc = jnp.dot(lhs_scr[...], w_ref[...], preferred_element_type=F32)
    o_ref[...] = x_ref[...] + gt_ref[...] * acc


def out_proj_even(yt, ya, w, x, gate, tm=512):
    rows, aw = ya.shape
    ng = yt.shape[0]
    k, n = w.shape
    return pl.pallas_call(
        _outproj_even_kernel,
        grid=(rows // tm,),
        in_specs=[pl.BlockSpec((ng, SUBLANES * tm // LANES, LANES), lambda i: (0, i, 0)),
                  pl.BlockSpec((tm, aw), lambda i: (i, 0)),
                  pl.BlockSpec((k, n), lambda i: (0, 0), pipeline_mode=pl.Buffered(1)),
                  pl.BlockSpec((tm, n), lambda i: (i, 0)),
                  pl.BlockSpec((1, n), lambda i: (0, 0))],
        out_specs=pl.BlockSpec((tm, n), lambda i: (i, 0)),
        out_shape=jax.ShapeDtypeStruct((rows, n), F32),
        scratch_shapes=[pltpu.VMEM((tm, k), BF16)],
        compiler_params=_params("parallel"),
        name="out_proj_even",
    )(yt, ya, w, x, gate)


def kernel(x, c, ctx, c_ctx, ada_w, ada_b, norm_g, ffn_wg, ffn_wu, ffn_wd, ev_w_in, ev_conv_w, ev_conv_b,
           hy_w1, hy_b1, hy_f1, hy_w2, hy_b2, hy_f2, hy_w3, hy_skip, att_sink, ev_w_out, od_w_in, sg_g, sg_ws,
           sg_bs, od_w_out, final_g):
    assert x.shape[0] == 1 and ada_w.shape[0] == 2, "written for batch 1, depth 2 (even layer then odd layer)"
    _, L, d = x.shape
    hy_width = hy_skip.shape[1]
    hy_in = 3 * hy_width
    kv_w = N_KV_HEADS * HEAD_DIM
    q_end = hy_in + (d - hy_width)

    xs = x[0]
    xc = ctx[0]
    cond8 = jnp.zeros((SUBLANES, d), F32).at[0].set(c[0]).at[1].set(c_ctx)
    mods = ada_mods(cond8, ada_w, ada_b)
    row = lambda v: v.reshape(1, d)
    bf = lambda w: w.astype(BF16)

    mod = mods[0, 0].reshape(N_MOD, 1, d)
    mc = mods[0, 1].reshape(N_MOD, 1, d)
    g = norm_g[0]
    wg, wu, wd = bf(ffn_wg), bf(ffn_wu), ffn_wd
    xs = half_ffn(xs, row(g[0]), mod[0], mod[1], mod[2], wg, wu, wd, 0, 0)
    xc = half_ffn(xc, row(g[0]), mc[0], mc[1], mc[2], wg, wu, wd, 0, 0)
    w_in = bf(ev_w_in[0])
    zt, qkv = even_in_proj(xs, row(g[1]), mod[3], mod[4], w_in, hy_in)
    kvc = in_proj(xc, row(g[1]), mc[3], mc[4], w_in[:, q_end:])
    ht = hyena_filter_t(L, hy_w1[0], hy_b1[0], hy_f1[0], hy_w2[0], hy_b2[0], hy_f2[0], hy_w3[0])
    yt = hyena_mix(zt, ht, ev_conv_w[0], ev_conv_b[0], hy_skip[0])
    ya = window_attention(qkv, kvc, att_sink[0])
    xs = out_proj_even(yt, ya, bf(ev_w_out[0]), xs, mod[5])
    xs = half_ffn(xs, row(g[2]), mod[6], mod[7], mod[8], wg, wu, wd, 0, 1)

    mod = mods[1, 0].reshape(N_MOD, 1, d)
    g = norm_g[1]
    xs = half_ffn(xs, row(g[0]), mod[0], mod[1], mod[2], wg, wu, wd, 1, 0)
    xs = odd_mixer(xs, row(g[1]), mod[3], mod[4], mod[5], bf(od_w_in[0]), row(sg_g[0]), sg_ws[0], sg_bs[0],
                   bf(od_w_out[0]))
    xs = half_ffn(xs, row(g[2]), mod[6], mod[7], mod[8], wg, wu, wd, 1, 1, final_g=row(final_g))
    return xs[None]
```

```python
import functools
import math

import numpy as np
import jax
import jax.numpy as jnp
from jax import lax
from jax.experimental import pallas as pl
from jax.experimental.pallas import tpu as pltpu

F32 = jnp.float32
BF16 = jnp.bfloat16

LANES = 128
SUBLANES = 8
VMEM_LIMIT_BYTES = 58 * 1024 * 1024

EPS = 1e-6
NEG_INF = -1e30
N_MOD = 9
HEAD_DIM = 128
N_KV_HEADS = 2
Q_PER_KV = 4
ATT_BLOCK = 128
GRID_W = 64
ROPE_BASE = 10000.0
HY_BANDS = 16
HY_DECAY_TARGET = 1e-2
HY_FAST_PCT = 0.3
HY_SLOW_PCT = 1.5
SG_GROUPS = 8
CHUNK = 128
HY_GROUP = 2 * SUBLANES


def _params(*sem):
    return pltpu.CompilerParams(dimension_semantics=sem, vmem_limit_bytes=VMEM_LIMIT_BYTES)


NORM_ROWS = 16


def _norm_mod_store(x_ref, g_ref, sh_ref, sc_ref, h_scr, zero_ref=None):
    g = g_ref[...]
    one_plus_scale = 1.0 + sc_ref[...]
    shift = sh_ref[...]

    def body(i, carry):
        rows = pl.ds(pl.multiple_of(i * NORM_ROWS, NORM_ROWS), NORM_ROWS)
        x = x_ref[rows, :]
        y = x * lax.rsqrt(jnp.mean(x * x, axis=-1, keepdims=True) + EPS) * g
        h_scr[rows, :] = (y * one_plus_scale + shift).astype(BF16)
        if zero_ref is not None:
            zero_ref[rows, :] = jnp.zeros((NORM_ROWS, zero_ref.shape[1]), zero_ref.dtype)
        return carry

    lax.fori_loop(0, x_ref.shape[0] // NORM_ROWS, body, 0, unroll=8)


def _ada_kernel(s_ref, w_ref, b_ref, o_ref):
    s = s_ref[...]
    s = (s * jax.nn.sigmoid(s)).astype(BF16)
    o_ref[0] = jnp.dot(s, w_ref[0].astype(BF16), preferred_element_type=F32) + b_ref[0]


def ada_mods(cond8, ada_w, ada_b, tn=1024):
    depth, d, n = ada_w.shape
    return pl.pallas_call(
        _ada_kernel,
        grid=(depth, n // tn),
        in_specs=[
            pl.BlockSpec((SUBLANES, d), lambda l, j: (0, 0)),
            pl.BlockSpec((1, d, tn), lambda l, j: (l, 0, j)),
            pl.BlockSpec((1, 1, tn), lambda l, j: (l, 0, j)),
        ],
        out_specs=pl.BlockSpec((1, SUBLANES, tn), lambda l, j: (l, 0, j)),
        out_shape=jax.ShapeDtypeStruct((depth, SUBLANES, n), F32),
        compiler_params=_params("parallel", "parallel"),
        name="ada_mods",
    )(cond8, ada_w, ada_b.reshape(depth, 1, n))


FFN_ROW_CHUNK = 512


def _ffn_kernel(x_ref, g_ref, sh_ref, sc_ref, gt_ref, wg_ref, wu_ref, wd_ref, *rest, final):
    if final:
        fg_ref, o_ref, h_scr = rest
    else:
        o_ref, h_scr = rest
    f = pl.program_id(1)

    @pl.when(f == 0)
    def _():
        _norm_mod_store(x_ref, g_ref, sh_ref, sc_ref, h_scr, zero_ref=o_ref)

    tm = x_ref.shape[0]
    rc = min(tm, FFN_ROW_CHUNK)
    wd = wd_ref[...].astype(BF16)
    for r in range(tm // rc):
        rs = slice(r * rc, (r + 1) * rc)
        h = h_scr[rs, :]
        gate_act = jnp.dot(h, wg_ref[...], preferred_element_type=F32)
        up = jnp.dot(h, wu_ref[...], preferred_element_type=F32)
        a = (gate_act * jax.nn.sigmoid(gate_act) * up).astype(BF16)
        o_ref[rs, :] += jnp.dot(a, wd, preferred_element_type=F32)

    @pl.when(f == pl.num_programs(1) - 1)
    def _():
        half_gate = 0.5 * gt_ref[...]
        group = 8

        def body(i, carry):
            chunk = lambda k: pl.ds(pl.multiple_of((i * group + k) * NORM_ROWS, NORM_ROWS), NORM_ROWS)
            resid = lambda rows: x_ref[rows, :] + half_gate * o_ref[rows, :]
            if final:
                inv = [lax.rsqrt(jnp.mean(jnp.square(resid(chunk(k))), axis=-1, keepdims=True) + EPS)
                       for k in range(group)]
                for k in range(group):
                    o_ref[chunk(k), :] = resid(chunk(k)) * inv[k] * fg_ref[...]
            else:
                for k in range(group):
                    o_ref[chunk(k), :] = resid(chunk(k))
            return carry

        lax.fori_loop(0, tm // (NORM_ROWS * group), body, 0)


def half_ffn(x, g, shift, scale, gate, wg, wu, wd, layer, which, final_g=None, tm=1024, tf=512):
    rows, d = x.shape
    dff = wg.shape[-1]
    tm = min(tm, rows)
    vec = pl.BlockSpec((1, d), lambda i, f: (0, 0))
    in_specs = [
        pl.BlockSpec((tm, d), lambda i, f: (i, 0)),
        vec, vec, vec, vec,
        pl.BlockSpec((None, None, d, tf), lambda i, f: (layer, which, 0, f)),
        pl.BlockSpec((None, None, d, tf), lambda i, f: (layer, which, 0, f)),
        pl.BlockSpec((None, None, tf, d), lambda i, f: (layer, which, f, 0)),
    ]
    args = [x, g, shift, scale, gate, wg, wu, wd]
    if final_g is not None:
        in_specs.append(vec)
        args.append(final_g)
    return pl.pallas_call(
        functools.partial(_ffn_kernel, final=final_g is not None),
        grid=(rows // tm, dff // tf),
        in_specs=in_specs,
        out_specs=pl.BlockSpec((tm, d), lambda i, f: (i, 0)),
        out_shape=jax.ShapeDtypeStruct((rows, d), F32),
        scratch_shapes=[pltpu.VMEM((tm, d), BF16)],
        compiler_params=_params("parallel", "arbitrary"),
        name="half_ffn",
    )(*args)


def _inproj_kernel(x_ref, g_ref, sh_ref, sc_ref, w_ref, o_ref, h_scr):
    @pl.when(pl.program_id(1) == 0)
    def _():
        _norm_mod_store(x_ref, g_ref, sh_ref, sc_ref, h_scr)

    o_ref[...] = jnp.dot(h_scr[...], w_ref[...], preferred_element_type=F32)


def in_proj(x, g, shift, scale, w, tm=1024, tn=512):
    rows, d = x.shape
    n = w.shape[1]
    tm = min(tm, rows)
    vec = pl.BlockSpec((1, d), lambda i, j: (0, 0))
    return pl.pallas_call(
        _inproj_kernel,
        grid=(rows // tm, n // tn),
        in_specs=[pl.BlockSpec((tm, d), lambda i, j: (i, 0)), vec, vec, vec,
                  pl.BlockSpec((d, tn), lambda i, j: (0, j))],
        out_specs=pl.BlockSpec((tm, tn), lambda i, j: (i, j)),
        out_shape=jax.ShapeDtypeStruct((rows, n), F32),
        scratch_shapes=[pltpu.VMEM((tm, d), BF16)],
        compiler_params=_params("parallel", "arbitrary"),
        name="in_proj",
    )(x, g, shift, scale, w)


HY_CHANNEL_TILE = 512


def _even_inproj_kernel(x_ref, g_ref, sh_ref, sc_ref, w_ref, zt_ref, qkv_ref, h_scr):
    _norm_mod_store(x_ref, g_ref, sh_ref, sc_ref, h_scr)
    tm = x_ref.shape[0]
    ct = HY_CHANNEL_TILE
    nc = zt_ref.shape[0] * SUBLANES
    for j in range(nc // ct):
        zt = lax.dot_general(w_ref[:, j * ct:(j + 1) * ct], h_scr[...], (((0,), (1,)), ((), ())),
                             preferred_element_type=F32)
        for r in range(tm // LANES):
            zt_ref[j * ct // SUBLANES:(j + 1) * ct // SUBLANES, r * SUBLANES:(r + 1) * SUBLANES, :] = (
                zt[:, r * LANES:(r + 1) * LANES].reshape(ct // SUBLANES, SUBLANES, LANES))
    qkv_ref[...] = jnp.dot(h_scr[...], w_ref[:, nc:], preferred_element_type=F32)


def even_in_proj(x, g, shift, scale, w, nc, tm=512):
    rows, d = x.shape
    n = w.shape[1] - nc
    vec = pl.BlockSpec((1, d), lambda i: (0, 0))
    return pl.pallas_call(
        _even_inproj_kernel,
        grid=(rows // tm,),
        in_specs=[pl.BlockSpec((tm, d), lambda i: (i, 0)), vec, vec, vec,
                  pl.BlockSpec((d, nc + n), lambda i: (0, 0), pipeline_mode=pl.Buffered(1))],
        out_specs=[pl.BlockSpec((nc // SUBLANES, SUBLANES * tm // LANES, LANES), lambda i: (0, i, 0)),
                   pl.BlockSpec((tm, n), lambda i: (i, 0))],
        out_shape=[jax.ShapeDtypeStruct((nc // SUBLANES, SUBLANES * rows // LANES, LANES), F32),
                   jax.ShapeDtypeStruct((rows, n), F32)],
        scratch_shapes=[pltpu.VMEM((tm, d), BF16)],
        compiler_params=_params("parallel"),
        name="even_in_proj",
    )(x, g, shift, scale, w)


def _filter_kernel(z_ref, t_ref, w1_ref, b1_ref, f1_ref, w2_ref, b2_ref, f2_ref, w3_ref, dl_ref, ds_ref, o_ref):
    hi = lax.Precision.HIGHEST
    ct = w3_ref.shape[0]
    dec = jnp.exp(-(t_ref[0:1, 0:LANES] * dl_ref[...]))
    for r in range(z_ref.shape[1] // LANES):
        sl = slice(r * LANES, (r + 1) * LANES)
        a1 = jnp.dot(w1_ref[...], z_ref[:, sl], precision=hi, preferred_element_type=F32)
        h1 = jnp.sin(f1_ref[...] * (a1 + b1_ref[...]))
        a2 = jnp.dot(w2_ref[...], h1, precision=hi, preferred_element_type=F32)
        h2 = jnp.sin(f2_ref[...] * (a2 + b2_ref[...]))
        h3 = jnp.dot(w3_ref[...], h2.astype(BF16), preferred_element_type=F32)
        o_ref[:, r * SUBLANES:(r + 1) * SUBLANES, :] = (h3 * dec).reshape(ct // SUBLANES, SUBLANES, LANES)
        dec = dec * ds_ref[...]


def hyena_filter_t(L, w1, b1, f1, w2, b2, f2, w3, tl=1024):
    hid = w1.shape[1]
    c2 = w3.shape[1]
    width = c2 // 2
    t = np.linspace(0.0, 1.0, L, dtype=np.float32)[:, None]
    w = np.float32(2.0 * math.pi / L) * np.arange(L, dtype=np.float32)[:, None]
    bands = np.linspace(1e-4, HY_BANDS - 1, HY_BANDS, dtype=np.float32)[None, :]
    z = np.concatenate([t, np.cos(bands * w), -np.sin(bands * w)], axis=-1)
    emb = z.shape[1]
    embp = -(-emb // SUBLANES) * SUBLANES
    zt = jnp.asarray(np.pad(z.T, ((0, embp - emb), (0, 0))), F32)
    w1t = jnp.pad(w1.astype(F32).T, ((0, 0), (0, embp - emb)))
    lt = math.log(HY_DECAY_TARGET)
    deltas = np.abs(np.linspace(lt / HY_SLOW_PCT, lt / HY_FAST_PCT, width, dtype=np.float32))
    deltas2 = jnp.asarray(np.concatenate([deltas, deltas]), F32)
    chunk_decay = jnp.asarray(np.exp(-np.concatenate([deltas, deltas]).astype(np.float64) * LANES / (L - 1)), F32)
    col = lambda v: jnp.broadcast_to(v.astype(F32)[:, None], (v.shape[0], LANES))
    trow = jnp.asarray(np.broadcast_to(t.T, (SUBLANES, L)), F32)
    tl = min(tl, L)
    full = lambda shape: pl.BlockSpec(shape, lambda i: (0,) * len(shape))
    return pl.pallas_call(
        _filter_kernel,
        grid=(L // tl,),
        in_specs=[
            pl.BlockSpec((embp, tl), lambda i: (0, i)),
            pl.BlockSpec((SUBLANES, tl), lambda i: (0, i)),
            full((hid, embp)), full((hid, LANES)), full((hid, LANES)),
            full((hid, hid)), full((hid, LANES)), full((hid, LANES)),
            full((c2, hid)), full((c2, LANES)), full((c2, LANES)),
        ],
        out_specs=pl.BlockSpec((c2 // SUBLANES, SUBLANES * tl // LANES, LANES), lambda i: (0, i, 0)),
        out_shape=jax.ShapeDtypeStruct((c2 // SUBLANES, SUBLANES * L // LANES, LANES), F32),
        compiler_params=_params("parallel"),
        name="hyena_filter",
    )(zt, trow, w1t, col(b1), col(f1), w2.astype(F32).T, col(b2), col(f2), w3.T.astype(BF16), col(deltas2), col(chunk_decay))


def _dft_constants(h1):
    n_outer = 2 * h1
    n = n_outer * LANES
    bf16_rows = 2 * SUBLANES
    kp = -(-(h1 + 1) // bf16_rows) * bf16_rows
    k1 = np.arange(kp)[:, None]
    n1 = np.arange(h1)[None, :]
    ang_a = 2.0 * np.pi * ((k1 * n1) % n_outer) / n_outer
    fa = np.concatenate([np.cos(ang_a), -np.sin(ang_a)], axis=0)
    n2 = np.arange(LANES)[None, :]
    ang_t = 2.0 * np.pi * ((k1 * n2) % n) / n
    twr, twi = np.cos(ang_t), -np.sin(ang_t)
    a = np.arange(LANES)
    ang_b = 2.0 * np.pi * ((a[:, None] * a[None, :]) % LANES) / LANES
    cb, sb = np.cos(ang_b), np.sin(ang_b)
    fb = np.block([[cb, -sb], [sb, cb]])
    gb = np.block([[cb, sb], [-sb, cb]])
    wk = np.where((k1 == 0) | (k1 == h1), 1.0, 2.0) * (k1 <= h1)
    ga = np.concatenate([(wk * np.cos(ang_a)).T, (-wk * np.sin(ang_a)).T], axis=1)
    as_bf = lambda m: jnp.asarray(m, dtype=F32).astype(BF16)
    return (as_bf(fa), as_bf(twr), as_bf(twi), as_bf(fb), as_bf(gb), as_bf(ga), kp, n)


def _hyena_kernel(scal_ref, z0_ref, z1_ref, zv_ref, hf_ref, hb_ref, fa_ref, twr_ref, twi_ref, fb_ref, gb_ref,
                  ga_ref, o_ref, *, h1, kp, inv_n):
    grp = pl.program_id(0)
    row = lax.broadcasted_iota(jnp.int32, (h1, LANES), 0)
    lane = lax.broadcasted_iota(jnp.int32, (h1, LANES), 1)
    first = (row == 0) & (lane == 0)
    last = (row == h1 - 1) & (lane == LANES - 1)

    def chan(ref, ci):
        return ref[ci // SUBLANES, pl.ds(ci % SUBLANES, h1, stride=SUBLANES), :]

    def prev(x):
        r = pltpu.roll(x, 1, 1)
        r = jnp.where(lane == 0, pltpu.roll(r, 1, 0), r)
        return jnp.where(first, 0.0, r)

    def nxt(x):
        r = pltpu.roll(x, LANES - 1, 1)
        r = jnp.where(lane == LANES - 1, pltpu.roll(r, h1 - 1, 0), r)
        return jnp.where(last, 0.0, r)

    us, x0s, hfs, hbs, nrm = [], [], [], [], []
    for ci in range(HY_GROUP):
        c = grp * HY_GROUP + ci

        def sconv(ref, s):
            x = chan(ref, ci)
            return (scal_ref[3 * s, c] * prev(x) + scal_ref[3 * s + 1, c] * x
                    + scal_ref[3 * s + 2, c] * nxt(x) + scal_ref[9 + s, c])

        x0 = sconv(z0_ref, 0)
        x1 = sconv(z1_ref, 1)
        hv = sconv(zv_ref, 2)
        us.append(hv * x1)
        x0s.append(x0)
        hf = chan(hf_ref, ci)
        hb = jnp.where(first, 0.0, chan(hb_ref, ci))
        hfs.append(hf)
        hbs.append(hb)
        ssq = jnp.sum(hf * hf, keepdims=True) + jnp.sum(hb * hb, keepdims=True)
        nrm.append(lax.rsqrt(ssq + EPS))

    fa = fa_ref[...]
    twr, twi = twr_ref[...], twi_ref[...]

    def outer_fwd(mats):
        xc = jnp.concatenate([m.astype(BF16) for m in mats], axis=1)
        a = jnp.dot(fa, xc, preferred_element_type=F32).astype(BF16)
        out = []
        for ci in range(HY_GROUP):
            ar = a[:kp, ci * LANES:(ci + 1) * LANES]
            ai = a[kp:, ci * LANES:(ci + 1) * LANES]
            out.append(jnp.concatenate([ar * twr - ai * twi, ar * twi + ai * twr], axis=1))
        return out

    stacked = jnp.concatenate(outer_fwd(us) + outer_fwd(hfs) + outer_fwd(hbs), axis=0)
    spec = jnp.dot(stacked, fb_ref[...], preferred_element_type=F32).astype(BF16)

    prod = []
    for ci in range(HY_GROUP):
        xu = spec[ci * kp:(ci + 1) * kp]
        xf = spec[(HY_GROUP + ci) * kp:(HY_GROUP + ci + 1) * kp]
        xb = spec[(2 * HY_GROUP + ci) * kp:(2 * HY_GROUP + ci + 1) * kp]
        xr, xi = xu[:, :LANES], xu[:, LANES:]
        kr = xf[:, :LANES] + xb[:, :LANES]
        ki = xf[:, LANES:] - xb[:, LANES:]
        prod.append(jnp.concatenate([xr * kr - xi * ki, xr * ki + xi * kr], axis=1))
    inner = jnp.dot(jnp.concatenate(prod, axis=0), gb_ref[...], preferred_element_type=F32).astype(BF16)

    cols = []
    for ci in range(HY_GROUP):
        b = inner[ci * kp:(ci + 1) * kp]
        br, bi = b[:, :LANES], b[:, LANES:]
        cols.append(jnp.concatenate([br * twr + bi * twi, bi * twr - br * twi], axis=0))
    y = jnp.dot(ga_ref[...], jnp.concatenate(cols, axis=1), preferred_element_type=F32)

    for ci in range(HY_GROUP):
        c = grp * HY_GROUP + ci
        yc = y[:, ci * LANES:(ci + 1) * LANES] * (nrm[ci] * inv_n)
        o_ref[ci // SUBLANES, pl.ds(ci % SUBLANES, h1, stride=SUBLANES), :] = (
            x0s[ci] * (yc + us[ci] * scal_ref[12, c]))


def hyena_mix(zt, ht, conv_w, conv_b, skip):
    ng3, r8, _ = zt.shape
    ng = ng3 // 3
    h1 = r8 // SUBLANES
    width = ng * SUBLANES
    fa, twr, twi, fb, gb, ga, kp, n = _dft_constants(h1)
    scal = jnp.concatenate([
        conv_w[:, 0:width], conv_w[:, width:2 * width], conv_w[:, 2 * width:3 * width],
        conv_b.reshape(3, width), skip.reshape(1, width)], axis=0).astype(F32)
    gps = HY_GROUP // SUBLANES
    steps = ng // gps
    blk = lambda stream: pl.BlockSpec((gps, r8, LANES), lambda g, stream=stream: (g + stream * steps, 0, 0))
    full = lambda a: pl.BlockSpec(a.shape, lambda g: (0,) * a.ndim)
    return pl.pallas_call(
        functools.partial(_hyena_kernel, h1=h1, kp=kp, inv_n=1.0 / n),
        grid=(steps,),
        in_specs=[pl.BlockSpec(memory_space=pltpu.SMEM),
                  blk(0), blk(1), blk(2), blk(0), blk(1),
                  full(fa), full(twr), full(twi), full(fb), full(gb), full(ga)],
        out_specs=pl.BlockSpec((gps, r8, LANES), lambda g: (g, 0, 0)),
        out_shape=jax.ShapeDtypeStruct((ng, r8, LANES), F32),
        compiler_params=_params("parallel"),
        name="hyena_mix",
    )(scal, zt, zt, zt, ht, ht, fa, twr, twi, fb, gb, ga)


ATT_Q_BLOCKS = 4


def _attn_kernel(sink_ref, q_ref, *refs, scale):
    nkb = ATT_Q_BLOCKS + 2
    k_refs, v_refs = refs[0:nkb], refs[nkb:2 * nkb]
    cc_refs, ss_refs = refs[2 * nkb:3 * nkb], refs[3 * nkb:4 * nkb]
    kc_ref, vc_ref, o_ref = refs[4 * nkb:]
    step = pl.program_id(0)
    nsteps = pl.num_programs(0)
    hd, blk = HEAD_DIM, ATT_BLOCK
    nt = (((1,), (1,)), ((), ()))

    def rope(x, cc, ss):
        return x * cc + pltpu.roll(x, hd // 2, 1) * ss

    log2e = math.log2(math.e)
    c2 = scale * log2e
    rows = Q_PER_KV * blk
    off = lax.broadcasted_iota(jnp.int32, (rows, blk), 0) & (blk - 1)
    col = lax.broadcasted_iota(jnp.int32, (rows, blk), 1)
    head_of_row = lax.broadcasted_iota(jnp.int32, (rows, 1), 0) // blk
    rowmax = lambda a: jnp.max(a, axis=-1, keepdims=True)
    rowsum = lambda a: jnp.sum(a, axis=-1, keepdims=True)

    for g in range(N_KV_HEADS):
        gs = slice(g * hd, (g + 1) * hd)
        rk = [rope(k_refs[b][:, gs], cc_refs[b][...], ss_refs[b][...]).astype(BF16) for b in range(nkb)]
        vv = [v_refs[b][:, gs].astype(BF16) for b in range(nkb)]
        kcg = kc_ref[:, gs].astype(BF16)
        vcg = vc_ref[:, gs].astype(BF16)
        sink = jnp.zeros((rows, 1), F32)
        for h in range(Q_PER_KV):
            sink = jnp.where(head_of_row == h, sink_ref[g * Q_PER_KV + h] * log2e, sink)
        for sb in range(ATT_Q_BLOCKS):
            qrows = slice(sb * blk, (sb + 1) * blk)
            kb = jnp.concatenate(rk[sb:sb + 3], axis=0)
            vb = jnp.concatenate(vv[sb:sb + 3], axis=0)
            q4 = jnp.concatenate([
                rope(q_ref[qrows, (g * Q_PER_KV + h) * hd:(g * Q_PER_KV + h + 1) * hd],
                     cc_refs[sb + 1][...], ss_refs[sb + 1][...]) * c2
                for h in range(Q_PER_KV)], axis=0).astype(BF16)
            keep_prev = col >= (off + jnp.where(step > 0, 0, blk) if sb == 0 else off)
            keep_next = col <= (off - jnp.where(step < nsteps - 1, 0, blk) if sb == ATT_Q_BLOCKS - 1 else off)
            s_loc = lax.dot_general(q4, kb, nt, preferred_element_type=F32)
            s_ctx = lax.dot_general(q4, kcg, nt, preferred_element_type=F32)
            s_prev = jnp.where(keep_prev, s_loc[:, :blk], NEG_INF)
            s_own = s_loc[:, blk:2 * blk]
            s_next = jnp.where(keep_next, s_loc[:, 2 * blk:], NEG_INF)
            ctx_tiles = [s_ctx[:, j * LANES:(j + 1) * LANES] for j in range(s_ctx.shape[1] // LANES)]
            m = jnp.maximum(rowmax(functools.reduce(jnp.maximum, [s_prev, s_own, s_next] + ctx_tiles)), sink)
            p_prev, p_own, p_next = jnp.exp2(s_prev - m), jnp.exp2(s_own - m), jnp.exp2(s_next - m)
            p_ctx = jnp.exp2(s_ctx - m)
            p_tiles = [p_prev, p_own, p_next] + [p_ctx[:, j * LANES:(j + 1) * LANES] for j in range(len(ctx_tiles))]
            den = jnp.exp2(sink - m) + rowsum(functools.reduce(jnp.add, p_tiles))
            p_loc = jnp.concatenate([p_prev, p_own, p_next], axis=1).astype(BF16)
            o = (jnp.dot(p_ctx.astype(BF16), vcg, preferred_element_type=F32)
                 + jnp.dot(p_loc, vb, preferred_element_type=F32)) * (1.0 / den)
            for h in range(Q_PER_KV):
                o_ref[qrows, (g * Q_PER_KV + h) * hd:(g * Q_PER_KV + h + 1) * hd] = (
                    o[h * blk:(h + 1) * blk].astype(o_ref.dtype))


def window_attention(qkv, kvc, sink):
    L = qkv.shape[0]
    n_ctx = kvc.shape[0]
    nb = L // ATT_BLOCK
    kvw = N_KV_HEADS * HEAD_DIM
    qw = N_KV_HEADS * Q_PER_KV * HEAD_DIM
    kcol, vcol = qw // kvw, qw // kvw + 1
    t = np.arange(L)
    rowp = (t // GRID_W).astype(np.float32)
    colp = (t % GRID_W).astype(np.float32)
    nq = HEAD_DIM // 4
    inv = (ROPE_BASE ** (-np.arange(nq, dtype=np.float32) / nq)).astype(np.float32)
    ang = np.concatenate([rowp[:, None] * inv, colp[:, None] * inv], axis=-1)
    cos, sin = np.cos(ang), np.sin(ang)
    cc = jnp.asarray(np.concatenate([cos, cos], axis=-1), F32)
    ss = jnp.asarray(np.concatenate([-sin, sin], axis=-1), F32)
    nkb = ATT_Q_BLOCKS + 2
    kblock = lambda b: (lambda i: jnp.clip(i * ATT_Q_BLOCKS - 1 + b, 0, nb - 1))
    kspecs = [pl.BlockSpec((ATT_BLOCK, kvw), lambda i, f=kblock(b): (f(i), kcol)) for b in range(nkb)]
    vspecs = [pl.BlockSpec((ATT_BLOCK, kvw), lambda i, f=kblock(b): (f(i), vcol)) for b in range(nkb)]
    tspecs = [pl.BlockSpec((ATT_BLOCK, HEAD_DIM), lambda i, f=kblock(b): (f(i), 0)) for b in range(nkb)]
    tq = ATT_Q_BLOCKS * ATT_BLOCK
    return pl.pallas_call(
        functools.partial(_attn_kernel, scale=HEAD_DIM ** -0.5),
        grid=(L // tq,),
        in_specs=[pl.BlockSpec(memory_space=pltpu.SMEM),
                  pl.BlockSpec((tq, qw), lambda i: (i, 0))]
                 + kspecs + vspecs + tspecs + tspecs
                 + [pl.BlockSpec((n_ctx, kvw), lambda i: (0, 0)),
                    pl.BlockSpec((n_ctx, kvw), lambda i: (0, 1))],
        out_specs=pl.BlockSpec((tq, qw), lambda i: (i, 0)),
        out_shape=jax.ShapeDtypeStruct((L, qw), BF16),
        compiler_params=_params("parallel"),
        name="window_attention",
    )(sink.astype(F32), qkv, *([qkv] * (2 * nkb)), *([cc] * nkb), *([ss] * nkb), kvc, kvc)


ODD_OUT_TILE = 512


def _odd_mixer_kernel(x_ref, g_ref, sh_ref, sc_ref, win_ref, sgg_ref, ws_ref, bs_ref, wout_ref, gt_ref,
                      o_ref, h_scr, vn_scr, sg_scr):
    tm, width = sg_scr.shape
    gd = width // SG_GROUPS
    _norm_mod_store(x_ref, g_ref, sh_ref, sc_ref, h_scr)
    v = jax.nn.gelu(jnp.dot(h_scr[...], win_ref[:, width:], preferred_element_type=F32), approximate=True)
    vn_scr[...] = (v * lax.rsqrt(jnp.mean(v * v, axis=-1, keepdims=True) + EPS) * sgg_ref[...]).astype(BF16)
    for g in range(SG_GROUPS):
        cs = slice(g * gd, (g + 1) * gd)
        u = jax.nn.gelu(jnp.dot(h_scr[...], win_ref[:, cs], preferred_element_type=F32), approximate=True)
        bias = jnp.concatenate([bs_ref[g]] * (gd // LANES), axis=1)
        for ch in range(tm // CHUNK):
            rs = slice(ch * CHUNK, (ch + 1) * CHUNK)
            mixed = jnp.dot(ws_ref[g], vn_scr[rs, cs], preferred_element_type=F32) + bias
            sg_scr[rs, cs] = (u[rs] * mixed).astype(BF16)
    for j in range(o_ref.shape[1] // ODD_OUT_TILE):
        cs = slice(j * ODD_OUT_TILE, (j + 1) * ODD_OUT_TILE)
        acc = jnp.dot(sg_scr[...], wout_ref[:, cs], preferred_element_type=F32)
        o_ref[:, cs] = x_ref[:, cs] + gt_ref[:, cs] * acc


def odd_mixer(x, g, shift, scale, gate, w_in, sg_g, ws, bs, w_out, tm=512):
    rows, d = x.shape
    width = w_in.shape[1] // 2
    n = w_out.shape[1]
    bsb = jnp.broadcast_to(bs.astype(F32)[:, :, None], (SG_GROUPS, CHUNK, LANES))
    vec = lambda w: pl.BlockSpec((1, w), lambda i: (0, 0))
    once = pl.Buffered(1)
    return pl.pallas_call(
        _odd_mixer_kernel,
        grid=(rows // tm,),
        in_specs=[pl.BlockSpec((tm, d), lambda i: (i, 0)), vec(d), vec(d), vec(d),
                  pl.BlockSpec((d, 2 * width), lambda i: (0, 0), pipeline_mode=once),
                  vec(width),
                  pl.BlockSpec((SG_GROUPS, CHUNK, CHUNK), lambda i: (0, 0, 0)),
                  pl.BlockSpec((SG_GROUPS, CHUNK, LANES), lambda i: (0, 0, 0)),
                  pl.BlockSpec((width, n), lambda i: (0, 0), pipeline_mode=once),
                  vec(n)],
        out_specs=pl.BlockSpec((tm, n), lambda i: (i, 0)),
        out_shape=jax.ShapeDtypeStruct((rows, n), F32),
        scratch_shapes=[pltpu.VMEM((tm, d), BF16), pltpu.VMEM((tm, width), BF16), pltpu.VMEM((tm, width), BF16)],
        compiler_params=_params("parallel"),
        name="odd_mixer",
    )(x, g, shift, scale, w_in, sg_g, ws.astype(BF16), bsb, w_out, gate)


def _outproj_even_kernel(yt_ref, ya_ref, w_ref, x_ref, gt_ref, o_ref, lhs_scr):
    ng = yt_ref.shape[0]
    c = ng * SUBLANES
    for r in range(yt_ref.shape[1] // SUBLANES):
        sub = yt_ref[:, r * SUBLANES:(r + 1) * SUBLANES, :].reshape(c, LANES)
        lhs_scr[r * LANES:(r + 1) * LANES, 0:c] = sub.T.astype(BF16)
    lhs_scr[:, c:] = ya_ref[...]
    acc = jnp.dot(lhs_scr[...], w_ref[...], preferred_element_type=F32)
    o_ref[...] = x_ref[...] + gt_ref[...] * acc


def out_proj_even(yt, ya, w, x, gate, tm=512):
    rows, aw = ya.shape
    ng = yt.shape[0]
    k, n = w.shape
    return pl.pallas_call(
        _outproj_even_kernel,
        grid=(rows // tm,),
        in_specs=[pl.BlockSpec((ng, SUBLANES * tm // LANES, LANES), lambda i: (0, i, 0)),
                  pl.BlockSpec((tm, aw), lambda i: (i, 0)),
                  pl.BlockSpec((k, n), lambda i: (0, 0), pipeline_mode=pl.Buffered(1)),
                  pl.BlockSpec((tm, n), lambda i: (i, 0)),
                  pl.BlockSpec((1, n), lambda i: (0, 0))],
        out_specs=pl.BlockSpec((tm, n), lambda i: (i, 0)),
        out_shape=jax.ShapeDtypeStruct((rows, n), F32),
        scratch_shapes=[pltpu.VMEM((tm, k), BF16)],
        compiler_params=_params("parallel"),
        name="out_proj_even",
    )(yt, ya, w, x, gate)


def kernel(x, c, ctx, c_ctx, ada_w, ada_b, norm_g, ffn_wg, ffn_wu, ffn_wd, ev_w_in, ev_conv_w, ev_conv_b,
           hy_w1, hy_b1, hy_f1, hy_w2, hy_b2, hy_f2, hy_w3, hy_skip, att_sink, ev_w_out, od_w_in, sg_g, sg_ws,
           sg_bs, od_w_out, final_g):
    assert x.shape[0] == 1 and ada_w.shape[0] == 2, "written for batch 1, depth 2 (even layer then odd layer)"
    _, L, d = x.shape
    hy_width = hy_skip.shape[1]
    hy_in = 3 * hy_width
    kv_w = N_KV_HEADS * HEAD_DIM
    q_end = hy_in + (d - hy_width)

    xs = x[0]
    xc = ctx[0]
    cond8 = jnp.zeros((SUBLANES, d), F32).at[0].set(c[0]).at[1].set(c_ctx)
    mods = ada_mods(cond8, ada_w, ada_b)
    row = lambda v: v.reshape(1, d)
    bf = lambda w: w.astype(BF16)

    mod = mods[0, 0].reshape(N_MOD, 1, d)
    mc = mods[0, 1].reshape(N_MOD, 1, d)
    g = norm_g[0]
    wg, wu, wd = bf(ffn_wg), bf(ffn_wu), ffn_wd
    xs = half_ffn(xs, row(g[0]), mod[0], mod[1], mod[2], wg, wu, wd, 0, 0)
    xc = half_ffn(xc, row(g[0]), mc[0], mc[1], mc[2], wg, wu, wd, 0, 0)
    w_in = bf(ev_w_in[0])
    zt, qkv = even_in_proj(xs, row(g[1]), mod[3], mod[4], w_in, hy_in)
    kvc = in_proj(xc, row(g[1]), mc[3], mc[4], w_in[:, q_end:])
    ht = hyena_filter_t(L, hy_w1[0], hy_b1[0], hy_f1[0], hy_w2[0], hy_b2[0], hy_f2[0], hy_w3[0])
    yt = hyena_mix(zt, ht, ev_conv_w[0], ev_conv_b[0], hy_skip[0])
    ya = window_attention(qkv, kvc, att_sink[0])
    xs = out_proj_even(yt, ya, bf(ev_w_out[0]), xs, mod[5])
    xs = half_ffn(xs, row(g[2]), mod[6], mod[7], mod[8], wg, wu, wd, 0, 1)

    mod = mods[1, 0].reshape(N_MOD, 1, d)
    g = norm_g[1]
    xs = half_ffn(xs, row(g[0]), mod[0], mod[1], mod[2], wg, wu, wd, 1, 0)
    xs = odd_mixer(xs, row(g[1]), mod[3], mod[4], mod[5], bf(od_w_in[0]), row(sg_g[0]), sg_ws[0], sg_bs[0],
                   bf(od_w_out[0]))
    xs = half_ffn(xs, row(g[2]), mod[6], mod[7], mod[8], wg, wu, wd, 1, 1, final_g=row(final_g))
    return xs[None]
```

```python
import functools
import math

import numpy as np
import jax
import jax.numpy as jnp
from jax import lax
from jax.experimental import pallas as pl
from jax.experimental.pallas import tpu as pltpu

F32 = jnp.float32
BF16 = jnp.bfloat16

LANES = 128
SUBLANES = 8
VMEM_LIMIT_BYTES = 58 * 1024 * 1024

EPS = 1e-6
NEG_INF = -1e30
N_MOD = 9
HEAD_DIM = 128
N_KV_HEADS = 2
Q_PER_KV = 4
ATT_BLOCK = 128
GRID_W = 64
ROPE_BASE = 10000.0
HY_BANDS = 16
HY_DECAY_TARGET = 1e-2
HY_FAST_PCT = 0.3
HY_SLOW_PCT = 1.5
SG_GROUPS = 8
CHUNK = 128
HY_GROUP = 2 * SUBLANES


def _params(*sem):
    return pltpu.CompilerParams(dimension_semantics=sem, vmem_limit_bytes=VMEM_LIMIT_BYTES)


NORM_ROWS = 16


def _norm_mod_store(x_ref, g_ref, sh_ref, sc_ref, h_scr, zero_ref=None):
    g = g_ref[...]
    one_plus_scale = 1.0 + sc_ref[...]
    shift = sh_ref[...]

    def body(i, carry):
        rows = pl.ds(pl.multiple_of(i * NORM_ROWS, NORM_ROWS), NORM_ROWS)
        x = x_ref[rows, :]
        y = x * lax.rsqrt(jnp.mean(x * x, axis=-1, keepdims=True) + EPS) * g
        h_scr[rows, :] = (y * one_plus_scale + shift).astype(BF16)
        if zero_ref is not None:
            zero_ref[rows, :] = jnp.zeros((NORM_ROWS, zero_ref.shape[1]), zero_ref.dtype)
        return carry

    lax.fori_loop(0, x_ref.shape[0] // NORM_ROWS, body, 0, unroll=8)


def _ada_kernel(s_ref, w_ref, b_ref, o_ref):
    s = s_ref[...]
    s = (s * jax.nn.sigmoid(s)).astype(BF16)
    o_ref[0] = jnp.dot(s, w_ref[0].astype(BF16), preferred_element_type=F32) + b_ref[0]


def ada_mods(cond8, ada_w, ada_b, tn=1024):
    depth, d, n = ada_w.shape
    return pl.pallas_call(
        _ada_kernel,
        grid=(depth, n // tn),
        in_specs=[
            pl.BlockSpec((SUBLANES, d), lambda l, j: (0, 0)),
            pl.BlockSpec((1, d, tn), lambda l, j: (l, 0, j)),
            pl.BlockSpec((1, 1, tn), lambda l, j: (l, 0, j)),
        ],
        out_specs=pl.BlockSpec((1, SUBLANES, tn), lambda l, j: (l, 0, j)),
        out_shape=jax.ShapeDtypeStruct((depth, SUBLANES, n), F32),
        compiler_params=_params("parallel", "parallel"),
        name="ada_mods",
    )(cond8, ada_w, ada_b.reshape(depth, 1, n))


FFN_ROW_CHUNK = 512


def _ffn_kernel(x_ref, g_ref, sh_ref, sc_ref, gt_ref, wg_ref, wu_ref, wd_ref, *rest, final):
    if final:
        fg_ref, o_ref, h_scr = rest
    else:
        o_ref, h_scr = rest
    f = pl.program_id(1)

    @pl.when(f == 0)
    def _():
        _norm_mod_store(x_ref, g_ref, sh_ref, sc_ref, h_scr, zero_ref=o_ref)

    tm = x_ref.shape[0]
    rc = min(tm, FFN_ROW_CHUNK)
    wd = wd_ref[...].astype(BF16)
    for r in range(tm // rc):
        rs = slice(r * rc, (r + 1) * rc)
        h = h_scr[rs, :]
        gate_act = jnp.dot(h, wg_ref[...], preferred_element_type=F32)
        up = jnp.dot(h, wu_ref[...], preferred_element_type=F32)
        a = (gate_act * jax.nn.sigmoid(gate_act) * up).astype(BF16)
        o_ref[rs, :] += jnp.dot(a, wd, preferred_element_type=F32)

    @pl.when(f == pl.num_programs(1) - 1)
    def _():
        half_gate = 0.5 * gt_ref[...]
        group = 8

        def body(i, carry):
            chunk = lambda k: pl.ds(pl.multiple_of((i * group + k) * NORM_ROWS, NORM_ROWS), NORM_ROWS)
            resid = lambda rows: x_ref[rows, :] + half_gate * o_ref[rows, :]
            if final:
                inv = [lax.rsqrt(jnp.mean(jnp.square(resid(chunk(k))), axis=-1, keepdims=True) + EPS)
                       for k in range(group)]
                for k in range(group):
                    o_ref[chunk(k), :] = resid(chunk(k)) * inv[k] * fg_ref[...]
            else:
                for k in range(group):
                    o_ref[chunk(k), :] = resid(chunk(k))
            return carry

        lax.fori_loop(0, tm // (NORM_ROWS * group), body, 0)


def half_ffn(x, g, shift, scale, gate, wg, wu, wd, layer, which, final_g=None, tm=1024, tf=512):
    rows, d = x.shape
    dff = wg.shape[-1]
    tm = min(tm, rows)
    vec = pl.BlockSpec((1, d), lambda i, f: (0, 0))
    in_specs = [
        pl.BlockSpec((tm, d), lambda i, f: (i, 0)),
        vec, vec, vec, vec,
        pl.BlockSpec((None, None, d, tf), lambda i, f: (layer, which, 0, f)),
        pl.BlockSpec((None, None, d, tf), lambda i, f: (layer, which, 0, f)),
        pl.BlockSpec((None, None, tf, d), lambda i, f: (layer, which, f, 0)),
    ]
    args = [x, g, shift, scale, gate, wg, wu, wd]
    if final_g is not None:
        in_specs.append(vec)
        args.append(final_g)
    return pl.pallas_call(
        functools.partial(_ffn_kernel, final=final_g is not None),
        grid=(rows // tm, dff // tf),
        in_specs=in_specs,
        out_specs=pl.BlockSpec((tm, d), lambda i, f: (i, 0)),
        out_shape=jax.ShapeDtypeStruct((rows, d), F32),
        scratch_shapes=[pltpu.VMEM((tm, d), BF16)],
        compiler_params=_params("parallel", "arbitrary"),
        name="half_ffn",
    )(*args)


def _inproj_kernel(x_ref, g_ref, sh_ref, sc_ref, w_ref, o_ref, h_scr):
    @pl.when(pl.program_id(1) == 0)
    def _():
        _norm_mod_store(x_ref, g_ref, sh_ref, sc_ref, h_scr)

    o_ref[...] = jnp.dot(h_scr[...], w_ref[...], preferred_element_type=F32)


def in_proj(x, g, shift, scale, w, tm=1024, tn=512):
    rows, d = x.shape
    n = w.shape[1]
    tm = min(tm, rows)
    vec = pl.BlockSpec((1, d), lambda i, j: (0, 0))
    return pl.pallas_call(
        _inproj_kernel,
        grid=(rows // tm, n // tn),
        in_specs=[pl.BlockSpec((tm, d), lambda i, j: (i, 0)), vec, vec, vec,
                  pl.BlockSpec((d, tn), lambda i, j: (0, j))],
        out_specs=pl.BlockSpec((tm, tn), lambda i, j: (i, j)),
        out_shape=jax.ShapeDtypeStruct((rows, n), F32),
        scratch_shapes=[pltpu.VMEM((tm, d), BF16)],
        compiler_params=_params("parallel", "arbitrary"),
        name="in_proj",
    )(x, g, shift, scale, w)


HY_CHANNEL_TILE = 512


HALO = 16


def _even_inproj_kernel(x_ref, xp_ref, xn_ref, g_ref, sh_ref, sc_ref, w_ref, cw_ref, cb_ref,
                        ut_ref, x0t_ref, qkv_ref, h_scr, z_scr):
    i = pl.program_id(0)
    tm = x_ref.shape[0]
    ct = HY_CHANNEL_TILE
    width = ut_ref.shape[0] * SUBLANES
    _norm_mod_store(x_ref, g_ref, sh_ref, sc_ref, h_scr.at[pl.ds(HALO, tm), :])

    def halo(ref, valid):
        x = ref[...]
        y = x * lax.rsqrt(jnp.mean(x * x, axis=-1, keepdims=True) + EPS) * g_ref[...]
        return ((y * (1.0 + sc_ref[...]) + sh_ref[...]) * valid).astype(BF16)

    h_scr[0:HALO, :] = halo(xp_ref, jnp.where(i > 0, 1.0, 0.0))
    h_scr[HALO + tm:, :] = halo(xn_ref, jnp.where(i < pl.num_programs(0) - 1, 1.0, 0.0))

    for c in range(width // ct):
        def stream(s):
            cols = slice(s * width + c * ct, s * width + (c + 1) * ct)
            z_scr[...] = jnp.dot(h_scr[...], w_ref[:, cols], preferred_element_type=F32)
            return (z_scr[HALO - 1:HALO - 1 + tm, :] * cw_ref[0:1, cols] + z_scr[HALO:HALO + tm, :] * cw_ref[1:2, cols]
                    + z_scr[HALO + 1:HALO + 1 + tm, :] * cw_ref[2:3, cols] + cb_ref[:, cols])

        x0 = stream(0)
        u = stream(2) * stream(1)
        for val, ref in ((u, ut_ref), (x0, x0t_ref)):
            vt = val.T
            for r in range(tm // LANES):
                ref[c * ct // SUBLANES:(c + 1) * ct // SUBLANES, r * SUBLANES:(r + 1) * SUBLANES, :] = (
                    vt[:, r * LANES:(r + 1) * LANES].reshape(ct // SUBLANES, SUBLANES, LANES))
    qkv_ref[...] = jnp.dot(h_scr[HALO:HALO + tm, :], w_ref[:, 3 * width:], preferred_element_type=F32)


def even_in_proj(x, g, shift, scale, w, conv_w, conv_b, tm=512):
    rows, d = x.shape
    nc = conv_w.shape[1]
    width = nc // 3
    n = w.shape[1] - nc
    vec = pl.BlockSpec((1, d), lambda i: (0, 0))
    per_tile = tm // HALO
    last_halo = rows // HALO - 1
    tiled = jax.ShapeDtypeStruct((width // SUBLANES, SUBLANES * rows // LANES, LANES), F32)
    tiled_spec = pl.BlockSpec((width // SUBLANES, SUBLANES * tm // LANES, LANES), lambda i: (0, i, 0))
    return pl.pallas_call(
        _even_inproj_kernel,
        grid=(rows // tm,),
        in_specs=[pl.BlockSpec((tm, d), lambda i: (i, 0)),
                  pl.BlockSpec((HALO, d), lambda i: (jnp.maximum(i * per_tile - 1, 0), 0)),
                  pl.BlockSpec((HALO, d), lambda i: (jnp.minimum((i + 1) * per_tile, last_halo), 0)),
                  vec, vec, vec,
                  pl.BlockSpec((d, nc + n), lambda i: (0, 0), pipeline_mode=pl.Buffered(1)),
                  pl.BlockSpec((3, nc), lambda i: (0, 0)),
                  pl.BlockSpec((1, nc), lambda i: (0, 0))],
        out_specs=[tiled_spec, tiled_spec, pl.BlockSpec((tm, n), lambda i: (i, 0))],
        out_shape=[tiled, tiled, jax.ShapeDtypeStruct((rows, n), F32)],
        scratch_shapes=[pltpu.VMEM((tm + 2 * HALO, d), BF16), pltpu.VMEM((tm + 2 * HALO, HY_CHANNEL_TILE), F32)],
        compiler_params=_params("parallel"),
        name="even_in_proj",
    )(x, x, x, g, shift, scale, w, conv_w.astype(F32), conv_b.astype(F32).reshape(1, nc))


def _filter_kernel(z_ref, t_ref, w1_ref, b1_ref, f1_ref, w2_ref, b2_ref, f2_ref, w3_ref, dl_ref, ds_ref, o_ref):
    hi = lax.Precision.HIGHEST
    ct = w3_ref.shape[0]
    dec = jnp.exp(-(t_ref[0:1, 0:LANES] * dl_ref[...]))
    for r in range(z_ref.shape[1] // LANES):
        sl = slice(r * LANES, (r + 1) * LANES)
        a1 = jnp.dot(w1_ref[...], z_ref[:, sl], precision=hi, preferred_element_type=F32)
        h1 = jnp.sin(f1_ref[...] * (a1 + b1_ref[...]))
        a2 = jnp.dot(w2_ref[...], h1, precision=hi, preferred_element_type=F32)
        h2 = jnp.sin(f2_ref[...] * (a2 + b2_ref[...]))
        h3 = jnp.dot(w3_ref[...], h2.astype(BF16), preferred_element_type=F32)
        o_ref[:, r * SUBLANES:(r + 1) * SUBLANES, :] = (h3 * dec).reshape(ct // SUBLANES, SUBLANES, LANES)
        dec = dec * ds_ref[...]


def hyena_filter_t(L, w1, b1, f1, w2, b2, f2, w3, tl=1024):
    hid = w1.shape[1]
    c2 = w3.shape[1]
    width = c2 // 2
    t = np.linspace(0.0, 1.0, L, dtype=np.float32)[:, None]
    w = np.float32(2.0 * math.pi / L) * np.arange(L, dtype=np.float32)[:, None]
    bands = np.linspace(1e-4, HY_BANDS - 1, HY_BANDS, dtype=np.float32)[None, :]
    z = np.concatenate([t, np.cos(bands * w), -np.sin(bands * w)], axis=-1)
    emb = z.shape[1]
    embp = -(-emb // SUBLANES) * SUBLANES
    zt = jnp.asarray(np.pad(z.T, ((0, embp - emb), (0, 0))), F32)
    w1t = jnp.pad(w1.astype(F32).T, ((0, 0), (0, embp - emb)))
    lt = math.log(HY_DECAY_TARGET)
    deltas = np.abs(np.linspace(lt / HY_SLOW_PCT, lt / HY_FAST_PCT, width, dtype=np.float32))
    deltas2 = jnp.asarray(np.concatenate([deltas, deltas]), F32)
    chunk_decay = jnp.asarray(np.exp(-np.concatenate([deltas, deltas]).astype(np.float64) * LANES / (L - 1)), F32)
    col = lambda v: jnp.broadcast_to(v.astype(F32)[:, None], (v.shape[0], LANES))
    trow = jnp.asarray(np.broadcast_to(t.T, (SUBLANES, L)), F32)
    tl = min(tl, L)
    full = lambda shape: pl.BlockSpec(shape, lambda i: (0,) * len(shape))
    return pl.pallas_call(
        _filter_kernel,
        grid=(L // tl,),
        in_specs=[
            pl.BlockSpec((embp, tl), lambda i: (0, i)),
            pl.BlockSpec((SUBLANES, tl), lambda i: (0, i)),
            full((hid, embp)), full((hid, LANES)), full((hid, LANES)),
            full((hid, hid)), full((hid, LANES)), full((hid, LANES)),
            full((c2, hid)), full((c2, LANES)), full((c2, LANES)),
        ],
        out_specs=pl.BlockSpec((c2 // SUBLANES, SUBLANES * tl // LANES, LANES), lambda i: (0, i, 0)),
        out_shape=jax.ShapeDtypeStruct((c2 // SUBLANES, SUBLANES * L // LANES, LANES), F32),
        compiler_params=_params("parallel"),
        name="hyena_filter",
    )(zt, trow, w1t, col(b1), col(f1), w2.astype(F32).T, col(b2), col(f2), w3.T.astype(BF16), col(deltas2), col(chunk_decay))


def _dft_constants(h1):
    n_outer = 2 * h1
    n = n_outer * LANES
    bf16_rows = 2 * SUBLANES
    kp = -(-(h1 + 1) // bf16_rows) * bf16_rows
    k1 = np.arange(kp)[:, None]
    n1 = np.arange(h1)[None, :]
    ang_a = 2.0 * np.pi * ((k1 * n1) % n_outer) / n_outer
    fa = np.concatenate([np.cos(ang_a), -np.sin(ang_a)], axis=0)
    n2 = np.arange(LANES)[None, :]
    ang_t = 2.0 * np.pi * ((k1 * n2) % n) / n
    twr, twi = np.cos(ang_t), -np.sin(ang_t)
    a = np.arange(LANES)
    ang_b = 2.0 * np.pi * ((a[:, None] * a[None, :]) % LANES) / LANES
    cb, sb = np.cos(ang_b), np.sin(ang_b)
    fb = np.block([[cb, -sb], [sb, cb]])
    gb = np.block([[cb, sb], [-sb, cb]])
    wk = np.where((k1 == 0) | (k1 == h1), 1.0, 2.0) * (k1 <= h1)
    ga = np.concatenate([(wk * np.cos(ang_a)).T, (-wk * np.sin(ang_a)).T], axis=1)
    as_bf = lambda m: jnp.asarray(m, dtype=F32).astype(BF16)
    return (as_bf(fa), as_bf(twr), as_bf(twi), as_bf(fb), as_bf(gb), as_bf(ga), kp, n)


def _hyena_kernel(skip_ref, u_ref, x0_ref, hf_ref, hb_ref, fa_ref, twr_ref, twi_ref, fb_ref, gb_ref,
                  ga_ref, o_ref, *, h1, kp, inv_n):
    grp = pl.program_id(0)
    row = lax.broadcasted_iota(jnp.int32, (h1, LANES), 0)
    lane = lax.broadcasted_iota(jnp.int32, (h1, LANES), 1)
    first = (row == 0) & (lane == 0)

    def chan(ref, ci):
        return ref[ci // SUBLANES, pl.ds(ci % SUBLANES, h1, stride=SUBLANES), :]

    us, x0s, hfs, hbs, nrm = [], [], [], [], []
    for ci in range(HY_GROUP):
        us.append(chan(u_ref, ci))
        x0s.append(chan(x0_ref, ci))
        hf = chan(hf_ref, ci)
        hb = jnp.where(first, 0.0, chan(hb_ref, ci))
        hfs.append(hf)
        hbs.append(hb)
        ssq = jnp.sum(hf * hf, keepdims=True) + jnp.sum(hb * hb, keepdims=True)
        nrm.append(lax.rsqrt(ssq + EPS))

    fa = fa_ref[...]
    twr, twi = twr_ref[...], twi_ref[...]

    def outer_fwd(mats):
        xc = jnp.concatenate([m.astype(BF16) for m in mats], axis=1)
        a = jnp.dot(fa, xc, preferred_element_type=F32).astype(BF16)
        out = []
        for ci in range(HY_GROUP):
            ar = a[:kp, ci * LANES:(ci + 1) * LANES]
            ai = a[kp:, ci * LANES:(ci + 1) * LANES]
            out.append(jnp.concatenate([ar * twr - ai * twi, ar * twi + ai * twr], axis=1))
        return out

    stacked = jnp.concatenate(outer_fwd(us) + outer_fwd(hfs) + outer_fwd(hbs), axis=0)
    spec = jnp.dot(stacked, fb_ref[...], preferred_element_type=F32).astype(BF16)

    prod = []
    for ci in range(HY_GROUP):
        xu = spec[ci * kp:(ci + 1) * kp]
        xf = spec[(HY_GROUP + ci) * kp:(HY_GROUP + ci + 1) * kp]
        xb = spec[(2 * HY_GROUP + ci) * kp:(2 * HY_GROUP + ci + 1) * kp]
        xr, xi = xu[:, :LANES], xu[:, LANES:]
        kr = xf[:, :LANES] + xb[:, :LANES]
        ki = xf[:, LANES:] - xb[:, LANES:]
        prod.append(jnp.concatenate([xr * kr - xi * ki, xr * ki + xi * kr], axis=1))
    inner = jnp.dot(jnp.concatenate(prod, axis=0), gb_ref[...], preferred_element_type=F32).astype(BF16)

    cols = []
    for ci in range(HY_GROUP):
        b = inner[ci * kp:(ci + 1) * kp]
        br, bi = b[:, :LANES], b[:, LANES:]
        cols.append(jnp.concatenate([br * twr + bi * twi, bi * twr - br * twi], axis=0))
    y = jnp.dot(ga_ref[...], jnp.concatenate(cols, axis=1), preferred_element_type=F32)

    for ci in range(HY_GROUP):
        c = grp * HY_GROUP + ci
        yc = y[:, ci * LANES:(ci + 1) * LANES] * (nrm[ci] * inv_n)
        o_ref[ci // SUBLANES, pl.ds(ci % SUBLANES, h1, stride=SUBLANES), :] = (
            x0s[ci] * (yc + us[ci] * skip_ref[c]))


def hyena_mix(ut, x0t, ht, skip):
    ng, r8, _ = ut.shape
    h1 = r8 // SUBLANES
    fa, twr, twi, fb, gb, ga, kp, n = _dft_constants(h1)
    gps = HY_GROUP // SUBLANES
    steps = ng // gps
    blk = lambda stream: pl.BlockSpec((gps, r8, LANES), lambda g, stream=stream: (g + stream * steps, 0, 0))
    full = lambda a: pl.BlockSpec(a.shape, lambda g: (0,) * a.ndim)
    return pl.pallas_call(
        functools.partial(_hyena_kernel, h1=h1, kp=kp, inv_n=1.0 / n),
        grid=(steps,),
        in_specs=[pl.BlockSpec(memory_space=pltpu.SMEM),
                  blk(0), blk(0), blk(0), blk(1),
                  full(fa), full(twr), full(twi), full(fb), full(gb), full(ga)],
        out_specs=pl.BlockSpec((gps, r8, LANES), lambda g: (g, 0, 0)),
        out_shape=jax.ShapeDtypeStruct((ng, r8, LANES), F32),
        compiler_params=_params("parallel"),
        name="hyena_mix",
    )(skip.astype(F32), ut, x0t, ht, ht, fa, twr, twi, fb, gb, ga)


ATT_Q_BLOCKS = 4


def _attn_kernel(sink_ref, q_ref, *refs, scale):
    nkb = ATT_Q_BLOCKS + 2
    k_refs, v_refs = refs[0:nkb], refs[nkb:2 * nkb]
    cc_refs, ss_refs = refs[2 * nkb:3 * nkb], refs[3 * nkb:4 * nkb]
    kc_ref, vc_ref, o_ref = refs[4 * nkb:]
    step = pl.program_id(0)
    nsteps = pl.num_programs(0)
    hd, blk = HEAD_DIM, ATT_BLOCK
    nt = (((1,), (1,)), ((), ()))

    def rope(x, cc, ss):
        return x * cc + pltpu.roll(x, hd // 2, 1) * ss

    log2e = math.log2(math.e)
    c2 = scale * log2e
    rows = Q_PER_KV * blk
    off = lax.broadcasted_iota(jnp.int32, (rows, blk), 0) & (blk - 1)
    col = lax.broadcasted_iota(jnp.int32, (rows, blk), 1)
    head_of_row = lax.broadcasted_iota(jnp.int32, (rows, 1), 0) // blk
    rowmax = lambda a: jnp.max(a, axis=-1, keepdims=True)
    rowsum = lambda a: jnp.sum(a, axis=-1, keepdims=True)

    for g in range(N_KV_HEADS):
        gs = slice(g * hd, (g + 1) * hd)
        rk = [rope(k_refs[b][:, gs], cc_refs[b][...], ss_refs[b][...]).astype(BF16) for b in range(nkb)]
        vv = [v_refs[b][:, gs].astype(BF16) for b in range(nkb)]
        kcg = kc_ref[:, gs].astype(BF16)
        vcg = vc_ref[:, gs].astype(BF16)
        sink = jnp.zeros((rows, 1), F32)
        for h in range(Q_PER_KV):
            sink = jnp.where(head_of_row == h, sink_ref[g * Q_PER_KV + h] * log2e, sink)
        for sb in range(ATT_Q_BLOCKS):
            qrows = slice(sb * blk, (sb + 1) * blk)
            kb = jnp.concatenate(rk[sb:sb + 3], axis=0)
            vb = jnp.concatenate(vv[sb:sb + 3], axis=0)
            q4 = jnp.concatenate([
                rope(q_ref[qrows, (g * Q_PER_KV + h) * hd:(g * Q_PER_KV + h + 1) * hd],
                     cc_refs[sb + 1][...], ss_refs[sb + 1][...]) * c2
                for h in range(Q_PER_KV)], axis=0).astype(BF16)
            keep_prev = col >= (off + jnp.where(step > 0, 0, blk) if sb == 0 else off)
            keep_next = col <= (off - jnp.where(step < nsteps - 1, 0, blk) if sb == ATT_Q_BLOCKS - 1 else off)
            s_loc = lax.dot_general(q4, kb, nt, preferred_element_type=F32)
            s_ctx = lax.dot_general(q4, kcg, nt, preferred_element_type=F32)
            s_prev = jnp.where(keep_prev, s_loc[:, :blk], NEG_INF)
            s_own = s_loc[:, blk:2 * blk]
            s_next = jnp.where(keep_next, s_loc[:, 2 * blk:], NEG_INF)
            ctx_tiles = [s_ctx[:, j * LANES:(j + 1) * LANES] for j in range(s_ctx.shape[1] // LANES)]
            m = jnp.maximum(rowmax(functools.reduce(jnp.maximum, [s_prev, s_own, s_next] + ctx_tiles)), sink)
            p_prev, p_own, p_next = jnp.exp2(s_prev - m), jnp.exp2(s_own - m), jnp.exp2(s_next - m)
            p_ctx = jnp.exp2(s_ctx - m)
            p_tiles = [p_prev, p_own, p_next] + [p_ctx[:, j * LANES:(j + 1) * LANES] for j in range(len(ctx_tiles))]
            den = jnp.exp2(sink - m) + rowsum(functools.reduce(jnp.add, p_tiles))
            p_loc = jnp.concatenate([p_prev, p_own, p_next], axis=1).astype(BF16)
            o = (jnp.dot(p_ctx.astype(BF16), vcg, preferred_element_type=F32)
                 + jnp.dot(p_loc, vb, preferred_element_type=F32)) * (1.0 / den)
            for h in range(Q_PER_KV):
                o_ref[qrows, (g * Q_PER_KV + h) * hd:(g * Q_PER_KV + h + 1) * hd] = (
                    o[h * blk:(h + 1) * blk].astype(o_ref.dtype))


def window_attention(qkv, kvc, sink):
    L = qkv.shape[0]
    n_ctx = kvc.shape[0]
    nb = L // ATT_BLOCK
    kvw = N_KV_HEADS * HEAD_DIM
    qw = N_KV_HEADS * Q_PER_KV * HEAD_DIM
    kcol, vcol = qw // kvw, qw // kvw + 1
    t = np.arange(L)
    rowp = (t // GRID_W).astype(np.float32)
    colp = (t % GRID_W).astype(np.float32)
    nq = HEAD_DIM // 4
    inv = (ROPE_BASE ** (-np.arange(nq, dtype=np.float32) / nq)).astype(np.float32)
    ang = np.concatenate([rowp[:, None] * inv, colp[:, None] * inv], axis=-1)
    cos, sin = np.cos(ang), np.sin(ang)
    cc = jnp.asarray(np.concatenate([cos, cos], axis=-1), F32)
    ss = jnp.asarray(np.concatenate([-sin, sin], axis=-1), F32)
    nkb = ATT_Q_BLOCKS + 2
    kblock = lambda b: (lambda i: jnp.clip(i * ATT_Q_BLOCKS - 1 + b, 0, nb - 1))
    kspecs = [pl.BlockSpec((ATT_BLOCK, kvw), lambda i, f=kblock(b): (f(i), kcol)) for b in range(nkb)]
    vspecs = [pl.BlockSpec((ATT_BLOCK, kvw), lambda i, f=kblock(b): (f(i), vcol)) for b in range(nkb)]
    tspecs = [pl.BlockSpec((ATT_BLOCK, HEAD_DIM), lambda i, f=kblock(b): (f(i), 0)) for b in range(nkb)]
    tq = ATT_Q_BLOCKS * ATT_BLOCK
    return pl.pallas_call(
        functools.partial(_attn_kernel, scale=HEAD_DIM ** -0.5),
        grid=(L // tq,),
        in_specs=[pl.BlockSpec(memory_space=pltpu.SMEM),
                  pl.BlockSpec((tq, qw), lambda i: (i, 0))]
                 + kspecs + vspecs + tspecs + tspecs
                 + [pl.BlockSpec((n_ctx, kvw), lambda i: (0, 0)),
                    pl.BlockSpec((n_ctx, kvw), lambda i: (0, 1))],
        out_specs=pl.BlockSpec((tq, qw), lambda i: (i, 0)),
        out_shape=jax.ShapeDtypeStruct((L, qw), BF16),
        compiler_params=_params("parallel"),
        name="window_attention",
    )(sink.astype(F32), qkv, *([qkv] * (2 * nkb)), *([cc] * nkb), *([ss] * nkb), kvc, kvc)


ODD_OUT_TILE = 512


def _odd_mixer_kernel(x_ref, g_ref, sh_ref, sc_ref, win_ref, sgg_ref, ws_ref, bs_ref, wout_ref, gt_ref,
                      o_ref, h_scr, vn_scr, sg_scr):
    tm, width = sg_scr.shape
    gd = width // SG_GROUPS
    _norm_mod_store(x_ref, g_ref, sh_ref, sc_ref, h_scr)
    v = jax.nn.gelu(jnp.dot(h_scr[...], win_ref[:, width:], preferred_element_type=F32), approximate=True)
    vn_scr[...] = (v * lax.rsqrt(jnp.mean(v * v, axis=-1, keepdims=True) + EPS) * sgg_ref[...]).astype(BF16)
    for g in range(SG_GROUPS):
        cs = slice(g * gd, (g + 1) * gd)
        u = jax.nn.gelu(jnp.dot(h_scr[...], win_ref[:, cs], preferred_element_type=F32), approximate=True)
        bias = jnp.concatenate([bs_ref[g]] * (gd // LANES), axis=1)
        for ch in range(tm // CHUNK):
            rs = slice(ch * CHUNK, (ch + 1) * CHUNK)
            mixed = jnp.dot(ws_ref[g], vn_scr[rs, cs], preferred_element_type=F32) + bias
            sg_scr[rs, cs] = (u[rs] * mixed).astype(BF16)
    for j in range(o_ref.shape[1] // ODD_OUT_TILE):
        cs = slice(j * ODD_OUT_TILE, (j + 1) * ODD_OUT_TILE)
        acc = jnp.dot(sg_scr[...], wout_ref[:, cs], preferred_element_type=F32)
        o_ref[:, cs] = x_ref[:, cs] + gt_ref[:, cs] * acc


def odd_mixer(x, g, shift, scale, gate, w_in, sg_g, ws, bs, w_out, tm=512):
    rows, d = x.shape
    width = w_in.shape[1] // 2
    n = w_out.shape[1]
    bsb = jnp.broadcast_to(bs.astype(F32)[:, :, None], (SG_GROUPS, CHUNK, LANES))
    vec = lambda w: pl.BlockSpec((1, w), lambda i: (0, 0))
    once = pl.Buffered(1)
    return pl.pallas_call(
        _odd_mixer_kernel,
        grid=(rows // tm,),
        in_specs=[pl.BlockSpec((tm, d), lambda i: (i, 0)), vec(d), vec(d), vec(d),
                  pl.BlockSpec((d, 2 * width), lambda i: (0, 0), pipeline_mode=once),
                  vec(width),
                  pl.BlockSpec((SG_GROUPS, CHUNK, CHUNK), lambda i: (0, 0, 0)),
                  pl.BlockSpec((SG_GROUPS, CHUNK, LANES), lambda i: (0, 0, 0)),
                  pl.BlockSpec((width, n), lambda i: (0, 0), pipeline_mode=once),
                  vec(n)],
        out_specs=pl.BlockSpec((tm, n), lambda i: (i, 0)),
        out_shape=jax.ShapeDtypeStruct((rows, n), F32),
        scratch_shapes=[pltpu.VMEM((tm, d), BF16), pltpu.VMEM((tm, width), BF16), pltpu.VMEM((tm, width), BF16)],
        compiler_params=_params("parallel"),
        name="odd_mixer",
    )(x, g, shift, scale, w_in, sg_g, ws.astype(BF16), bsb, w_out, gate)


def _outproj_even_kernel(yt_ref, ya_ref, w_ref, x_ref, gt_ref, o_ref, lhs_scr):
    ng = yt_ref.shape[0]
    c = ng * SUBLANES
    for r in range(yt_ref.shape[1] // SUBLANES):
        sub = yt_ref[:, r * SUBLANES:(r + 1) * SUBLANES, :].reshape(c, LANES)
        lhs_scr[r * LANES:(r + 1) * LANES, 0:c] = sub.T.astype(BF16)
    lhs_scr[:, c:] = ya_ref[...]
    acc = jnp.dot(lhs_scr[...], w_ref[...], preferred_element_type=F32)
    o_ref[...] = x_ref[...] + gt_ref[...] * acc


def out_proj_even(yt, ya, w, x, gate, tm=512):
    rows, aw = ya.shape
    ng = yt.shape[0]
    k, n = w.shape
    return pl.pallas_call(
        _outproj_even_kernel,
        grid=(rows // tm,),
        in_specs=[pl.BlockSpec((ng, SUBLANES * tm // LANES, LANES), lambda i: (0, i, 0)),
                  pl.BlockSpec((tm, aw), lambda i: (i, 0)),
                  pl.BlockSpec((k, n), lambda i: (0, 0), pipeline_mode=pl.Buffered(1)),
                  pl.BlockSpec((tm, n), lambda i: (i, 0)),
                  pl.BlockSpec((1, n), lambda i: (0, 0))],
        out_specs=pl.BlockSpec((tm, n), lambda i: (i, 0)),
        out_shape=jax.ShapeDtypeStruct((rows, n), F32),
        scratch_shapes=[pltpu.VMEM((tm, k), BF16)],
        compiler_params=_params("parallel"),
        name="out_proj_even",
    )(yt, ya, w, x, gate)


def kernel(x, c, ctx, c_ctx, ada_w, ada_b, norm_g, ffn_wg, ffn_wu, ffn_wd, ev_w_in, ev_conv_w, ev_conv_b,
           hy_w1, hy_b1, hy_f1, hy_w2, hy_b2, hy_f2, hy_w3, hy_skip, att_sink, ev_w_out, od_w_in, sg_g, sg_ws,
           sg_bs, od_w_out, final_g):
    assert x.shape[0] == 1 and ada_w.shape[0] == 2, "written for batch 1, depth 2 (even layer then odd layer)"
    _, L, d = x.shape
    hy_width = hy_skip.shape[1]
    hy_in = 3 * hy_width
    kv_w = N_KV_HEADS * HEAD_DIM
    q_end = hy_in + (d - hy_width)

    xs = x[0]
    xc = ctx[0]
    cond8 = jnp.zeros((SUBLANES, d), F32).at[0].set(c[0]).at[1].set(c_ctx)
    mods = ada_mods(cond8, ada_w, ada_b)
    row = lambda v: v.reshape(1, d)
    bf = lambda w: w.astype(BF16)

    mod = mods[0, 0].reshape(N_MOD, 1, d)
    mc = mods[0, 1].reshape(N_MOD, 1, d)
    g = norm_g[0]
    wg, wu, wd = bf(ffn_wg), bf(ffn_wu), ffn_wd
    xs = half_ffn(xs, row(g[0]), mod[0], mod[1], mod[2], wg, wu, wd, 0, 0)
    xc = half_ffn(xc, row(g[0]), mc[0], mc[1], mc[2], wg, wu, wd, 0, 0)
    w_in = bf(ev_w_in[0])
    ut, x0t, qkv = even_in_proj(xs, row(g[1]), mod[3], mod[4], w_in, ev_conv_w[0], ev_conv_b[0])
    kvc = in_proj(xc, row(g[1]), mc[3], mc[4], w_in[:, q_end:])
    ht = hyena_filter_t(L, hy_w1[0], hy_b1[0], hy_f1[0], hy_w2[0], hy_b2[0], hy_f2[0], hy_w3[0])
    yt = hyena_mix(ut, x0t, ht, hy_skip[0])
    ya = window_attention(qkv, kvc, att_sink[0])
    xs = out_proj_even(yt, ya, bf(ev_w_out[0]), xs, mod[5])
    xs = half_ffn(xs, row(g[2]), mod[6], mod[7], mod[8], wg, wu, wd, 0, 1)

    mod = mods[1, 0].reshape(N_MOD, 1, d)
    g = norm_g[1]
    xs = half_ffn(xs, row(g[0]), mod[0], mod[1], mod[2], wg, wu, wd, 1, 0)
    xs = odd_mixer(xs, row(g[1]), mod[3], mod[4], mod[5], bf(od_w_in[0]), row(sg_g[0]), sg_ws[0], sg_bs[0],
                   bf(od_w_out[0]))
    xs = half_ffn(xs, row(g[2]), mod[6], mod[7], mod[8], wg, wu, wd, 1, 1, final_g=row(final_g))
    return xs[None]
```

```python
import functools
import math

import numpy as np
import jax
import jax.numpy as jnp
from jax import lax
from jax.experimental import pallas as pl
from jax.experimental.pallas import tpu as pltpu

F32 = jnp.float32
BF16 = jnp.bfloat16

LANES = 128
SUBLANES = 8
VMEM_LIMIT_BYTES = 58 * 1024 * 1024

EPS = 1e-6
NEG_INF = -1e30
N_MOD = 9
HEAD_DIM = 128
N_KV_HEADS = 2
Q_PER_KV = 4
ATT_BLOCK = 128
GRID_W = 64
ROPE_BASE = 10000.0
HY_BANDS = 16
HY_DECAY_TARGET = 1e-2
HY_FAST_PCT = 0.3
HY_SLOW_PCT = 1.5
SG_GROUPS = 8
CHUNK = 128
HY_GROUP = 2 * SUBLANES


def _params(*sem):
    return pltpu.CompilerParams(dimension_semantics=sem, vmem_limit_bytes=VMEM_LIMIT_BYTES)


NORM_ROWS = 16


def _norm_mod_store(x_ref, g_ref, sh_ref, sc_ref, h_scr, zero_ref=None):
    g = g_ref[...]
    one_plus_scale = 1.0 + sc_ref[...]
    shift = sh_ref[...]

    def body(i, carry):
        rows = pl.ds(pl.multiple_of(i * NORM_ROWS, NORM_ROWS), NORM_ROWS)
        x = x_ref[rows, :]
        y = x * lax.rsqrt(jnp.mean(x * x, axis=-1, keepdims=True) + EPS) * g
        h_scr[rows, :] = (y * one_plus_scale + shift).astype(BF16)
        if zero_ref is not None:
            zero_ref[rows, :] = jnp.zeros((NORM_ROWS, zero_ref.shape[1]), zero_ref.dtype)
        return carry

    lax.fori_loop(0, x_ref.shape[0] // NORM_ROWS, body, 0, unroll=8)


def _ada_kernel(s_ref, w_ref, b_ref, o_ref):
    s = s_ref[...]
    s = (s * jax.nn.sigmoid(s)).astype(BF16)
    o_ref[0] = jnp.dot(s, w_ref[0].astype(BF16), preferred_element_type=F32) + b_ref[0]


def ada_mods(cond8, ada_w, ada_b, tn=1024):
    depth, d, n = ada_w.shape
    return pl.pallas_call(
        _ada_kernel,
        grid=(depth, n // tn),
        in_specs=[
            pl.BlockSpec((SUBLANES, d), lambda l, j: (0, 0)),
            pl.BlockSpec((1, d, tn), lambda l, j: (l, 0, j)),
            pl.BlockSpec((1, 1, tn), lambda l, j: (l, 0, j)),
        ],
        out_specs=pl.BlockSpec((1, SUBLANES, tn), lambda l, j: (l, 0, j)),
        out_shape=jax.ShapeDtypeStruct((depth, SUBLANES, n), F32),
        compiler_params=_params("parallel", "parallel"),
        name="ada_mods",
    )(cond8, ada_w, ada_b.reshape(depth, 1, n))


FFN_ROW_CHUNK = 512


def _ffn_kernel(x_ref, g_ref, sh_ref, sc_ref, gt_ref, wg_ref, wu_ref, wd_ref, *rest, final):
    if final:
        fg_ref, o_ref, h_scr = rest
    else:
        o_ref, h_scr = rest
    f = pl.program_id(1)

    @pl.when(f == 0)
    def _():
        _norm_mod_store(x_ref, g_ref, sh_ref, sc_ref, h_scr, zero_ref=o_ref)

    tm = x_ref.shape[0]
    rc = min(tm, FFN_ROW_CHUNK)
    wd = wd_ref[...].astype(BF16)
    for r in range(tm // rc):
        rs = slice(r * rc, (r + 1) * rc)
        h = h_scr[rs, :]
        gate_act = jnp.dot(h, wg_ref[...], preferred_element_type=F32)
        up = jnp.dot(h, wu_ref[...], preferred_element_type=F32)
        a = (gate_act * jax.nn.sigmoid(gate_act) * up).astype(BF16)
        o_ref[rs, :] += jnp.dot(a, wd, preferred_element_type=F32)

    @pl.when(f == pl.num_programs(1) - 1)
    def _():
        half_gate = 0.5 * gt_ref[...]
        group = 8

        def body(i, carry):
            chunk = lambda k: pl.ds(pl.multiple_of((i * group + k) * NORM_ROWS, NORM_ROWS), NORM_ROWS)
            resid = lambda rows: x_ref[rows, :] + half_gate * o_ref[rows, :]
            if final:
                inv = [lax.rsqrt(jnp.mean(jnp.square(resid(chunk(k))), axis=-1, keepdims=True) + EPS)
                       for k in range(group)]
                for k in range(group):
                    o_ref[chunk(k), :] = resid(chunk(k)) * inv[k] * fg_ref[...]
            else:
                for k in range(group):
                    o_ref[chunk(k), :] = resid(chunk(k))
            return carry

        lax.fori_loop(0, tm // (NORM_ROWS * group), body, 0)


def half_ffn(x, g, shift, scale, gate, wg, wu, wd, layer, which, final_g=None, tm=1024, tf=512):
    rows, d = x.shape
    dff = wg.shape[-1]
    tm = min(tm, rows)
    vec = pl.BlockSpec((1, d), lambda i, f: (0, 0))
    in_specs = [
        pl.BlockSpec((tm, d), lambda i, f: (i, 0)),
        vec, vec, vec, vec,
        pl.BlockSpec((None, None, d, tf), lambda i, f: (layer, which, 0, f)),
        pl.BlockSpec((None, None, d, tf), lambda i, f: (layer, which, 0, f)),
        pl.BlockSpec((None, None, tf, d), lambda i, f: (layer, which, f, 0)),
    ]
    args = [x, g, shift, scale, gate, wg, wu, wd]
    if final_g is not None:
        in_specs.append(vec)
        args.append(final_g)
    return pl.pallas_call(
        functools.partial(_ffn_kernel, final=final_g is not None),
        grid=(rows // tm, dff // tf),
        in_specs=in_specs,
        out_specs=pl.BlockSpec((tm, d), lambda i, f: (i, 0)),
        out_shape=jax.ShapeDtypeStruct((rows, d), F32),
        scratch_shapes=[pltpu.VMEM((tm, d), BF16)],
        compiler_params=_params("parallel", "arbitrary"),
        name="half_ffn",
    )(*args)


def _inproj_kernel(x_ref, g_ref, sh_ref, sc_ref, w_ref, o_ref, h_scr):
    @pl.when(pl.program_id(1) == 0)
    def _():
        _norm_mod_store(x_ref, g_ref, sh_ref, sc_ref, h_scr)

    o_ref[...] = jnp.dot(h_scr[...], w_ref[...], preferred_element_type=F32)


def in_proj(x, g, shift, scale, w, tm=1024, tn=512):
    rows, d = x.shape
    n = w.shape[1]
    tm = min(tm, rows)
    vec = pl.BlockSpec((1, d), lambda i, j: (0, 0))
    return pl.pallas_call(
        _inproj_kernel,
        grid=(rows // tm, n // tn),
        in_specs=[pl.BlockSpec((tm, d), lambda i, j: (i, 0)), vec, vec, vec,
                  pl.BlockSpec((d, tn), lambda i, j: (0, j))],
        out_specs=pl.BlockSpec((tm, tn), lambda i, j: (i, j)),
        out_shape=jax.ShapeDtypeStruct((rows, n), F32),
        scratch_shapes=[pltpu.VMEM((tm, d), BF16)],
        compiler_params=_params("parallel", "arbitrary"),
        name="in_proj",
    )(x, g, shift, scale, w)


HY_CHANNEL_TILE = 512


HALO = 16


def _even_inproj_kernel(x_ref, xp_ref, xn_ref, g_ref, sh_ref, sc_ref, w_ref, cw_ref, cb_ref,
                        ut_ref, x0t_ref, qkv_ref, h_scr, z_scr):
    i = pl.program_id(0)
    tm = x_ref.shape[0]
    ct = HY_CHANNEL_TILE
    width = ut_ref.shape[0] * SUBLANES
    _norm_mod_store(x_ref, g_ref, sh_ref, sc_ref, h_scr.at[pl.ds(HALO, tm), :])

    def halo(ref, valid):
        x = ref[...]
        y = x * lax.rsqrt(jnp.mean(x * x, axis=-1, keepdims=True) + EPS) * g_ref[...]
        return ((y * (1.0 + sc_ref[...]) + sh_ref[...]) * valid).astype(BF16)

    h_scr[0:HALO, :] = halo(xp_ref, jnp.where(i > 0, 1.0, 0.0))
    h_scr[HALO + tm:, :] = halo(xn_ref, jnp.where(i < pl.num_programs(0) - 1, 1.0, 0.0))

    for c in range(width // ct):
        def stream(s):
            cols = slice(s * width + c * ct, s * width + (c + 1) * ct)
            z_scr[...] = jnp.dot(h_scr[...], w_ref[:, cols], preferred_element_type=F32)
            return (z_scr[HALO - 1:HALO - 1 + tm, :] * cw_ref[0:1, cols] + z_scr[HALO:HALO + tm, :] * cw_ref[1:2, cols]
                    + z_scr[HALO + 1:HALO + 1 + tm, :] * cw_ref[2:3, cols] + cb_ref[:, cols])

        x0 = stream(0)
        u = stream(2) * stream(1)
        for val, ref in ((u, ut_ref), (x0, x0t_ref)):
            vt = val.T
            for r in range(tm // LANES):
                ref[c * ct // SUBLANES:(c + 1) * ct // SUBLANES, r * SUBLANES:(r + 1) * SUBLANES, :] = (
                    vt[:, r * LANES:(r + 1) * LANES].reshape(ct // SUBLANES, SUBLANES, LANES))
    qkv_ref[...] = jnp.dot(h_scr[HALO:HALO + tm, :], w_ref[:, 3 * width:], preferred_element_type=F32)


def even_in_proj(x, g, shift, scale, w, conv_w, conv_b, tm=512):
    rows, d = x.shape
    nc = conv_w.shape[1]
    width = nc // 3
    n = w.shape[1] - nc
    vec = pl.BlockSpec((1, d), lambda i: (0, 0))
    per_tile = tm // HALO
    last_halo = rows // HALO - 1
    tiled = jax.ShapeDtypeStruct((width // SUBLANES, SUBLANES * rows // LANES, LANES), F32)
    tiled_spec = pl.BlockSpec((width // SUBLANES, SUBLANES * tm // LANES, LANES), lambda i: (0, i, 0))
    return pl.pallas_call(
        _even_inproj_kernel,
        grid=(rows // tm,),
        in_specs=[pl.BlockSpec((tm, d), lambda i: (i, 0)),
                  pl.BlockSpec((HALO, d), lambda i: (jnp.maximum(i * per_tile - 1, 0), 0)),
                  pl.BlockSpec((HALO, d), lambda i: (jnp.minimum((i + 1) * per_tile, last_halo), 0)),
                  vec, vec, vec,
                  pl.BlockSpec((d, nc + n), lambda i: (0, 0), pipeline_mode=pl.Buffered(1)),
                  pl.BlockSpec((3, nc), lambda i: (0, 0)),
                  pl.BlockSpec((1, nc), lambda i: (0, 0))],
        out_specs=[tiled_spec, tiled_spec, pl.BlockSpec((tm, n), lambda i: (i, 0))],
        out_shape=[tiled, tiled, jax.ShapeDtypeStruct((rows, n), F32)],
        scratch_shapes=[pltpu.VMEM((tm + 2 * HALO, d), BF16), pltpu.VMEM((tm + 2 * HALO, HY_CHANNEL_TILE), F32)],
        compiler_params=_params("parallel"),
        name="even_in_proj",
    )(x, x, x, g, shift, scale, w, conv_w.astype(F32), conv_b.astype(F32).reshape(1, nc))


def _filter_kernel(z_ref, t_ref, w1_ref, b1_ref, f1_ref, w2_ref, b2_ref, f2_ref, w3_ref, dl_ref, ds_ref, o_ref):
    hi = lax.Precision.HIGHEST
    ct = w3_ref.shape[0]
    dec = jnp.exp(-(t_ref[0:1, 0:LANES] * dl_ref[...]))
    for r in range(z_ref.shape[1] // LANES):
        sl = slice(r * LANES, (r + 1) * LANES)
        a1 = jnp.dot(w1_ref[...], z_ref[:, sl], precision=hi, preferred_element_type=F32)
        h1 = jnp.sin(f1_ref[...] * (a1 + b1_ref[...]))
        a2 = jnp.dot(w2_ref[...], h1, precision=hi, preferred_element_type=F32)
        h2 = jnp.sin(f2_ref[...] * (a2 + b2_ref[...]))
        h3 = jnp.dot(w3_ref[...], h2.astype(BF16), preferred_element_type=F32)
        o_ref[:, r * SUBLANES:(r + 1) * SUBLANES, :] = (h3 * dec).reshape(ct // SUBLANES, SUBLANES, LANES)
        dec = dec * ds_ref[...]


def hyena_filter_t(L, w1, b1, f1, w2, b2, f2, w3, tl=1024):
    hid = w1.shape[1]
    c2 = w3.shape[1]
    width = c2 // 2
    t = np.linspace(0.0, 1.0, L, dtype=np.float32)[:, None]
    w = np.float32(2.0 * math.pi / L) * np.arange(L, dtype=np.float32)[:, None]
    bands = np.linspace(1e-4, HY_BANDS - 1, HY_BANDS, dtype=np.float32)[None, :]
    z = np.concatenate([t, np.cos(bands * w), -np.sin(bands * w)], axis=-1)
    emb = z.shape[1]
    embp = -(-emb // SUBLANES) * SUBLANES
    zt = jnp.asarray(np.pad(z.T, ((0, embp - emb), (0, 0))), F32)
    w1t = jnp.pad(w1.astype(F32).T, ((0, 0), (0, embp - emb)))
    lt = math.log(HY_DECAY_TARGET)
    deltas = np.abs(np.linspace(lt / HY_SLOW_PCT, lt / HY_FAST_PCT, width, dtype=np.float32))
    deltas2 = jnp.asarray(np.concatenate([deltas, deltas]), F32)
    chunk_decay = jnp.asarray(np.exp(-np.concatenate([deltas, deltas]).astype(np.float64) * LANES / (L - 1)), F32)
    col = lambda v: jnp.broadcast_to(v.astype(F32)[:, None], (v.shape[0], LANES))
    trow = jnp.asarray(np.broadcast_to(t.T, (SUBLANES, L)), F32)
    tl = min(tl, L)
    full = lambda shape: pl.BlockSpec(shape, lambda i: (0,) * len(shape))
    return pl.pallas_call(
        _filter_kernel,
        grid=(L // tl,),
        in_specs=[
            pl.BlockSpec((embp, tl), lambda i: (0, i)),
            pl.BlockSpec((SUBLANES, tl), lambda i: (0, i)),
            full((hid, embp)), full((hid, LANES)), full((hid, LANES)),
            full((hid, hid)), full((hid, LANES)), full((hid, LANES)),
            full((c2, hid)), full((c2, LANES)), full((c2, LANES)),
        ],
        out_specs=pl.BlockSpec((c2 // SUBLANES, SUBLANES * tl // LANES, LANES), lambda i: (0, i, 0)),
        out_shape=jax.ShapeDtypeStruct((c2 // SUBLANES, SUBLANES * L // LANES, LANES), F32),
        compiler_params=_params("parallel"),
        name="hyena_filter",
    )(zt, trow, w1t, col(b1), col(f1), w2.astype(F32).T, col(b2), col(f2), w3.T.astype(BF16), col(deltas2), col(chunk_decay))


def _dft_constants(h1):
    n_outer = 2 * h1
    n = n_outer * LANES
    bf16_rows = 2 * SUBLANES
    kp = -(-(h1 + 1) // bf16_rows) * bf16_rows
    k1 = np.arange(kp)[:, None]
    n1 = np.arange(h1)[None, :]
    ang_a = 2.0 * np.pi * ((k1 * n1) % n_outer) / n_outer
    fa = np.concatenate([np.cos(ang_a), -np.sin(ang_a)], axis=0)
    n2 = np.arange(LANES)[None, :]
    ang_t = 2.0 * np.pi * ((k1 * n2) % n) / n
    twr, twi = np.cos(ang_t), -np.sin(ang_t)
    a = np.arange(LANES)
    ang_b = 2.0 * np.pi * ((a[:, None] * a[None, :]) % LANES) / LANES
    cb, sb = np.cos(ang_b), np.sin(ang_b)
    fb = np.block([[cb, -sb], [sb, cb]])
    gb = np.block([[cb, sb], [-sb, cb]])
    wk = np.where((k1 == 0) | (k1 == h1), 1.0, 2.0) * (k1 <= h1)
    ga = np.concatenate([(wk * np.cos(ang_a)).T, (-wk * np.sin(ang_a)).T], axis=1)
    as_bf = lambda m: jnp.asarray(m, dtype=F32).astype(BF16)
    return (as_bf(fa), as_bf(twr), as_bf(twi), as_bf(fb), as_bf(gb), as_bf(ga), kp, n)


def _hyena_kernel(skip_ref, u_ref, x0_ref, hf_ref, hb_ref, fa_ref, twr_ref, twi_ref, fb_ref, gb_ref,
                  ga_ref, o_ref, *, h1, kp, inv_n):
    grp = pl.program_id(0)
    row = lax.broadcasted_iota(jnp.int32, (h1, LANES), 0)
    lane = lax.broadcasted_iota(jnp.int32, (h1, LANES), 1)
    first = (row == 0) & (lane == 0)

    def chan(ref, ci):
        return ref[ci // SUBLANES, pl.ds(ci % SUBLANES, h1, stride=SUBLANES), :]

    us, x0s, hfs, hbs, nrm = [], [], [], [], []
    for ci in range(HY_GROUP):
        us.append(chan(u_ref, ci))
        x0s.append(chan(x0_ref, ci))
        hf = chan(hf_ref, ci)
        hb = jnp.where(first, 0.0, chan(hb_ref, ci))
        hfs.append(hf)
        hbs.append(hb)
        ssq = jnp.sum(hf * hf, keepdims=True) + jnp.sum(hb * hb, keepdims=True)
        nrm.append(lax.rsqrt(ssq + EPS))

    fa = fa_ref[...]
    twr, twi = twr_ref[...], twi_ref[...]

    def outer_fwd(mats):
        xc = jnp.concatenate([m.astype(BF16) for m in mats], axis=1)
        a = jnp.dot(fa, xc, preferred_element_type=F32).astype(BF16)
        out = []
        for ci in range(HY_GROUP):
            ar = a[:kp, ci * LANES:(ci + 1) * LANES]
            ai = a[kp:, ci * LANES:(ci + 1) * LANES]
            out.append(jnp.concatenate([ar * twr - ai * twi, ar * twi + ai * twr], axis=1))
        return out

    stacked = jnp.concatenate(outer_fwd(us) + outer_fwd(hfs) + outer_fwd(hbs), axis=0)
    spec = jnp.dot(stacked, fb_ref[...], preferred_element_type=F32).astype(BF16)

    prod = []
    for ci in range(HY_GROUP):
        xu = spec[ci * kp:(ci + 1) * kp]
        xf = spec[(HY_GROUP + ci) * kp:(HY_GROUP + ci + 1) * kp]
        xb = spec[(2 * HY_GROUP + ci) * kp:(2 * HY_GROUP + ci + 1) * kp]
        xr, xi = xu[:, :LANES], xu[:, LANES:]
        kr = xf[:, :LANES] + xb[:, :LANES]
        ki = xf[:, LANES:] - xb[:, LANES:]
        prod.append(jnp.concatenate([xr * kr - xi * ki, xr * ki + xi * kr], axis=1))
    inner = jnp.dot(jnp.concatenate(prod, axis=0), gb_ref[...], preferred_element_type=F32).astype(BF16)

    cols = []
    for ci in range(HY_GROUP):
        b = inner[ci * kp:(ci + 1) * kp]
        br, bi = b[:, :LANES], b[:, LANES:]
        cols.append(jnp.concatenate([br * twr + bi * twi, bi * twr - br * twi], axis=0))
    y = jnp.dot(ga_ref[...], jnp.concatenate(cols, axis=1), preferred_element_type=F32)

    for ci in range(HY_GROUP):
        c = grp * HY_GROUP + ci
        yc = y[:, ci * LANES:(ci + 1) * LANES] * (nrm[ci] * inv_n)
        o_ref[ci // SUBLANES, pl.ds(ci % SUBLANES, h1, stride=SUBLANES), :] = (
            x0s[ci] * (yc + us[ci] * skip_ref[c]))


def hyena_mix(ut, x0t, ht, skip):
    ng, r8, _ = ut.shape
    h1 = r8 // SUBLANES
    fa, twr, twi, fb, gb, ga, kp, n = _dft_constants(h1)
    gps = HY_GROUP // SUBLANES
    steps = ng // gps
    blk = lambda stream: pl.BlockSpec((gps, r8, LANES), lambda g, stream=stream: (g + stream * steps, 0, 0))
    full = lambda a: pl.BlockSpec(a.shape, lambda g: (0,) * a.ndim)
    return pl.pallas_call(
        functools.partial(_hyena_kernel, h1=h1, kp=kp, inv_n=1.0 / n),
        grid=(steps,),
        in_specs=[pl.BlockSpec(memory_space=pltpu.SMEM),
                  blk(0), blk(0), blk(0), blk(1),
                  full(fa), full(twr), full(twi), full(fb), full(gb), full(ga)],
        out_specs=pl.BlockSpec((gps, r8, LANES), lambda g: (g, 0, 0)),
        out_shape=jax.ShapeDtypeStruct((ng, r8, LANES), F32),
        compiler_params=_params("parallel"),
        name="hyena_mix",
    )(skip.astype(F32), ut, x0t, ht, ht, fa, twr, twi, fb, gb, ga)


ATT_Q_BLOCKS = 4


def _attn_kernel(sink_ref, q_ref, *refs, scale):
    nkb = ATT_Q_BLOCKS + 2
    k_refs, v_refs = refs[0:nkb], refs[nkb:2 * nkb]
    cc_refs, ss_refs = refs[2 * nkb:3 * nkb], refs[3 * nkb:4 * nkb]
    kc_ref, vc_ref, o_ref = refs[4 * nkb:]
    step = pl.program_id(0)
    nsteps = pl.num_programs(0)
    hd, blk = HEAD_DIM, ATT_BLOCK
    nt = (((1,), (1,)), ((), ()))

    def rope(x, cc, ss):
        return x * cc + pltpu.roll(x, hd // 2, 1) * ss

    log2e = math.log2(math.e)
    c2 = scale * log2e
    rows = Q_PER_KV * blk
    off = lax.broadcasted_iota(jnp.int32, (rows, blk), 0) & (blk - 1)
    col = lax.broadcasted_iota(jnp.int32, (rows, blk), 1)
    head_of_row = lax.broadcasted_iota(jnp.int32, (rows, 1), 0) // blk
    rowmax = lambda a: jnp.max(a, axis=-1, keepdims=True)
    rowsum = lambda a: jnp.sum(a, axis=-1, keepdims=True)

    for g in range(N_KV_HEADS):
        gs = slice(g * hd, (g + 1) * hd)
        rk = [rope(k_refs[b][:, gs], cc_refs[b][...], ss_refs[b][...]).astype(BF16) for b in range(nkb)]
        vv = [v_refs[b][:, gs].astype(BF16) for b in range(nkb)]
        kcg = kc_ref[:, gs].astype(BF16)
        vcg = vc_ref[:, gs].astype(BF16)
        sink = jnp.zeros((rows, 1), F32)
        for h in range(Q_PER_KV):
            sink = jnp.where(head_of_row == h, sink_ref[g * Q_PER_KV + h] * log2e, sink)
        for sb in range(ATT_Q_BLOCKS):
            qrows = slice(sb * blk, (sb + 1) * blk)
            kb = jnp.concatenate(rk[sb:sb + 3], axis=0)
            vb = jnp.concatenate(vv[sb:sb + 3], axis=0)
            q4 = jnp.concatenate([
                rope(q_ref[qrows, (g * Q_PER_KV + h) * hd:(g * Q_PER_KV + h + 1) * hd],
                     cc_refs[sb + 1][...], ss_refs[sb + 1][...]) * c2
                for h in range(Q_PER_KV)], axis=0).astype(BF16)
            keep_prev = col >= (off + jnp.where(step > 0, 0, blk) if sb == 0 else off)
            keep_next = col <= (off - jnp.where(step < nsteps - 1, 0, blk) if sb == ATT_Q_BLOCKS - 1 else off)
            s_loc = lax.dot_general(q4, kb, nt, preferred_element_type=F32)
            s_ctx = lax.dot_general(q4, kcg, nt, preferred_element_type=F32)
            s_prev = jnp.where(keep_prev, s_loc[:, :blk], NEG_INF)
            s_own = s_loc[:, blk:2 * blk]
            s_next = jnp.where(keep_next, s_loc[:, 2 * blk:], NEG_INF)
            ctx_tiles = [s_ctx[:, j * LANES:(j + 1) * LANES] for j in range(s_ctx.shape[1] // LANES)]
            m = jnp.maximum(rowmax(functools.reduce(jnp.maximum, [s_prev, s_own, s_next] + ctx_tiles)), sink)
            p_prev, p_own, p_next = jnp.exp2(s_prev - m), jnp.exp2(s_own - m), jnp.exp2(s_next - m)
            p_ctx = jnp.exp2(s_ctx - m)
            p_tiles = [p_prev, p_own, p_next] + [p_ctx[:, j * LANES:(j + 1) * LANES] for j in range(len(ctx_tiles))]
            den = jnp.exp2(sink - m) + rowsum(functools.reduce(jnp.add, p_tiles))
            p_loc = jnp.concatenate([p_prev, p_own, p_next], axis=1).astype(BF16)
            o = (jnp.dot(p_ctx.astype(BF16), vcg, preferred_element_type=F32)
                 + jnp.dot(p_loc, vb, preferred_element_type=F32)) * (1.0 / den)
            for h in range(Q_PER_KV):
                o_ref[qrows, (g * Q_PER_KV + h) * hd:(g * Q_PER_KV + h + 1) * hd] = (
                    o[h * blk:(h + 1) * blk].astype(o_ref.dtype))


def window_attention(qkv, kvc, sink):
    L = qkv.shape[0]
    n_ctx = kvc.shape[0]
    nb = L // ATT_BLOCK
    kvw = N_KV_HEADS * HEAD_DIM
    qw = N_KV_HEADS * Q_PER_KV * HEAD_DIM
    kcol, vcol = qw // kvw, qw // kvw + 1
    t = np.arange(L)
    rowp = (t // GRID_W).astype(np.float32)
    colp = (t % GRID_W).astype(np.float32)
    nq = HEAD_DIM // 4
    inv = (ROPE_BASE ** (-np.arange(nq, dtype=np.float32) / nq)).astype(np.float32)
    ang = np.concatenate([rowp[:, None] * inv, colp[:, None] * inv], axis=-1)
    cos, sin = np.cos(ang), np.sin(ang)
    cc = jnp.asarray(np.concatenate([cos, cos], axis=-1), F32)
    ss = jnp.asarray(np.concatenate([-sin, sin], axis=-1), F32)
    nkb = ATT_Q_BLOCKS + 2
    kblock = lambda b: (lambda i: jnp.clip(i * ATT_Q_BLOCKS - 1 + b, 0, nb - 1))
    kspecs = [pl.BlockSpec((ATT_BLOCK, kvw), lambda i, f=kblock(b): (f(i), kcol)) for b in range(nkb)]
    vspecs = [pl.BlockSpec((ATT_BLOCK, kvw), lambda i, f=kblock(b): (f(i), vcol)) for b in range(nkb)]
    tspecs = [pl.BlockSpec((ATT_BLOCK, HEAD_DIM), lambda i, f=kblock(b): (f(i), 0)) for b in range(nkb)]
    tq = ATT_Q_BLOCKS * ATT_BLOCK
    return pl.pallas_call(
        functools.partial(_attn_kernel, scale=HEAD_DIM ** -0.5),
        grid=(L // tq,),
        in_specs=[pl.BlockSpec(memory_space=pltpu.SMEM),
                  pl.BlockSpec((tq, qw), lambda i: (i, 0))]
                 + kspecs + vspecs + tspecs + tspecs
                 + [pl.BlockSpec((n_ctx, kvw), lambda i: (0, 0)),
                    pl.BlockSpec((n_ctx, kvw), lambda i: (0, 1))],
        out_specs=pl.BlockSpec((tq, qw), lambda i: (i, 0)),
        out_shape=jax.ShapeDtypeStruct((L, qw), BF16),
        compiler_params=_params("parallel"),
        name="window_attention",
    )(sink.astype(F32), qkv, *([qkv] * (2 * nkb)), *([cc] * nkb), *([ss] * nkb), kvc, kvc)


ODD_OUT_TILE = 512


def _odd_mixer_kernel(x_ref, g_ref, sh_ref, sc_ref, win_ref, sgg_ref, ws_ref, bs_ref, wout_ref, gt_ref,
                      o_ref, h_scr, vn_scr, sg_scr):
    tm, width = sg_scr.shape
    gd = width // SG_GROUPS
    _norm_mod_store(x_ref, g_ref, sh_ref, sc_ref, h_scr)
    v = jax.nn.gelu(jnp.dot(h_scr[...], win_ref[:, width:], preferred_element_type=F32), approximate=True)
    vn_scr[...] = (v * lax.rsqrt(jnp.mean(v * v, axis=-1, keepdims=True) + EPS) * sgg_ref[...]).astype(BF16)
    for g in range(SG_GROUPS):
        cs = slice(g * gd, (g + 1) * gd)
        u = jax.nn.gelu(jnp.dot(h_scr[...], win_ref[:, cs], preferred_element_type=F32), approximate=True)
        bias = jnp.concatenate([bs_ref[g]] * (gd // LANES), axis=1)
        for ch in range(tm // CHUNK):
            rs = slice(ch * CHUNK, (ch + 1) * CHUNK)
            mixed = jnp.dot(ws_ref[g], vn_scr[rs, cs], preferred_element_type=F32) + bias
            sg_scr[rs, cs] = (u[rs] * mixed).astype(BF16)
    for j in range(o_ref.shape[1] // ODD_OUT_TILE):
        cs = slice(j * ODD_OUT_TILE, (j + 1) * ODD_OUT_TILE)
        acc = jnp.dot(sg_scr[...], wout_ref[:, cs], preferred_element_type=F32)
        o_ref[:, cs] = x_ref[:, cs] + gt_ref[:, cs] * acc


def odd_mixer(x, g, shift, scale, gate, w_in, sg_g, ws, bs, w_out, tm=512):
    rows, d = x.shape
    width = w_in.shape[1] // 2
    n = w_out.shape[1]
    bsb = jnp.broadcast_to(bs.astype(F32)[:, :, None], (SG_GROUPS, CHUNK, LANES))
    vec = lambda w: pl.BlockSpec((1, w), lambda i: (0, 0))
    once = pl.Buffered(1)
    return pl.pallas_call(
        _odd_mixer_kernel,
        grid=(rows // tm,),
        in_specs=[pl.BlockSpec((tm, d), lambda i: (i, 0)), vec(d), vec(d), vec(d),
                  pl.BlockSpec((d, 2 * width), lambda i: (0, 0), pipeline_mode=once),
                  vec(width),
                  pl.BlockSpec((SG_GROUPS, CHUNK, CHUNK), lambda i: (0, 0, 0)),
                  pl.BlockSpec((SG_GROUPS, CHUNK, LANES), lambda i: (0, 0, 0)),
                  pl.BlockSpec((width, n), lambda i: (0, 0), pipeline_mode=once),
                  vec(n)],
        out_specs=pl.BlockSpec((tm, n), lambda i: (i, 0)),
        out_shape=jax.ShapeDtypeStruct((rows, n), F32),
        scratch_shapes=[pltpu.VMEM((tm, d), BF16), pltpu.VMEM((tm, width), BF16), pltpu.VMEM((tm, width), BF16)],
        compiler_params=_params("parallel"),
        name="odd_mixer",
    )(x, g, shift, scale, w_in, sg_g, ws.astype(BF16), bsb, w_out, gate)


def _outproj_even_kernel(yt_ref, ya_ref, w_ref, x_ref, gt_ref, o_ref, lhs_scr):
    ng = yt_ref.shape[0]
    c = ng * SUBLANES
    for r in range(yt_ref.shape[1] // SUBLANES):
        sub = yt_ref[:, r * SUBLANES:(r + 1) * SUBLANES, :].reshape(c, LANES)
        lhs_scr[r * LANES:(r + 1) * LANES, 0:c] = sub.T.astype(BF16)
    lhs_scr[:, c:] = ya_ref[...]
    acc = jnp.dot(lhs_scr[...], w_ref[...], preferred_element_type=F32)
    o_ref[...] = x_ref[...] + gt_ref[...] * acc


def out_proj_even(yt, ya, w, x, gate, tm=512):
    rows, aw = ya.shape
    ng = yt.shape[0]
    k, n = w.shape
    return pl.pallas_call(
        _outproj_even_kernel,
        grid=(rows // tm,),
        in_specs=[pl.BlockSpec((ng, SUBLANES * tm // LANES, LANES), lambda i: (0, i, 0)),
                  pl.BlockSpec((tm, aw), lambda i: (i, 0)),
                  pl.BlockSpec((k, n), lambda i: (0, 0), pipeline_mode=pl.Buffered(1)),
                  pl.BlockSpec((tm, n), lambda i: (i, 0)),
                  pl.BlockSpec((1, n), lambda i: (0, 0))],
        out_specs=pl.BlockSpec((tm, n), lambda i: (i, 0)),
        out_shape=jax.ShapeDtypeStruct((rows, n), F32),
        scratch_shapes=[pltpu.VMEM((tm, k), BF16)],
        compiler_params=_params("parallel"),
        name="out_proj_even",
    )(yt, ya, w, x, gate)


def kernel(x, c, ctx, c_ctx, ada_w, ada_b, norm_g, ffn_wg, ffn_wu, ffn_wd, ev_w_in, ev_conv_w, ev_conv_b,
           hy_w1, hy_b1, hy_f1, hy_w2, hy_b2, hy_f2, hy_w3, hy_skip, att_sink, ev_w_out, od_w_in, sg_g, sg_ws,
           sg_bs, od_w_out, final_g):
    assert x.shape[0] == 1 and ada_w.shape[0] == 2, "written for batch 1, depth 2 (even layer then odd layer)"
    _, L, d = x.shape
    hy_width = hy_skip.shape[1]
    q_end = 3 * hy_width + (d - hy_width)

    xs = x[0]
    xc = ctx[0]
    cond8 = jnp.zeros((SUBLANES, d), F32).at[0].set(c[0]).at[1].set(c_ctx)
    mods = ada_mods(cond8, ada_w, ada_b)
    row = lambda v: v.reshape(1, d)
    bf = lambda w: w.astype(BF16)

    mod = mods[0, 0].reshape(N_MOD, 1, d)
    mc = mods[0, 1].reshape(N_MOD, 1, d)
    g = norm_g[0]
    wg, wu, wd = bf(ffn_wg), bf(ffn_wu), ffn_wd
    xs = half_ffn(xs, row(g[0]), mod[0], mod[1], mod[2], wg, wu, wd, 0, 0)
    xc = half_ffn(xc, row(g[0]), mc[0], mc[1], mc[2], wg, wu, wd, 0, 0)
    w_in = bf(ev_w_in[0])
    ut, x0t, qkv = even_in_proj(xs, row(g[1]), mod[3], mod[4], w_in, ev_conv_w[0], ev_conv_b[0])
    kvc = in_proj(xc, row(g[1]), mc[3], mc[4], w_in[:, q_end:])
    ht = hyena_filter_t(L, hy_w1[0], hy_b1[0], hy_f1[0], hy_w2[0], hy_b2[0], hy_f2[0], hy_w3[0])
    yt = hyena_mix(ut, x0t, ht, hy_skip[0])
    ya = window_attention(qkv, kvc, att_sink[0])
    xs = out_proj_even(yt, ya, bf(ev_w_out[0]), xs, mod[5])
    xs = half_ffn(xs, row(g[2]), mod[6], mod[7], mod[8], wg, wu, wd, 0, 1)

    mod = mods[1, 0].reshape(N_MOD, 1, d)
    g = norm_g[1]
    xs = half_ffn(xs, row(g[0]), mod[0], mod[1], mod[2], wg, wu, wd, 1, 0)
    xs = odd_mixer(xs, row(g[1]), mod[3], mod[4], mod[5], bf(od_w_in[0]), row(sg_g[0]), sg_ws[0], sg_bs[0],
                   bf(od_w_out[0]))
    xs = half_ffn(xs, row(g[2]), mod[6], mod[7], mod[8], wg, wu, wd, 1, 1, final_g=row(final_g))
    return xs[None]
```

```python
import functools
import math

import numpy as np
import jax
import jax.numpy as jnp
from jax import lax
from jax.experimental import pallas as pl
from jax.experimental.pallas import tpu as pltpu

F32 = jnp.float32
BF16 = jnp.bfloat16

LANES = 128
SUBLANES = 8
VMEM_LIMIT_BYTES = 58 * 1024 * 1024

EPS = 1e-6
NEG_INF = -1e30
N_MOD = 9
HEAD_DIM = 128
N_KV_HEADS = 2
Q_PER_KV = 4
ATT_BLOCK = 128
GRID_W = 64
ROPE_BASE = 10000.0
HY_BANDS = 16
HY_DECAY_TARGET = 1e-2
HY_FAST_PCT = 0.3
HY_SLOW_PCT = 1.5
SG_GROUPS = 8
CHUNK = 128
HY_GROUP = 4 * SUBLANES


def _params(*sem):
    return pltpu.CompilerParams(dimension_semantics=sem, vmem_limit_bytes=VMEM_LIMIT_BYTES)


NORM_ROWS = 16


def _norm_mod_store(x_ref, g_ref, sh_ref, sc_ref, h_scr, zero_ref=None):
    g = g_ref[...]
    one_plus_scale = 1.0 + sc_ref[...]
    shift = sh_ref[...]

    def body(i, carry):
        rows = pl.ds(pl.multiple_of(i * NORM_ROWS, NORM_ROWS), NORM_ROWS)
        x = x_ref[rows, :]
        y = x * lax.rsqrt(jnp.mean(x * x, axis=-1, keepdims=True) + EPS) * g
        h_scr[rows, :] = (y * one_plus_scale + shift).astype(BF16)
        if zero_ref is not None:
            zero_ref[rows, :] = jnp.zeros((NORM_ROWS, zero_ref.shape[1]), zero_ref.dtype)
        return carry

    lax.fori_loop(0, x_ref.shape[0] // NORM_ROWS, body, 0, unroll=8)


def _ada_kernel(s_ref, w_ref, b_ref, o_ref):
    s = s_ref[...]
    s = (s * jax.nn.sigmoid(s)).astype(BF16)
    o_ref[0] = jnp.dot(s, w_ref[0].astype(BF16), preferred_element_type=F32) + b_ref[0]


def ada_mods(cond8, ada_w, ada_b, tn=1024):
    depth, d, n = ada_w.shape
    return pl.pallas_call(
        _ada_kernel,
        grid=(depth, n // tn),
        in_specs=[
            pl.BlockSpec((SUBLANES, d), lambda l, j: (0, 0)),
            pl.BlockSpec((1, d, tn), lambda l, j: (l, 0, j)),
            pl.BlockSpec((1, 1, tn), lambda l, j: (l, 0, j)),
        ],
        out_specs=pl.BlockSpec((1, SUBLANES, tn), lambda l, j: (l, 0, j)),
        out_shape=jax.ShapeDtypeStruct((depth, SUBLANES, n), F32),
        compiler_params=_params("parallel", "parallel"),
        name="ada_mods",
    )(cond8, ada_w, ada_b.reshape(depth, 1, n))


FFN_ROW_CHUNK = 512


def _ffn_kernel(x_ref, g_ref, sh_ref, sc_ref, gt_ref, wg_ref, wu_ref, wd_ref, *rest, final):
    if final:
        fg_ref, o_ref, h_scr = rest
    else:
        o_ref, h_scr = rest
    f = pl.program_id(1)

    @pl.when(f == 0)
    def _():
        _norm_mod_store(x_ref, g_ref, sh_ref, sc_ref, h_scr, zero_ref=o_ref)

    tm = x_ref.shape[0]
    rc = min(tm, FFN_ROW_CHUNK)
    wd = wd_ref[...].astype(BF16)
    for r in range(tm // rc):
        rs = slice(r * rc, (r + 1) * rc)
        h = h_scr[rs, :]
        gate_act = jnp.dot(h, wg_ref[...], preferred_element_type=F32)
        up = jnp.dot(h, wu_ref[...], preferred_element_type=F32)
        a = (gate_act * jax.nn.sigmoid(gate_act) * up).astype(BF16)
        o_ref[rs, :] += jnp.dot(a, wd, preferred_element_type=F32)

    @pl.when(f == pl.num_programs(1) - 1)
    def _():
        half_gate = 0.5 * gt_ref[...]
        group = 8

        def body(i, carry):
            chunk = lambda k: pl.ds(pl.multiple_of((i * group + k) * NORM_ROWS, NORM_ROWS), NORM_ROWS)
            resid = lambda rows: x_ref[rows, :] + half_gate * o_ref[rows, :]
            if final:
                inv = [lax.rsqrt(jnp.mean(jnp.square(resid(chunk(k))), axis=-1, keepdims=True) + EPS)
                       for k in range(group)]
                for k in range(group):
                    o_ref[chunk(k), :] = resid(chunk(k)) * inv[k] * fg_ref[...]
            else:
                for k in range(group):
                    o_ref[chunk(k), :] = resid(chunk(k))
            return carry

        lax.fori_loop(0, tm // (NORM_ROWS * group), body, 0)


def half_ffn(x, g, shift, scale, gate, wg, wu, wd, layer, which, final_g=None, tm=1024, tf=512):
    rows, d = x.shape
    dff = wg.shape[-1]
    tm = min(tm, rows)
    vec = pl.BlockSpec((1, d), lambda i, f: (0, 0))
    in_specs = [
        pl.BlockSpec((tm, d), lambda i, f: (i, 0)),
        vec, vec, vec, vec,
        pl.BlockSpec((None, None, d, tf), lambda i, f: (layer, which, 0, f)),
        pl.BlockSpec((None, None, d, tf), lambda i, f: (layer, which, 0, f)),
        pl.BlockSpec((None, None, tf, d), lambda i, f: (layer, which, f, 0)),
    ]
    args = [x, g, shift, scale, gate, wg, wu, wd]
    if final_g is not None:
        in_specs.append(vec)
        args.append(final_g)
    return pl.pallas_call(
        functools.partial(_ffn_kernel, final=final_g is not None),
        grid=(rows // tm, dff // tf),
        in_specs=in_specs,
        out_specs=pl.BlockSpec((tm, d), lambda i, f: (i, 0)),
        out_shape=jax.ShapeDtypeStruct((rows, d), F32),
        scratch_shapes=[pltpu.VMEM((tm, d), BF16)],
        compiler_params=_params("parallel", "arbitrary"),
        name="half_ffn",
    )(*args)


def _inproj_kernel(x_ref, g_ref, sh_ref, sc_ref, w_ref, o_ref, h_scr):
    @pl.when(pl.program_id(1) == 0)
    def _():
        _norm_mod_store(x_ref, g_ref, sh_ref, sc_ref, h_scr)

    o_ref[...] = jnp.dot(h_scr[...], w_ref[...], preferred_element_type=F32)


def in_proj(x, g, shift, scale, w, tm=1024, tn=512):
    rows, d = x.shape
    n = w.shape[1]
    tm = min(tm, rows)
    vec = pl.BlockSpec((1, d), lambda i, j: (0, 0))
    return pl.pallas_call(
        _inproj_kernel,
        grid=(rows // tm, n // tn),
        in_specs=[pl.BlockSpec((tm, d), lambda i, j: (i, 0)), vec, vec, vec,
                  pl.BlockSpec((d, tn), lambda i, j: (0, j))],
        out_specs=pl.BlockSpec((tm, tn), lambda i, j: (i, j)),
        out_shape=jax.ShapeDtypeStruct((rows, n), F32),
        scratch_shapes=[pltpu.VMEM((tm, d), BF16)],
        compiler_params=_params("parallel", "arbitrary"),
        name="in_proj",
    )(x, g, shift, scale, w)


HY_CHANNEL_TILE = 512


HALO = 16


def _even_inproj_kernel(x_ref, xp_ref, xn_ref, g_ref, sh_ref, sc_ref, w_ref, cw_ref, cb_ref,
                        ut_ref, x0t_ref, qkv_ref, h_scr, z_scr):
    i = pl.program_id(0)
    tm = x_ref.shape[0]
    ct = HY_CHANNEL_TILE
    width = ut_ref.shape[0] * SUBLANES
    _norm_mod_store(x_ref, g_ref, sh_ref, sc_ref, h_scr.at[pl.ds(HALO, tm), :])

    def halo(ref, valid):
        x = ref[...]
        y = x * lax.rsqrt(jnp.mean(x * x, axis=-1, keepdims=True) + EPS) * g_ref[...]
        return ((y * (1.0 + sc_ref[...]) + sh_ref[...]) * valid).astype(BF16)

    h_scr[0:HALO, :] = halo(xp_ref, jnp.where(i > 0, 1.0, 0.0))
    h_scr[HALO + tm:, :] = halo(xn_ref, jnp.where(i < pl.num_programs(0) - 1, 1.0, 0.0))

    for c in range(width // ct):
        def stream(s):
            cols = slice(s * width + c * ct, s * width + (c + 1) * ct)
            z_scr[...] = jnp.dot(h_scr[...], w_ref[:, cols], preferred_element_type=F32)
            return (z_scr[HALO - 1:HALO - 1 + tm, :] * cw_ref[0:1, cols] + z_scr[HALO:HALO + tm, :] * cw_ref[1:2, cols]
                    + z_scr[HALO + 1:HALO + 1 + tm, :] * cw_ref[2:3, cols] + cb_ref[:, cols])

        x0 = stream(0)
        u = stream(2) * stream(1)
        for val, ref in ((u, ut_ref), (x0, x0t_ref)):
            vt = val.T
            for r in range(tm // LANES):
                ref[c * ct // SUBLANES:(c + 1) * ct // SUBLANES, r * SUBLANES:(r + 1) * SUBLANES, :] = (
                    vt[:, r * LANES:(r + 1) * LANES].reshape(ct // SUBLANES, SUBLANES, LANES))
    qkv_ref[...] = jnp.dot(h_scr[HALO:HALO + tm, :], w_ref[:, 3 * width:], preferred_element_type=F32)


def even_in_proj(x, g, shift, scale, w, conv_w, conv_b, tm=512):
    rows, d = x.shape
    nc = conv_w.shape[1]
    width = nc // 3
    n = w.shape[1] - nc
    vec = pl.BlockSpec((1, d), lambda i: (0, 0))
    per_tile = tm // HALO
    last_halo = rows // HALO - 1
    tiled = jax.ShapeDtypeStruct((width // SUBLANES, SUBLANES * rows // LANES, LANES), F32)
    tiled_spec = pl.BlockSpec((width // SUBLANES, SUBLANES * tm // LANES, LANES), lambda i: (0, i, 0))
    return pl.pallas_call(
        _even_inproj_kernel,
        grid=(rows // tm,),
        in_specs=[pl.BlockSpec((tm, d), lambda i: (i, 0)),
                  pl.BlockSpec((HALO, d), lambda i: (jnp.maximum(i * per_tile - 1, 0), 0)),
                  pl.BlockSpec((HALO, d), lambda i: (jnp.minimum((i + 1) * per_tile, last_halo), 0)),
                  vec, vec, vec,
                  pl.BlockSpec((d, nc + n), lambda i: (0, 0), pipeline_mode=pl.Buffered(1)),
                  pl.BlockSpec((3, nc), lambda i: (0, 0)),
                  pl.BlockSpec((1, nc), lambda i: (0, 0))],
        out_specs=[tiled_spec, tiled_spec, pl.BlockSpec((tm, n), lambda i: (i, 0))],
        out_shape=[tiled, tiled, jax.ShapeDtypeStruct((rows, n), F32)],
        scratch_shapes=[pltpu.VMEM((tm + 2 * HALO, d), BF16), pltpu.VMEM((tm + 2 * HALO, HY_CHANNEL_TILE), F32)],
        compiler_params=_params("parallel"),
        name="even_in_proj",
    )(x, x, x, g, shift, scale, w, conv_w.astype(F32), conv_b.astype(F32).reshape(1, nc))


def _filter_kernel(z_ref, t_ref, w1_ref, b1_ref, f1_ref, w2_ref, b2_ref, f2_ref, w3_ref, dl_ref, ds_ref, o_ref):
    hi = lax.Precision.HIGHEST
    ct = w3_ref.shape[0]
    dec = jnp.exp(-(t_ref[0:1, 0:LANES] * dl_ref[...]))
    for r in range(z_ref.shape[1] // LANES):
        sl = slice(r * LANES, (r + 1) * LANES)
        a1 = jnp.dot(w1_ref[...], z_ref[:, sl], precision=hi, preferred_element_type=F32)
        h1 = jnp.sin(f1_ref[...] * (a1 + b1_ref[...]))
        a2 = jnp.dot(w2_ref[...], h1, precision=hi, preferred_element_type=F32)
        h2 = jnp.sin(f2_ref[...] * (a2 + b2_ref[...]))
        h3 = jnp.dot(w3_ref[...], h2.astype(BF16), preferred_element_type=F32)
        o_ref[:, r * SUBLANES:(r + 1) * SUBLANES, :] = (h3 * dec).reshape(ct // SUBLANES, SUBLANES, LANES)
        dec = dec * ds_ref[...]


def hyena_filter_t(L, w1, b1, f1, w2, b2, f2, w3, tl=1024):
    hid = w1.shape[1]
    c2 = w3.shape[1]
    width = c2 // 2
    t = np.linspace(0.0, 1.0, L, dtype=np.float32)[:, None]
    w = np.float32(2.0 * math.pi / L) * np.arange(L, dtype=np.float32)[:, None]
    bands = np.linspace(1e-4, HY_BANDS - 1, HY_BANDS, dtype=np.float32)[None, :]
    z = np.concatenate([t, np.cos(bands * w), -np.sin(bands * w)], axis=-1)
    emb = z.shape[1]
    embp = -(-emb // SUBLANES) * SUBLANES
    zt = jnp.asarray(np.pad(z.T, ((0, embp - emb), (0, 0))), F32)
    w1t = jnp.pad(w1.astype(F32).T, ((0, 0), (0, embp - emb)))
    lt = math.log(HY_DECAY_TARGET)
    deltas = np.abs(np.linspace(lt / HY_SLOW_PCT, lt / HY_FAST_PCT, width, dtype=np.float32))
    deltas2 = jnp.asarray(np.concatenate([deltas, deltas]), F32)
    chunk_decay = jnp.asarray(np.exp(-np.concatenate([deltas, deltas]).astype(np.float64) * LANES / (L - 1)), F32)
    col = lambda v: jnp.broadcast_to(v.astype(F32)[:, None], (v.shape[0], LANES))
    trow = jnp.asarray(np.broadcast_to(t.T, (SUBLANES, L)), F32)
    tl = min(tl, L)
    full = lambda shape: pl.BlockSpec(shape, lambda i: (0,) * len(shape))
    return pl.pallas_call(
        _filter_kernel,
        grid=(L // tl,),
        in_specs=[
            pl.BlockSpec((embp, tl), lambda i: (0, i)),
            pl.BlockSpec((SUBLANES, tl), lambda i: (0, i)),
            full((hid, embp)), full((hid, LANES)), full((hid, LANES)),
            full((hid, hid)), full((hid, LANES)), full((hid, LANES)),
            full((c2, hid)), full((c2, LANES)), full((c2, LANES)),
        ],
        out_specs=pl.BlockSpec((c2 // SUBLANES, SUBLANES * tl // LANES, LANES), lambda i: (0, i, 0)),
        out_shape=jax.ShapeDtypeStruct((c2 // SUBLANES, SUBLANES * L // LANES, LANES), F32),
        compiler_params=_params("parallel"),
        name="hyena_filter",
    )(zt, trow, w1t, col(b1), col(f1), w2.astype(F32).T, col(b2), col(f2), w3.T.astype(BF16), col(deltas2), col(chunk_decay))


def _dft_constants(h1):
    n_outer = 2 * h1
    n = n_outer * LANES
    bf16_rows = 2 * SUBLANES
    kp = -(-(h1 + 1) // bf16_rows) * bf16_rows
    k1 = np.arange(kp)[:, None]
    n1 = np.arange(h1)[None, :]
    ang_a = 2.0 * np.pi * ((k1 * n1) % n_outer) / n_outer
    fa = np.concatenate([np.cos(ang_a), -np.sin(ang_a)], axis=0)
    n2 = np.arange(LANES)[None, :]
    ang_t = 2.0 * np.pi * ((k1 * n2) % n) / n
    twr, twi = np.cos(ang_t), -np.sin(ang_t)
    a = np.arange(LANES)
    ang_b = 2.0 * np.pi * ((a[:, None] * a[None, :]) % LANES) / LANES
    cb, sb = np.cos(ang_b), np.sin(ang_b)
    fb = np.block([[cb, -sb], [sb, cb]])
    gb = np.block([[cb, sb], [-sb, cb]])
    wk = np.where((k1 == 0) | (k1 == h1), 1.0, 2.0) * (k1 <= h1)
    ga = np.concatenate([(wk * np.cos(ang_a)).T, (-wk * np.sin(ang_a)).T], axis=1)
    as_bf = lambda m: jnp.asarray(m, dtype=F32).astype(BF16)
    return (as_bf(fa), as_bf(twr), as_bf(twi), as_bf(fb), as_bf(gb), as_bf(ga), kp, n)


def _hyena_kernel(skip_ref, u_ref, x0_ref, hf_ref, hb_ref, fa_ref, twr_ref, twi_ref, fb_ref, gb_ref,
                  ga_ref, o_ref, *, h1, kp, inv_n):
    grp = pl.program_id(0)
    row = lax.broadcasted_iota(jnp.int32, (h1, LANES), 0)
    lane = lax.broadcasted_iota(jnp.int32, (h1, LANES), 1)
    first = (row == 0) & (lane == 0)

    def chan(ref, ci):
        return ref[ci // SUBLANES, pl.ds(ci % SUBLANES, h1, stride=SUBLANES), :]

    us, x0s, hfs, hbs, nrm = [], [], [], [], []
    for ci in range(HY_GROUP):
        us.append(chan(u_ref, ci))
        x0s.append(chan(x0_ref, ci))
        hf = chan(hf_ref, ci)
        hb = jnp.where(first, 0.0, chan(hb_ref, ci))
        hfs.append(hf)
        hbs.append(hb)
        ssq = jnp.sum(hf * hf, keepdims=True) + jnp.sum(hb * hb, keepdims=True)
        nrm.append(lax.rsqrt(ssq + EPS))

    fa = fa_ref[...]
    twr, twi = twr_ref[...], twi_ref[...]

    def outer_fwd(mats):
        xc = jnp.concatenate([m.astype(BF16) for m in mats], axis=1)
        a = jnp.dot(fa, xc, preferred_element_type=F32).astype(BF16)
        out = []
        for ci in range(HY_GROUP):
            ar = a[:kp, ci * LANES:(ci + 1) * LANES]
            ai = a[kp:, ci * LANES:(ci + 1) * LANES]
            out.append(jnp.concatenate([ar * twr - ai * twi, ar * twi + ai * twr], axis=1))
        return out

    stacked = jnp.concatenate(outer_fwd(us) + outer_fwd(hfs) + outer_fwd(hbs), axis=0)
    spec = jnp.dot(stacked, fb_ref[...], preferred_element_type=F32).astype(BF16)

    prod = []
    for ci in range(HY_GROUP):
        xu = spec[ci * kp:(ci + 1) * kp]
        xf = spec[(HY_GROUP + ci) * kp:(HY_GROUP + ci + 1) * kp]
        xb = spec[(2 * HY_GROUP + ci) * kp:(2 * HY_GROUP + ci + 1) * kp]
        xr, xi = xu[:, :LANES], xu[:, LANES:]
        kr = xf[:, :LANES] + xb[:, :LANES]
        ki = xf[:, LANES:] - xb[:, LANES:]
        prod.append(jnp.concatenate([xr * kr - xi * ki, xr * ki + xi * kr], axis=1))
    inner = jnp.dot(jnp.concatenate(prod, axis=0), gb_ref[...], preferred_element_type=F32).astype(BF16)

    cols = []
    for ci in range(HY_GROUP):
        b = inner[ci * kp:(ci + 1) * kp]
        br, bi = b[:, :LANES], b[:, LANES:]
        cols.append(jnp.concatenate([br * twr + bi * twi, bi * twr - br * twi], axis=0))
    y = jnp.dot(ga_ref[...], jnp.concatenate(cols, axis=1), preferred_element_type=F32)

    for ci in range(HY_GROUP):
        c = grp * HY_GROUP + ci
        yc = y[:, ci * LANES:(ci + 1) * LANES] * (nrm[ci] * inv_n)
        o_ref[ci // SUBLANES, pl.ds(ci % SUBLANES, h1, stride=SUBLANES), :] = (
            x0s[ci] * (yc + us[ci] * skip_ref[c]))


def hyena_mix(ut, x0t, ht, skip):
    ng, r8, _ = ut.shape
    h1 = r8 // SUBLANES
    fa, twr, twi, fb, gb, ga, kp, n = _dft_constants(h1)
    gps = HY_GROUP // SUBLANES
    steps = ng // gps
    blk = lambda stream: pl.BlockSpec((gps, r8, LANES), lambda g, stream=stream: (g + stream * steps, 0, 0))
    full = lambda a: pl.BlockSpec(a.shape, lambda g: (0,) * a.ndim)
    return pl.pallas_call(
        functools.partial(_hyena_kernel, h1=h1, kp=kp, inv_n=1.0 / n),
        grid=(steps,),
        in_specs=[pl.BlockSpec(memory_space=pltpu.SMEM),
                  blk(0), blk(0), blk(0), blk(1),
                  full(fa), full(twr), full(twi), full(fb), full(gb), full(ga)],
        out_specs=pl.BlockSpec((gps, r8, LANES), lambda g: (g, 0, 0)),
        out_shape=jax.ShapeDtypeStruct((ng, r8, LANES), F32),
        compiler_params=_params("parallel"),
        name="hyena_mix",
    )(skip.astype(F32), ut, x0t, ht, ht, fa, twr, twi, fb, gb, ga)


ATT_Q_BLOCKS = 4


def _attn_kernel(sink_ref, q_ref, *refs, scale):
    nkb = ATT_Q_BLOCKS + 2
    k_refs, v_refs = refs[0:nkb], refs[nkb:2 * nkb]
    cc_refs, ss_refs = refs[2 * nkb:3 * nkb], refs[3 * nkb:4 * nkb]
    kc_ref, vc_ref, o_ref = refs[4 * nkb:]
    step = pl.program_id(0)
    nsteps = pl.num_programs(0)
    hd, blk = HEAD_DIM, ATT_BLOCK
    nt = (((1,), (1,)), ((), ()))

    def rope(x, cc, ss):
        return x * cc + pltpu.roll(x, hd // 2, 1) * ss

    log2e = math.log2(math.e)
    c2 = scale * log2e
    rows = Q_PER_KV * blk
    off = lax.broadcasted_iota(jnp.int32, (rows, blk), 0) & (blk - 1)
    col = lax.broadcasted_iota(jnp.int32, (rows, blk), 1)
    head_of_row = lax.broadcasted_iota(jnp.int32, (rows, 1), 0) // blk
    rowmax = lambda a: jnp.max(a, axis=-1, keepdims=True)
    rowsum = lambda a: jnp.sum(a, axis=-1, keepdims=True)

    for g in range(N_KV_HEADS):
        gs = slice(g * hd, (g + 1) * hd)
        rk = [rope(k_refs[b][:, gs], cc_refs[b][...], ss_refs[b][...]).astype(BF16) for b in range(nkb)]
        vv = [v_refs[b][:, gs].astype(BF16) for b in range(nkb)]
        kcg = kc_ref[:, gs].astype(BF16)
        vcg = vc_ref[:, gs].astype(BF16)
        sink = jnp.zeros((rows, 1), F32)
        for h in range(Q_PER_KV):
            sink = jnp.where(head_of_row == h, sink_ref[g * Q_PER_KV + h] * log2e, sink)
        for sb in range(ATT_Q_BLOCKS):
            qrows = slice(sb * blk, (sb + 1) * blk)
            kb = jnp.concatenate(rk[sb:sb + 3], axis=0)
            vb = jnp.concatenate(vv[sb:sb + 3], axis=0)
            q4 = jnp.concatenate([
                rope(q_ref[qrows, (g * Q_PER_KV + h) * hd:(g * Q_PER_KV + h + 1) * hd],
                     cc_refs[sb + 1][...], ss_refs[sb + 1][...]) * c2
                for h in range(Q_PER_KV)], axis=0).astype(BF16)
            keep_prev = col >= (off + jnp.where(step > 0, 0, blk) if sb == 0 else off)
            keep_next = col <= (off - jnp.where(step < nsteps - 1, 0, blk) if sb == ATT_Q_BLOCKS - 1 else off)
            s_loc = lax.dot_general(q4, kb, nt, preferred_element_type=F32)
            s_ctx = lax.dot_general(q4, kcg, nt, preferred_element_type=F32)
            s_prev = jnp.where(keep_prev, s_loc[:, :blk], NEG_INF)
            s_own = s_loc[:, blk:2 * blk]
            s_next = jnp.where(keep_next, s_loc[:, 2 * blk:], NEG_INF)
            ctx_tiles = [s_ctx[:, j * LANES:(j + 1) * LANES] for j in range(s_ctx.shape[1] // LANES)]
            m = jnp.maximum(rowmax(functools.reduce(jnp.maximum, [s_prev, s_own, s_next] + ctx_tiles)), sink)
            p_prev, p_own, p_next = jnp.exp2(s_prev - m), jnp.exp2(s_own - m), jnp.exp2(s_next - m)
            p_ctx = jnp.exp2(s_ctx - m)
            p_tiles = [p_prev, p_own, p_next] + [p_ctx[:, j * LANES:(j + 1) * LANES] for j in range(len(ctx_tiles))]
            den = jnp.exp2(sink - m) + rowsum(functools.reduce(jnp.add, p_tiles))
            p_loc = jnp.concatenate([p_prev, p_own, p_next], axis=1).astype(BF16)
            o = (jnp.dot(p_ctx.astype(BF16), vcg, preferred_element_type=F32)
                 + jnp.dot(p_loc, vb, preferred_element_type=F32)) * (1.0 / den)
            for h in range(Q_PER_KV):
                o_ref[qrows, (g * Q_PER_KV + h) * hd:(g * Q_PER_KV + h + 1) * hd] = (
                    o[h * blk:(h + 1) * blk].astype(o_ref.dtype))


def window_attention(qkv, kvc, sink):
    L = qkv.shape[0]
    n_ctx = kvc.shape[0]
    nb = L // ATT_BLOCK
    kvw = N_KV_HEADS * HEAD_DIM
    qw = N_KV_HEADS * Q_PER_KV * HEAD_DIM
    kcol, vcol = qw // kvw, qw // kvw + 1
    t = np.arange(L)
    rowp = (t // GRID_W).astype(np.float32)
    colp = (t % GRID_W).astype(np.float32)
    nq = HEAD_DIM // 4
    inv = (ROPE_BASE ** (-np.arange(nq, dtype=np.float32) / nq)).astype(np.float32)
    ang = np.concatenate([rowp[:, None] * inv, colp[:, None] * inv], axis=-1)
    cos, sin = np.cos(ang), np.sin(ang)
    cc = jnp.asarray(np.concatenate([cos, cos], axis=-1), F32)
    ss = jnp.asarray(np.concatenate([-sin, sin], axis=-1), F32)
    nkb = ATT_Q_BLOCKS + 2
    kblock = lambda b: (lambda i: jnp.clip(i * ATT_Q_BLOCKS - 1 + b, 0, nb - 1))
    kspecs = [pl.BlockSpec((ATT_BLOCK, kvw), lambda i, f=kblock(b): (f(i), kcol)) for b in range(nkb)]
    vspecs = [pl.BlockSpec((ATT_BLOCK, kvw), lambda i, f=kblock(b): (f(i), vcol)) for b in range(nkb)]
    tspecs = [pl.BlockSpec((ATT_BLOCK, HEAD_DIM), lambda i, f=kblock(b): (f(i), 0)) for b in range(nkb)]
    tq = ATT_Q_BLOCKS * ATT_BLOCK
    return pl.pallas_call(
        functools.partial(_attn_kernel, scale=HEAD_DIM ** -0.5),
        grid=(L // tq,),
        in_specs=[pl.BlockSpec(memory_space=pltpu.SMEM),
                  pl.BlockSpec((tq, qw), lambda i: (i, 0))]
                 + kspecs + vspecs + tspecs + tspecs
                 + [pl.BlockSpec((n_ctx, kvw), lambda i: (0, 0)),
                    pl.BlockSpec((n_ctx, kvw), lambda i: (0, 1))],
        out_specs=pl.BlockSpec((tq, qw), lambda i: (i, 0)),
        out_shape=jax.ShapeDtypeStruct((L, qw), BF16),
        compiler_params=_params("parallel"),
        name="window_attention",
    )(sink.astype(F32), qkv, *([qkv] * (2 * nkb)), *([cc] * nkb), *([ss] * nkb), kvc, kvc)


ODD_OUT_TILE = 512


def _odd_mixer_kernel(x_ref, g_ref, sh_ref, sc_ref, win_ref, sgg_ref, ws_ref, bs_ref, wout_ref, gt_ref,
                      o_ref, h_scr, vn_scr, sg_scr):
    tm, width = sg_scr.shape
    gd = width // SG_GROUPS
    _norm_mod_store(x_ref, g_ref, sh_ref, sc_ref, h_scr)
    v = jax.nn.gelu(jnp.dot(h_scr[...], win_ref[:, width:], preferred_element_type=F32), approximate=True)
    vn_scr[...] = (v * lax.rsqrt(jnp.mean(v * v, axis=-1, keepdims=True) + EPS) * sgg_ref[...]).astype(BF16)
    for g in range(SG_GROUPS):
        cs = slice(g * gd, (g + 1) * gd)
        u = jax.nn.gelu(jnp.dot(h_scr[...], win_ref[:, cs], preferred_element_type=F32), approximate=True)
        bias = jnp.concatenate([bs_ref[g]] * (gd // LANES), axis=1)
        for ch in range(tm // CHUNK):
            rs = slice(ch * CHUNK, (ch + 1) * CHUNK)
            mixed = jnp.dot(ws_ref[g], vn_scr[rs, cs], preferred_element_type=F32) + bias
            sg_scr[rs, cs] = (u[rs] * mixed).astype(BF16)
    for j in range(o_ref.shape[1] // ODD_OUT_TILE):
        cs = slice(j * ODD_OUT_TILE, (j + 1) * ODD_OUT_TILE)
        acc = jnp.dot(sg_scr[...], wout_ref[:, cs], preferred_element_type=F32)
        o_ref[:, cs] = x_ref[:, cs] + gt_ref[:, cs] * acc


def odd_mixer(x, g, shift, scale, gate, w_in, sg_g, ws, bs, w_out, tm=512):
    rows, d = x.shape
    width = w_in.shape[1] // 2
    n = w_out.shape[1]
    bsb = jnp.broadcast_to(bs.astype(F32)[:, :, None], (SG_GROUPS, CHUNK, LANES))
    vec = lambda w: pl.BlockSpec((1, w), lambda i: (0, 0))
    once = pl.Buffered(1)
    return pl.pallas_call(
        _odd_mixer_kernel,
        grid=(rows // tm,),
        in_specs=[pl.BlockSpec((tm, d), lambda i: (i, 0)), vec(d), vec(d), vec(d),
                  pl.BlockSpec((d, 2 * width), lambda i: (0, 0), pipeline_mode=once),
                  vec(width),
                  pl.BlockSpec((SG_GROUPS, CHUNK, CHUNK), lambda i: (0, 0, 0)),
                  pl.BlockSpec((SG_GROUPS, CHUNK, LANES), lambda i: (0, 0, 0)),
                  pl.BlockSpec((width, n), lambda i: (0, 0), pipeline_mode=once),
                  vec(n)],
        out_specs=pl.BlockSpec((tm, n), lambda i: (i, 0)),
        out_shape=jax.ShapeDtypeStruct((rows, n), F32),
        scratch_shapes=[pltpu.VMEM((tm, d), BF16), pltpu.VMEM((tm, width), BF16), pltpu.VMEM((tm, width), BF16)],
        compiler_params=_params("parallel"),
        name="odd_mixer",
    )(x, g, shift, scale, w_in, sg_g, ws.astype(BF16), bsb, w_out, gate)


def _outproj_even_kernel(yt_ref, ya_ref, w_ref, x_ref, gt_ref, o_ref, lhs_scr):
    ng = yt_ref.shape[0]
    c = ng * SUBLANES
    for r in range(yt_ref.shape[1] // SUBLANES):
        sub = yt_ref[:, r * SUBLANES:(r + 1) * SUBLANES, :].reshape(c, LANES)
        lhs_scr[r * LANES:(r + 1) * LANES, 0:c] = sub.T.astype(BF16)
    lhs_scr[:, c:] = ya_ref[...]
    acc = jnp.dot(lhs_scr[...], w_ref[...], preferred_element_type=F32)
    o_ref[...] = x_ref[...] + gt_ref[...] * acc


def out_proj_even(yt, ya, w, x, gate, tm=512):
    rows, aw = ya.shape
    ng = yt.shape[0]
    k, n = w.shape
    return pl.pallas_call(
        _outproj_even_kernel,
        grid=(rows // tm,),
        in_specs=[pl.BlockSpec((ng, SUBLANES * tm // LANES, LANES), lambda i: (0, i, 0)),
                  pl.BlockSpec((tm, aw), lambda i: (i, 0)),
                  pl.BlockSpec((k, n), lambda i: (0, 0), pipeline_mode=pl.Buffered(1)),
                  pl.BlockSpec((tm, n), lambda i: (i, 0)),
                  pl.BlockSpec((1, n), lambda i: (0, 0))],
        out_specs=pl.BlockSpec((tm, n), lambda i: (i, 0)),
        out_shape=jax.ShapeDtypeStruct((rows, n), F32),
        scratch_shapes=[pltpu.VMEM((tm, k), BF16)],
        compiler_params=_params("parallel"),
        name="out_proj_even",
    )(yt, ya, w, x, gate)


def kernel(x, c, ctx, c_ctx, ada_w, ada_b, norm_g, ffn_wg, ffn_wu, ffn_wd, ev_w_in, ev_conv_w, ev_conv_b,
           hy_w1, hy_b1, hy_f1, hy_w2, hy_b2, hy_f2, hy_w3, hy_skip, att_sink, ev_w_out, od_w_in, sg_g, sg_ws,
           sg_bs, od_w_out, final_g):
    assert x.shape[0] == 1 and ada_w.shape[0] == 2, "written for batch 1, depth 2 (even layer then odd layer)"
    _, L, d = x.shape
    hy_width = hy_skip.shape[1]
    q_end = 3 * hy_width + (d - hy_width)

    xs = x[0]
    xc = ctx[0]
    cond8 = jnp.zeros((SUBLANES, d), F32).at[0].set(c[0]).at[1].set(c_ctx)
    mods = ada_mods(cond8, ada_w, ada_b)
    row = lambda v: v.reshape(1, d)
    bf = lambda w: w.astype(BF16)

    mod = mods[0, 0].reshape(N_MOD, 1, d)
    mc = mods[0, 1].reshape(N_MOD, 1, d)
    g = norm_g[0]
    wg, wu, wd = bf(ffn_wg), bf(ffn_wu), ffn_wd
    xs = half_ffn(xs, row(g[0]), mod[0], mod[1], mod[2], wg, wu, wd, 0, 0)
    xc = half_ffn(xc, row(g[0]), mc[0], mc[1], mc[2], wg, wu, wd, 0, 0)
    w_in = bf(ev_w_in[0])
    ut, x0t, qkv = even_in_proj(xs, row(g[1]), mod[3], mod[4], w_in, ev_conv_w[0], ev_conv_b[0])
    kvc = in_proj(xc, row(g[1]), mc[3], mc[4], w_in[:, q_end:])
    ht = hyena_filter_t(L, hy_w1[0], hy_b1[0], hy_f1[0], hy_w2[0], hy_b2[0], hy_f2[0], hy_w3[0])
    yt = hyena_mix(ut, x0t, ht, hy_skip[0])
    ya = window_attention(qkv, kvc, att_sink[0])
    xs = out_proj_even(yt, ya, bf(ev_w_out[0]), xs, mod[5])
    xs = half_ffn(xs, row(g[2]), mod[6], mod[7], mod[8], wg, wu, wd, 0, 1)

    mod = mods[1, 0].reshape(N_MOD, 1, d)
    g = norm_g[1]
    xs = half_ffn(xs, row(g[0]), mod[0], mod[1], mod[2], wg, wu, wd, 1, 0)
    xs = odd_mixer(xs, row(g[1]), mod[3], mod[4], mod[5], bf(od_w_in[0]), row(sg_g[0]), sg_ws[0], sg_bs[0],
                   bf(od_w_out[0]))
    xs = half_ffn(xs, row(g[2]), mod[6], mod[7], mod[8], wg, wu, wd, 1, 1, final_g=row(final_g))
    return xs[None]
```
